```python
import jax, jax.numpy as jnp
from jax import lax
import numpy as np

D_MODEL = 1024
BATCH = 4
SEQ = 8192
DEPTH = 2
DEC_BATCH = 32
DEC_SEQ = 1
PAST_LEN = 16384
PAGE_SIZE = 128

N_META = 16
HEAD_DIM = 64
RWKV_WIDTH = D_MODEL // 2
RWKV_HEADS = RWKV_WIDTH // HEAD_DIM
ATTN_WIDTH = D_MODEL - RWKV_WIDTH
ATTN_HEADS = ATTN_WIDTH // HEAD_DIM
KV_HEADS = ATTN_HEADS // 2
W_LORA = 64
A_LORA = 64
G_LORA = 128
SHIFT_W = 3 * RWKV_WIDTH + W_LORA + A_LORA + G_LORA
IDX_HEADS = 8
IDX_DIM = 64
ATTN_PROJ_W = ATTN_HEADS * HEAD_DIM + 2 * KV_HEADS * HEAD_DIM + IDX_HEADS * IDX_DIM + IDX_DIM + IDX_HEADS
P_TOTAL = SHIFT_W + ATTN_PROJ_W
TOPK_MAX = 256
ROPE_THETA = 500000.0
PEER_HEADS = 8
PEER_DKEY = 128
N_KEYS = 128
N_EXPERTS = N_KEYS * N_KEYS
PEER_TOPK = 16
DEEPNORM_ALPHA = (2.0 * DEPTH) ** 0.25
DEEPNORM_BETA = (8.0 * DEPTH) ** -0.25
Q_BLOCK = 128
PEER_BLOCK = 128
LN_EPS = 1e-5
GN_EPS = 64e-5
RW_SPLITS = [RWKV_WIDTH, 2 * RWKV_WIDTH, 3 * RWKV_WIDTH, 3 * RWKV_WIDTH + W_LORA, 3 * RWKV_WIDTH + W_LORA + A_LORA]
Q_W = ATTN_HEADS * HEAD_DIM
KV_W = KV_HEADS * HEAD_DIM
AT_SPLITS = [Q_W, Q_W + KV_W, Q_W + 2 * KV_W, Q_W + 2 * KV_W + IDX_HEADS * IDX_DIM, Q_W + 2 * KV_W + IDX_HEADS * IDX_DIM + IDX_DIM]

kernel_name = "hymba_rwkv7_dsa_peer_step"


def layer_norm(x, g, b, eps=LN_EPS):
    xf = x.astype(jnp.float32)
    mu = xf.mean(-1, keepdims=True)
    var = jnp.mean(jnp.square(xf - mu), -1, keepdims=True)
    return ((xf - mu) * lax.rsqrt(var + eps) * g.astype(jnp.float32) + b.astype(jnp.float32)).astype(x.dtype)


def deepnorm_residual(x, f, g, b):
    return layer_norm(DEEPNORM_ALPHA * x + f.astype(x.dtype), g, b)


def partial_rotary(x, pos):
    rot = x.shape[-1] // 4
    half = rot // 2
    inv = ROPE_THETA ** (-jnp.arange(half, dtype=jnp.float32) * 2.0 / rot)
    ang = pos.astype(jnp.float32)[:, None] * inv[None, :]
    cos = jnp.cos(ang)[:, None, :]
    sin = jnp.sin(ang)[:, None, :]
    xf = x.astype(jnp.float32)
    x1, x2 = xf[..., :half], xf[..., half:rot]
    out = jnp.concatenate([x1 * cos - x2 * sin, x2 * cos + x1 * sin, xf[..., rot:]], axis=-1)
    return out.astype(x.dtype)


def wkv_recurrence(r, decay, k, v, kk, a, s0):
    def step(s, inp):
        r_t, w_t, k_t, v_t, kk_t, a_t = inp
        s_kk = jnp.einsum('bhvk,bhk->bhv', s, kk_t)
        s = (s * w_t[:, :, None, :]
             - s_kk[..., None] * (kk_t * a_t)[:, :, None, :]
             + v_t[..., None] * k_t[:, :, None, :])
        return s, jnp.einsum('bhvk,bhk->bhv', s, r_t)
    xs = tuple(jnp.swapaxes(t, 0, 1) for t in (r, decay, k, v, kk, a))
    s_final, o = lax.scan(step, s0, xs)
    return jnp.swapaxes(o, 0, 1), s_final


def rwkv_group(p_rw, prev_row, s0, mu, w0, w_up, a0, a_up, g_up, k_k, k_a, r_k, lnx_g, lnx_b):
    B, T, _ = p_rw.shape
    f32 = jnp.float32
    pf = p_rw.astype(f32)
    prev = jnp.concatenate([prev_row.astype(f32)[:, None], pf[:, :-1]], axis=1)
    xs = pf + mu.astype(f32) * (prev - pf)
    r, k, v, wd, ad, gd = jnp.split(xs, RW_SPLITS, axis=-1)
    w_raw = w0.astype(f32) + jnp.tanh(wd) @ w_up.astype(f32)
    decay = jnp.exp(-jnp.exp(-jax.nn.softplus(-w_raw) - 0.5))
    a = jax.nn.sigmoid(a0.astype(f32) + ad @ a_up.astype(f32))
    g = jax.nn.sigmoid(gd) @ g_up.astype(f32)
    heads = lambda t: t.reshape(B, T, RWKV_HEADS, HEAD_DIM)
    kk = heads(k * k_k.astype(f32))
    kk = kk / jnp.maximum(jnp.sqrt(jnp.sum(jnp.square(kk), -1, keepdims=True)), 1e-12)
    k = k * (1.0 + (a - 1.0) * k_a.astype(f32))
    rh, kh, vh = heads(r), heads(k), heads(v)
    o, s_final = wkv_recurrence(rh, heads(decay), kh, vh, kk, heads(a), s0.astype(f32))
    mean = o.mean(-1, keepdims=True)
    var = jnp.mean(jnp.square(o - mean), -1, keepdims=True)
    o = (o - mean) * lax.rsqrt(var + GN_EPS) * lnx_g.astype(f32).reshape(RWKV_HEADS, HEAD_DIM) \
        + lnx_b.astype(f32).reshape(RWKV_HEADS, HEAD_DIM)
    bonus = jnp.sum(rh * kh * r_k.astype(f32), -1, keepdims=True) * vh
    out = (o + bonus).reshape(B, T, RWKV_WIDTH) * g
    return out.astype(p_rw.dtype), s_final, p_rw[:, -1]


def attn_project(p_at, pos, idx_g, idx_b):
    B, T, _ = p_at.shape
    q, k, v, qi, ki, wi = jnp.split(p_at, AT_SPLITS, axis=-1)
    q = partial_rotary(q.reshape(B, T, ATTN_HEADS, HEAD_DIM), pos)
    k = partial_rotary(k.reshape(B, T, KV_HEADS, HEAD_DIM), pos)
    v = v.reshape(B, T, KV_HEADS, HEAD_DIM)
    qi = partial_rotary(qi.reshape(B, T, IDX_HEADS, IDX_DIM), pos)
    ki = partial_rotary(layer_norm(ki, idx_g, idx_b)[:, :, None, :], pos)[:, :, 0]
    return q, k, v, qi, ki, wi


def dsa_select(qi, wi, ki_all, q_pos, n_sel):
    f32 = jnp.float32
    s = jnp.einsum('bqhd,bkd->bqhk', qi.astype(f32), ki_all.astype(f32)) * IDX_DIM ** -0.5
    score = jnp.einsum('bqh,bqhk->bqk', wi.astype(f32) * IDX_HEADS ** -0.5, jax.nn.relu(s))
    key_pos = jnp.arange(ki_all.shape[1], dtype=jnp.int32)
    score = jnp.where(key_pos[None, None, :] <= q_pos[None, :, None], score, -jnp.inf)
    _, sel = lax.top_k(score, n_sel)
    valid = sel <= q_pos[None, :, None]
    return sel, valid


def sparse_attend(q, k_sel, v_sel, valid):
    B, Tq, H, hd = q.shape
    qg = q.reshape(B, Tq, KV_HEADS, H // KV_HEADS, hd).astype(jnp.float32)
    logits = jnp.einsum('bqkgd,bqskd->bqkgs', qg, k_sel.astype(jnp.float32)) * hd ** -0.5
    logits = jnp.where(valid[:, :, None, None, :], logits, -jnp.inf)
    p = jax.nn.softmax(logits, axis=-1)
    o = jnp.einsum('bqkgs,bqskd->bqkgd', p, v_sel.astype(jnp.float32))
    return o.reshape(B, Tq, H * hd).astype(q.dtype)


def dsa_prompt_attend(q, k, v, qi, ki, wi):
    B, T = q.shape[:2]
    n_sel = min(TOPK_MAX, T // 4)
    nb = -(-T // Q_BLOCK)
    pad = nb * Q_BLOCK - T

    def blocks(t):
        t = jnp.pad(t, [(0, 0), (0, pad)] + [(0, 0)] * (t.ndim - 2))
        return jnp.moveaxis(t.reshape((B, nb, Q_BLOCK) + t.shape[2:]), 1, 0)

    pos_b = jnp.arange(nb * Q_BLOCK, dtype=jnp.int32).reshape(nb, Q_BLOCK)
    take = jax.vmap(lambda rows, ix: rows[ix])

    def one_block(args):
        qb, qib, wib, pb = args
        sel, valid = dsa_select(qib, wib, ki, pb, n_sel)
        return sparse_attend(qb, take(k, sel), take(v, sel), valid)

    o = lax.map(one_block, (blocks(q), blocks(qi), blocks(wi), pos_b))
    return jnp.moveaxis(o, 0, 1).reshape(B, nb * Q_BLOCK, ATTN_WIDTH)[:, :T]


def dsa_sample_attend(q, k, v, qi, ki, wi, pos, cache_k, cache_v, cache_idx_k, layer, page_table):
    Bd, Tq = q.shape[:2]
    past = page_table.shape[1] * PAGE_SIZE
    ki_past = cache_idx_k[layer, page_table].reshape(Bd, past, IDX_DIM)
    ki_all = jnp.concatenate([ki_past.astype(ki.dtype), ki], axis=1)
    n_sel = min(TOPK_MAX, (past + Tq) // 4)
    sel, valid = dsa_select(qi, wi, ki_all, pos, n_sel)
    past_ix = jnp.minimum(sel, past - 1)
    page = jax.vmap(lambda pt, ix: pt[ix])(page_table, past_ix // PAGE_SIZE)
    off = past_ix % PAGE_SIZE
    is_new = (sel >= past)[..., None, None]
    new_ix = jnp.clip(sel - past, 0, Tq - 1)
    take = jax.vmap(lambda rows, ix: rows[ix])
    k_sel = jnp.where(is_new, take(k, new_ix), cache_k[layer, page, off].astype(k.dtype))
    v_sel = jnp.where(is_new, take(v, new_ix), cache_v[layer, page, off].astype(v.dtype))
    return sparse_attend(q, k_sel, v_sel, valid)


def peer_ffn(x, wq, subkeys, u, v):
    B, T, D = x.shape
    n = B * T
    nb = -(-n // PEER_BLOCK)
    xb_all = jnp.pad(x.reshape(n, D), ((0, nb * PEER_BLOCK - n), (0, 0))).reshape(nb, PEER_BLOCK, D)
    half = PEER_DKEY // 2

    def one_block(xb):
        q = (xb @ wq).reshape(PEER_BLOCK, PEER_HEADS, PEER_DKEY).astype(jnp.float32)
        s1 = jnp.einsum('nhd,kd->nhk', q[..., :half], subkeys[0].astype(jnp.float32))
        s2 = jnp.einsum('nhd,kd->nhk', q[..., half:], subkeys[1].astype(jnp.float32))
        t1, i1 = lax.top_k(s1, PEER_TOPK)
        t2, i2 = lax.top_k(s2, PEER_TOPK)
        comb = (t1[..., :, None] + t2[..., None, :]).reshape(PEER_BLOCK, PEER_HEADS, PEER_TOPK * PEER_TOPK)
        ts, ic = lax.top_k(comb, PEER_TOPK)
        e = (jnp.take_along_axis(i1, ic // PEER_TOPK, axis=-1) * N_KEYS
             + jnp.take_along_axis(i2, ic % PEER_TOPK, axis=-1))
        gate = jax.nn.softmax(ts, axis=-1)
        h = jax.nn.gelu(jnp.einsum('nd,nhed->nhe', xb, u[e]).astype(jnp.float32))
        return jnp.einsum('nhe,nhed->nd', (gate * h).astype(xb.dtype), v[e])

    y = lax.map(one_block, xb_all).reshape(nb * PEER_BLOCK, D)[:n]
    return y.reshape(B, T, D)


def merge_groups(rw_out, at_out, w_out):
    return jnp.einsum('btc,cd->btd', jnp.concatenate([rw_out, at_out], axis=-1), w_out)


def setup_inputs(seed: int = 0) -> dict:
    key = jax.random.key(seed)
    ks = iter(list(jax.random.split(key, 40)))
    nrm = lambda shape, scale=1.0: jax.random.normal(next(ks), shape, jnp.float32) * scale
    n_pages = PAST_LEN // PAGE_SIZE
    used = DEC_BATCH * n_pages
    n_pool = used + max(1, used // 4)
    page_table = jax.random.permutation(next(ks), n_pool)[:used].reshape(DEC_BATCH, n_pages).astype(jnp.int32)
    L = DEPTH
    return {
        "x_prompt": nrm((BATCH, SEQ, D_MODEL)),
        "x_sample": nrm((DEC_BATCH, DEC_SEQ, D_MODEL)),
        "cache_k": nrm((L, n_pool, PAGE_SIZE, KV_HEADS, HEAD_DIM)),
        "cache_v": nrm((L, n_pool, PAGE_SIZE, KV_HEADS, HEAD_DIM)),
        "cache_idx_k": nrm((L, n_pool, PAGE_SIZE, IDX_DIM)),
        "state_wkv": nrm((L, DEC_BATCH, RWKV_HEADS, HEAD_DIM, HEAD_DIM), 0.3),
        "state_shift": nrm((L, DEC_BATCH, SHIFT_W)),
        "page_table": page_table,
        "meta_tokens": nrm((N_META, D_MODEL)),
        "w_in": nrm((L, D_MODEL, P_TOTAL), D_MODEL ** -0.5),
        "shift_mu": jax.random.uniform(next(ks), (L, SHIFT_W), jnp.float32),
        "decay_w0": nrm((L, RWKV_WIDTH), 0.5) - 0.5,
        "decay_up": nrm((L, W_LORA, RWKV_WIDTH), 0.1 * W_LORA ** -0.5),
        "iclr_a0": nrm((L, RWKV_WIDTH), 0.1),
        "iclr_up": nrm((L, A_LORA, RWKV_WIDTH), 0.5 * A_LORA ** -0.5),
        "gate_up": nrm((L, G_LORA, RWKV_WIDTH), G_LORA ** -0.5),
        "k_k": 0.85 + nrm((L, RWKV_WIDTH), 0.05),
        "k_a": 1.0 + nrm((L, RWKV_WIDTH), 0.05),
        "r_k": nrm((L, RWKV_HEADS, HEAD_DIM), 0.1),
        "lnx_g": 1.0 + nrm((L, RWKV_WIDTH), 0.02),
        "lnx_b": nrm((L, RWKV_WIDTH), 0.02),
        "idx_ln_g": 1.0 + nrm((L, IDX_DIM), 0.02),
        "idx_ln_b": nrm((L, IDX_DIM), 0.02),
        "w_out": nrm((L, D_MODEL, D_MODEL), DEEPNORM_BETA * D_MODEL ** -0.5),
        "ln1_g": 1.0 + nrm((L, D_MODEL), 0.02),
        "ln1_b": nrm((L, D_MODEL), 0.02),
        "ln2_g": 1.0 + nrm((L, D_MODEL), 0.02),
        "ln2_b": nrm((L, D_MODEL), 0.02),
        "peer_wq": nrm((L, D_MODEL, PEER_HEADS * PEER_DKEY), D_MODEL ** -0.5),
        "peer_subkeys": nrm((L, 2, N_KEYS, PEER_DKEY // 2), (PEER_DKEY // 2) ** -0.5),
        "peer_u": nrm((L, N_EXPERTS, D_MODEL), D_MODEL ** -0.5),
        "peer_v": nrm((L, N_EXPERTS, D_MODEL), DEEPNORM_BETA * D_MODEL ** -0.5),
    }


def reference(x_prompt, x_sample, cache_k, cache_v, cache_idx_k, state_wkv, state_shift, page_table,
              meta_tokens, w_in, shift_mu, decay_w0, decay_up, iclr_a0, iclr_up, gate_up, k_k, k_a, r_k,
              lnx_g, lnx_b, idx_ln_g, idx_ln_b, w_out, ln1_g, ln1_b, ln2_g, ln2_b,
              peer_wq, peer_subkeys, peer_u, peer_v):
    B = x_prompt.shape[0]
    T = x_prompt.shape[1] + N_META
    xp = jnp.concatenate([jnp.broadcast_to(meta_tokens.astype(x_prompt.dtype)[None], (B, N_META, D_MODEL)),
                          x_prompt], axis=1)
    pos_p = jnp.arange(T, dtype=jnp.int32)
    xs = x_sample
    Bd, Ts = xs.shape[:2]
    past = page_table.shape[1] * PAGE_SIZE
    pos_s = past + jnp.arange(Ts, dtype=jnp.int32)

    k_p, v_p, ki_p, wkv_p, sh_p = [], [], [], [], []
    k_s, v_s, ki_s, wkv_s, sh_s = [], [], [], [], []
    for l in range(DEPTH):
        rw_params = (shift_mu[l], decay_w0[l], decay_up[l], iclr_a0[l], iclr_up[l], gate_up[l],
                     k_k[l], k_a[l], r_k[l], lnx_g[l], lnx_b[l])
        peer_params = (peer_wq[l], peer_subkeys[l], peer_u[l], peer_v[l])

        proj = jnp.einsum('btd,dp->btp', xp, w_in[l])
        rw_out, wkv_fin, shift_last = rwkv_group(
            proj[..., :SHIFT_W], jnp.zeros((B, SHIFT_W), xp.dtype),
            jnp.zeros((B, RWKV_HEADS, HEAD_DIM, HEAD_DIM), jnp.float32), *rw_params)
        q, k, v, qi, ki, wi = attn_project(proj[..., SHIFT_W:], pos_p, idx_ln_g[l], idx_ln_b[l])
        at_out = dsa_prompt_attend(q, k, v, qi, ki, wi)
        xp = deepnorm_residual(xp, merge_groups(rw_out, at_out, w_out[l]), ln1_g[l], ln1_b[l])
        xp = deepnorm_residual(xp, peer_ffn(xp, *peer_params), ln2_g[l], ln2_b[l])
        k_p.append(k); v_p.append(v); ki_p.append(ki); wkv_p.append(wkv_fin); sh_p.append(shift_last)

        proj = jnp.einsum('btd,dp->btp', xs, w_in[l])
        rw_out, wkv_fin, shift_last = rwkv_group(proj[..., :SHIFT_W], state_shift[l], state_wkv[l], *rw_params)
        q, k, v, qi, ki, wi = attn_project(proj[..., SHIFT_W:], pos_s, idx_ln_g[l], idx_ln_b[l])
        at_out = dsa_sample_attend(q, k, v, qi, ki, wi, pos_s, cache_k, cache_v, cache_idx_k, l, page_table)
        xs = deepnorm_residual(xs, merge_groups(rw_out, at_out, w_out[l]), ln1_g[l], ln1_b[l])
        xs = deepnorm_residual(xs, peer_ffn(xs, *peer_params), ln2_g[l], ln2_b[l])
        k_s.append(k); v_s.append(v); ki_s.append(ki); wkv_s.append(wkv_fin); sh_s.append(shift_last)

    y_prompt = xp[:, N_META:]
    y_sample = xs
    k_rows_prompt = jnp.stack(k_p)
    v_rows_prompt = jnp.stack(v_p)
    idx_rows_prompt = jnp.stack(ki_p)
    wkv_prompt = jnp.stack(wkv_p).astype(state_wkv.dtype)
    shift_prompt = jnp.stack(sh_p).astype(state_shift.dtype)
    k_rows_sample = jnp.stack(k_s)
    v_rows_sample = jnp.stack(v_s)
    idx_rows_sample = jnp.stack(ki_s)
    wkv_sample = jnp.stack(wkv_s).astype(state_wkv.dtype)
    shift_sample = jnp.stack(sh_s).astype(state_shift.dtype)
    return (y_prompt, y_sample, k_rows_prompt, v_rows_prompt, idx_rows_prompt, wkv_prompt, shift_prompt,
            k_rows_sample, v_rows_sample, idx_rows_sample, wkv_sample, shift_sample)
```

```python
import functools
import math

import jax
import jax.numpy as jnp
from jax import lax
from jax.experimental import pallas as pl
from jax.experimental.pallas import tpu as pltpu

F32 = jnp.float32
BF16 = jnp.bfloat16
I32 = jnp.int32

D_MODEL = 1024
N_META = 16
HEAD_DIM = 64
RWKV_WIDTH = D_MODEL // 2
RWKV_HEADS = RWKV_WIDTH // HEAD_DIM
ATTN_WIDTH = D_MODEL - RWKV_WIDTH
ATTN_HEADS = ATTN_WIDTH // HEAD_DIM
KV_HEADS = ATTN_HEADS // 2
KV_W = KV_HEADS * HEAD_DIM
W_LORA = 64
A_LORA = 64
G_LORA = 128
LORA_W = W_LORA + A_LORA + G_LORA
SHIFT_W = 3 * RWKV_WIDTH + LORA_W
IDX_HEADS = 8
IDX_DIM = 64
ATTN_PROJ_W = ATTN_WIDTH + 2 * KV_W + IDX_HEADS * IDX_DIM + IDX_DIM + IDX_HEADS
TOPK_MAX = 256
ROPE_THETA = 500000.0
ROT = HEAD_DIM // 4
ROT_HALF = ROT // 2
PEER_HEADS = 8
PEER_DKEY = 128
N_KEYS = 128
PEER_TOPK = 16
PEER_SLOTS = PEER_HEADS * PEER_TOPK
DEPTH = 2
DEEPNORM_ALPHA = (2.0 * DEPTH) ** 0.25
PAGE_SIZE = 128
LN_EPS = 1e-5
GN_EPS = 64e-5

LANE = 128
SUBLANE = 8
Q_BLOCK = 128
INT_MIN = -(2 ** 31)
NEG_BIG = -1e30
VMEM_LIMIT = 56 * 1024 * 1024
AT_PAD_W = 1664
KI_OFF = ATTN_WIDTH + 2 * KV_W + IDX_HEADS * IDX_DIM
PAGES_PER_STEP = 8

_NT = (((1,), (1,)), ((), ()))


def _cparams(*sem):
    return pltpu.CompilerParams(dimension_semantics=sem, vmem_limit_bytes=VMEM_LIMIT)


def _split2(x):
    hi = x.astype(BF16)
    lo = (x - hi.astype(F32)).astype(BF16)
    return hi, lo


def _split3(x):
    hi = x.astype(BF16)
    r1 = x - hi.astype(F32)
    mid = r1.astype(BF16)
    lo = (r1 - mid.astype(F32)).astype(BF16)
    return hi, mid, lo


def _dot3(a, b, dims=None):
    ah, al = _split2(a)
    bh, bl = _split2(b)
    if dims is None:
        d = lambda p, q: jnp.dot(p, q, preferred_element_type=F32)
    else:
        d = lambda p, q: lax.dot_general(p, q, dims, preferred_element_type=F32)
    return d(ah, bh) + d(al, bh) + d(ah, bl)


def _dot_sel(x, m):
    h, mid, lo = _split3(x)
    d = lambda p: jnp.dot(p, m, preferred_element_type=F32)
    return d(h) + d(mid) + d(lo)


def _f2key(x):
    x = jnp.where(x == 0.0, 0.0, x)
    b = lax.bitcast_convert_type(x, I32)
    return b ^ ((b >> 31) & 0x7FFFFFFF)


def _layer_norm(z, g, b):
    mu = jnp.mean(z, axis=-1, keepdims=True)
    zc = z - mu
    var = jnp.mean(zc * zc, axis=-1, keepdims=True)
    return zc * lax.rsqrt(var + LN_EPS) * g + b


def _mm_kernel(x_ref, w_ref, o_ref):
    o_ref[...] = jnp.dot(x_ref[...].astype(BF16), w_ref[...], preferred_element_type=F32)


def _matmul(x, w, tm):
    m, k = x.shape
    n = w.shape[1]
    tm = min(tm, m)
    return pl.pallas_call(
        _mm_kernel,
        grid=(m // tm,),
        in_specs=[pl.BlockSpec((tm, k), lambda i: (i, 0)),
                  pl.BlockSpec((k, n), lambda i: (0, 0))],
        out_specs=pl.BlockSpec((tm, n), lambda i: (i, 0)),
        out_shape=jax.ShapeDtypeStruct((m, n), F32),
        compiler_params=_cparams("parallel"),
        name="proj_matmul",
    )(x, w)


def _rwkv_prep_kernel(t_real, tt, shift, p_ref, prev_ref, mu_ref, w0_ref, a0_ref, lwh_ref, lwl_ref,
                      kk_ref, ka_ref, rk_ref, bd_ref,
                      r_o, w_o, kt_o, kko_o, b_o, v_o, bonus_o, g_o, carry_ref):
    j = pl.program_id(1)
    pf = p_ref[0]
    if shift:
        @pl.when(j == 0)
        def _():
            carry_ref[...] = prev_ref[0]
        row = lax.broadcasted_iota(I32, pf.shape, 0)
        prev = jnp.where(row == 0, carry_ref[...], pltpu.roll(pf, 1, 0))
        carry_ref[...] = pf[tt - 1:tt, :]
    else:
        prev = prev_ref[0]
    xs = pf + mu_ref[...] * (prev - pf)
    r = xs[:, 0:RWKV_WIDTH]
    k = xs[:, RWKV_WIDTH:2 * RWKV_WIDTH]
    v = xs[:, 2 * RWKV_WIDTH:3 * RWKV_WIDTH]
    z = xs[:, 3 * RWKV_WIDTH:SHIFT_W]
    lane = lax.broadcasted_iota(I32, z.shape, 1)
    zt = jnp.where(lane < W_LORA, jnp.tanh(z),
                   jnp.where(lane < W_LORA + A_LORA, z, jax.nn.sigmoid(z)))
    zh, zl = _split2(zt)
    d = lambda p, q: jnp.dot(p, q, preferred_element_type=F32)
    lo = d(zh, lwh_ref[...]) + d(zl, lwh_ref[...]) + d(zh, lwl_ref[...])
    w_raw = w0_ref[...] + lo[:, 0:RWKV_WIDTH]
    a = jax.nn.sigmoid(a0_ref[...] + lo[:, RWKV_WIDTH:2 * RWKV_WIDTH])
    g = lo[:, 2 * RWKV_WIDTH:3 * RWKV_WIDTH]
    decay = jnp.exp(-math.exp(-0.5) * jax.nn.sigmoid(w_raw))
    bd = bd_ref[...]
    kk = k * kk_ref[...]
    kk = kk / jnp.maximum(jnp.sqrt(_dot_sel(kk * kk, bd)), 1e-12)
    kt = k * (1.0 + (a - 1.0) * ka_ref[...])
    bonus = _dot_sel(r * kt * rk_ref[...], bd) * v
    pos = j * tt + lax.broadcasted_iota(I32, r.shape, 0)
    valid = pos < t_real
    r_o[0] = r
    w_o[0] = jnp.where(valid, decay, 1.0)
    kt_o[0] = jnp.where(valid, kt, 0.0)
    kko_o[0] = jnp.where(valid, kk, 0.0)
    b_o[0] = jnp.where(valid, kk * a, 0.0)
    v_o[0] = v
    bonus_o[0] = bonus
    g_o[0] = g


def _rwkv_prep(p_rw, prev, t_real, shift, wts, tt):
    bsz, tp, _ = p_rw.shape
    tt = min(tt, tp)
    row = lambda n: pl.BlockSpec((1, n), lambda b, j: (0, 0))
    full = lambda a: pl.BlockSpec(a.shape, lambda b, j: (0, 0))
    tok = lambda n: pl.BlockSpec((1, tt, n), lambda b, j: (b, j, 0))
    prev_spec = pl.BlockSpec((1, 1, SHIFT_W), lambda b, j: (b, 0, 0)) if shift else tok(SHIFT_W)
    outs = pl.pallas_call(
        functools.partial(_rwkv_prep_kernel, t_real, tt, shift),
        grid=(bsz, tp // tt),
        in_specs=[tok(SHIFT_W), prev_spec, row(SHIFT_W), row(RWKV_WIDTH), row(RWKV_WIDTH),
                  full(wts["lora_hi"]), full(wts["lora_lo"]),
                  row(RWKV_WIDTH), row(RWKV_WIDTH), row(RWKV_WIDTH), full(wts["bd"])],
        out_specs=[tok(RWKV_WIDTH)] * 8,
        out_shape=[jax.ShapeDtypeStruct((bsz, tp, RWKV_WIDTH), F32)] * 8,
        scratch_shapes=[pltpu.VMEM((1, SHIFT_W), F32)],
        compiler_params=_cparams("parallel", "arbitrary"),
        name="rwkv_prep",
    )(p_rw, prev, wts["mu"], wts["w0"], wts["a0"], wts["lora_hi"], wts["lora_lo"],
      wts["k_k"], wts["k_a"], wts["r_k"], wts["bd"])
    return outs


def _wkv_kernel(bb, tc, r_ref, w_ref, kt_ref, kk_ref, b_ref, v_ref, s0_ref, o_ref, sf_ref, s_ref):
    c = pl.program_id(1)

    @pl.when(c == 0)
    def _():
        s_ref[...] = s0_ref[...]

    lane = lax.broadcasted_iota(I32, (HEAD_DIM, LANE), 1)
    row = lax.broadcasted_iota(I32, (HEAD_DIM, LANE), 0)
    lo = lane < HEAD_DIM
    e0 = lane == row
    e1 = lane == row + HEAD_DIM
    e01 = e0 | e1

    def halves(x):
        s0 = jnp.sum(jnp.where(lo, x, 0.0), axis=-1, keepdims=True)
        s1 = jnp.sum(jnp.where(lo, 0.0, x), axis=-1, keepdims=True)
        return s0, s1

    def group(gi, carry):
        t0 = pl.multiple_of(gi * SUBLANE, SUBLANE)
        for b in range(bb):
            for j in range(RWKV_HEADS // 2):
                sl = slice(j * LANE, (j + 1) * LANE)
                rows = pl.ds(t0, SUBLANE)
                kk8, w8, b8 = kk_ref[b, rows, sl], w_ref[b, rows, sl], b_ref[b, rows, sl]
                kt8, v8, r8 = kt_ref[b, rows, sl], v_ref[b, rows, sl], r_ref[b, rows, sl]
                s = s_ref[b, j]
                orows = []
                for u in range(SUBLANE):
                    a0, a1 = halves(s * kk8[u:u + 1])
                    skk = jnp.where(lo, a0, a1)
                    c0, c1 = halves(jnp.where(e01, v8[u:u + 1], 0.0))
                    vb = jnp.where(lo, c0, c1)
                    s = s * w8[u:u + 1] - skk * b8[u:u + 1] + vb * kt8[u:u + 1]
                    o0, o1 = halves(s * r8[u:u + 1])
                    orows.append(jnp.sum(jnp.where(e0, o0, jnp.where(e1, o1, 0.0)), axis=0, keepdims=True))
                s_ref[b, j] = s
                o_ref[b, rows, sl] = jnp.concatenate(orows, axis=0)
        return carry

    lax.fori_loop(0, tc // SUBLANE, group, 0)

    @pl.when(c == pl.num_programs(1) - 1)
    def _():
        sf_ref[...] = s_ref[...]


def _pair_state(s):
    b = s.shape[0]
    return (s.reshape(b, RWKV_HEADS // 2, 2, HEAD_DIM, HEAD_DIM)
            .transpose(0, 1, 3, 2, 4).reshape(b, RWKV_HEADS // 2, HEAD_DIM, LANE))


def _unpair_state(s):
    b = s.shape[0]
    return (s.reshape(b, RWKV_HEADS // 2, HEAD_DIM, 2, HEAD_DIM)
            .transpose(0, 1, 3, 2, 4).reshape(b, RWKV_HEADS, HEAD_DIM, HEAD_DIM))


def _wkv(r, w, kt, kk, bv, v, s0, bb, tc):
    bsz, tp, _ = r.shape
    tc = min(tc, tp)
    tok = pl.BlockSpec((bb, tc, RWKV_WIDTH), lambda i, c: (i, c, 0))
    st = pl.BlockSpec((bb, RWKV_HEADS // 2, HEAD_DIM, LANE), lambda i, c: (i, 0, 0, 0))
    o, sf = pl.pallas_call(
        functools.partial(_wkv_kernel, bb, tc),
        grid=(bsz // bb, tp // tc),
        in_specs=[tok] * 6 + [st],
        out_specs=[tok, st],
        out_shape=[jax.ShapeDtypeStruct((bsz, tp, RWKV_WIDTH), F32),
                   jax.ShapeDtypeStruct((bsz, RWKV_HEADS // 2, HEAD_DIM, LANE), F32)],
        scratch_shapes=[pltpu.VMEM((bb, RWKV_HEADS // 2, HEAD_DIM, LANE), F32)],
        compiler_params=_cparams("parallel", "arbitrary"),
        name="wkv_scan",
    )(r, w, kt, kk, bv, v, _pair_state(s0))
    return o, _unpair_state(sf)


def _rope(x, c, sa, sb):
    w = x.shape[1]
    return x * c + pltpu.roll(x, w - ROT_HALF, 1) * sa + pltpu.roll(x, ROT_HALF, 1) * sb


def _attn_prep_kernel(p_ref, c_ref, sa_ref, sb_ref, g_ref, b_ref, q_o, k_o, qi_o, ki_o):
    c1, sa1, sb1 = c_ref[...], sa_ref[...], sb_ref[...]
    rep = lambda t, n: jnp.concatenate([t] * n, axis=1)
    nq = ATTN_WIDTH // LANE
    nk = KV_W // LANE
    q_o[...] = _rope(p_ref[:, 0:ATTN_WIDTH], rep(c1, nq), rep(sa1, nq), rep(sb1, nq))
    k_o[...] = _rope(p_ref[:, ATTN_WIDTH:ATTN_WIDTH + KV_W], rep(c1, nk), rep(sa1, nk), rep(sb1, nk))
    qi0 = ATTN_WIDTH + 2 * KV_W
    qi_o[...] = _rope(p_ref[:, qi0:qi0 + IDX_HEADS * IDX_DIM], rep(c1, nq), rep(sa1, nq), rep(sb1, nq))
    x = p_ref[:, KI_OFF:KI_OFF + LANE]
    lane = lax.broadcasted_iota(I32, x.shape, 1)
    isk = lane < IDX_DIM
    mu = jnp.sum(jnp.where(isk, x, 0.0), axis=-1, keepdims=True) * (1.0 / IDX_DIM)
    xc = jnp.where(isk, x - mu, 0.0)
    var = jnp.sum(xc * xc, axis=-1, keepdims=True) * (1.0 / IDX_DIM)
    y = xc * lax.rsqrt(var + LN_EPS) * g_ref[...] + b_ref[...]
    y = _rope(y, jnp.where(isk, c1, 1.0), jnp.where(isk, sa1, 0.0), jnp.where(isk, sb1, 0.0))
    ki_o[...] = jnp.where(isk, y, x)


def _attn_prep(p_at, tabs, idx_g, idx_b, tm):
    n = p_at.shape[0]
    tm = min(tm, n)
    tpb = tabs[0].shape[0] // tm
    tok = lambda w: pl.BlockSpec((tm, w), lambda i: (i, 0))
    tab = pl.BlockSpec((tm, LANE), lambda i: (i % tpb, 0))
    row = pl.BlockSpec((1, LANE), lambda i: (0, 0))
    return pl.pallas_call(
        _attn_prep_kernel,
        grid=(n // tm,),
        in_specs=[tok(AT_PAD_W), tab, tab, tab, row, row],
        out_specs=[tok(ATTN_WIDTH), tok(KV_W), tok(IDX_HEADS * IDX_DIM), tok(LANE)],
        out_shape=[jax.ShapeDtypeStruct((n, ATTN_WIDTH), F32), jax.ShapeDtypeStruct((n, KV_W), F32),
                   jax.ShapeDtypeStruct((n, IDX_HEADS * IDX_DIM), F32), jax.ShapeDtypeStruct((n, LANE), F32)],
        compiler_params=_cparams("parallel"),
        name="attn_prep",
    )(p_at, tabs[0], tabs[1], tabs[2], idx_g, idx_b)


def _rope_tables(pos):
    inv = ROPE_THETA ** (-jnp.arange(ROT_HALF, dtype=F32) * 2.0 / ROT)
    ang = pos.astype(F32)[:, None] * inv[None, :]
    cos, sin = jnp.cos(ang), jnp.sin(ang)
    n = pos.shape[0]
    rest = HEAD_DIM - ROT
    c = jnp.concatenate([cos, cos, jnp.ones((n, rest), F32)], axis=1)
    sa = jnp.concatenate([-sin, jnp.zeros((n, rest + ROT_HALF), F32)], axis=1)
    sb = jnp.concatenate([jnp.zeros((n, ROT_HALF), F32), sin, jnp.zeros((n, rest), F32)], axis=1)
    two = lambda t: jnp.concatenate([t, t], axis=1)
    return two(c), two(sa), two(sb)


def _select_bounds(key_ref, n_tiles, rows, kt, k_sel, idx_bits):
    def count(pred):
        def body(i, acc):
            off = pl.multiple_of(i * kt, kt)
            idx = off + lax.broadcasted_iota(I32, (rows, kt), 1)
            return acc + jnp.where(pred(key_ref[:, pl.ds(off, kt)], idx), 1.0, 0.0)
        acc = lax.fori_loop(0, n_tiles, body, jnp.zeros((rows, kt), F32))
        return jnp.sum(acc, axis=-1, keepdims=True)

    def thr_bit(i, res):
        cand = res + jnp.left_shift(jnp.int32(1), 31 - i)
        c = count(lambda key, idx: key >= cand)
        return jnp.where(c >= k_sel, cand, res)

    thr = lax.fori_loop(0, 32, thr_bit, jnp.full((rows, 1), INT_MIN, I32))
    need = k_sel - count(lambda key, idx: key > thr)

    def idx_bit(i, res):
        cand = res | jnp.left_shift(jnp.int32(1), idx_bits - 1 - i)
        c = count(lambda key, idx: (key == thr) & (idx < cand))
        return jnp.where(c < need, cand, res)

    jmax = lax.fori_loop(0, idx_bits, idx_bit, jnp.zeros((rows, 1), I32))
    return thr, jmax


def _dsa_prompt_kernel(kt, n_sel, idx_bits, q_ref, qi_ref, wi_ref, k_ref, v_ref, ki_ref, o_ref, key_ref):
    i = pl.program_id(1)
    n_kt = (i * Q_BLOCK + Q_BLOCK + kt - 1) // kt
    qi = qi_ref[0]
    qis = jnp.concatenate([qi[:, h * IDX_DIM:(h + 1) * IDX_DIM] for h in range(IDX_HEADS)], axis=0)
    wi = wi_ref[0] * IDX_HEADS ** -0.5
    qpos = i * Q_BLOCK + lax.broadcasted_iota(I32, (Q_BLOCK, kt), 0)
    lane = lax.broadcasted_iota(I32, (Q_BLOCK, kt), 1)

    def scores(t, carry):
        off = pl.multiple_of(t * kt, kt)
        s = lax.dot_general(qis, ki_ref[0, pl.ds(off, kt), :], _NT, preferred_element_type=F32)
        acc = jnp.zeros((Q_BLOCK, kt), F32)
        for h in range(IDX_HEADS):
            acc = acc + wi[:, h:h + 1] * jnp.maximum(s[h * Q_BLOCK:(h + 1) * Q_BLOCK] * IDX_DIM ** -0.5, 0.0)
        key_ref[:, pl.ds(off, kt)] = jnp.where(off + lane <= qpos, _f2key(acc), INT_MIN)
        return carry

    lax.fori_loop(0, n_kt, scores, 0)
    thr, jmax = _select_bounds(key_ref, n_kt, Q_BLOCK, kt, n_sel, idx_bits)

    q = q_ref[0].astype(F32)
    grp = lax.broadcasted_iota(I32, (Q_BLOCK, KV_W), 1) >> 6
    pieces = [None] * ATTN_HEADS
    for g in range(KV_HEADS):
        def expand(h):
            qh = q[:, h * HEAD_DIM:(h + 1) * HEAD_DIM]
            return jnp.where(grp == g, jnp.concatenate([qh] * KV_HEADS, axis=1), 0.0).astype(BF16)
        qe = jnp.concatenate([expand(2 * g), expand(2 * g + 1)], axis=0)

        def attend(t, carry):
            m, l, acc = carry
            off = pl.multiple_of(t * kt, kt)
            lg = lax.dot_general(qe, k_ref[0, pl.ds(off, kt), :], _NT,
                                 preferred_element_type=F32) * HEAD_DIM ** -0.5
            key = key_ref[:, pl.ds(off, kt)]
            kidx = off + lane
            sel = (kidx <= qpos) & ((key > thr) | ((key == thr) & (kidx <= jmax)))
            sel = jnp.concatenate([sel, sel], axis=0)
            lg = jnp.where(sel, lg, NEG_BIG)
            mn = jnp.maximum(m, jnp.max(lg, axis=-1, keepdims=True))
            p = jnp.where(sel, jnp.exp(lg - mn), 0.0)
            alpha = jnp.exp(m - mn)
            l = alpha * l + jnp.sum(p, axis=-1, keepdims=True)
            acc = alpha * acc + jnp.dot(p.astype(BF16), v_ref[0, pl.ds(off, kt), :],
                                        preferred_element_type=F32)
            return mn, l, acc

        init = (jnp.full((2 * Q_BLOCK, 1), NEG_BIG, F32), jnp.zeros((2 * Q_BLOCK, 1), F32),
                jnp.zeros((2 * Q_BLOCK, KV_W), F32))
        m, l, acc = lax.fori_loop(0, n_kt, attend, init)
        og = acc[:, g * HEAD_DIM:(g + 1) * HEAD_DIM] / l
        pieces[2 * g] = og[0:Q_BLOCK]
        pieces[2 * g + 1] = og[Q_BLOCK:2 * Q_BLOCK]
    o_ref[0] = jnp.concatenate(pieces, axis=1)


def _dsa_prompt(q, qi, wi, k, v, ki, t_real):
    bsz, tp, _ = q.shape
    kt = 640 if tp % 640 == 0 else LANE
    n_sel = min(TOPK_MAX, t_real // 4)
    idx_bits = max(1, (tp - 1).bit_length())
    blk = lambda w: pl.BlockSpec((1, Q_BLOCK, w), lambda b, i: (b, i, 0))
    seq = lambda w: pl.BlockSpec((1, tp, w), lambda b, i: (b, 0, 0))
    return pl.pallas_call(
        functools.partial(_dsa_prompt_kernel, kt, n_sel, idx_bits),
        grid=(bsz, tp // Q_BLOCK),
        in_specs=[blk(ATTN_WIDTH), blk(IDX_HEADS * IDX_DIM), blk(IDX_HEADS),
                  seq(KV_W), seq(KV_W), seq(IDX_DIM)],
        out_specs=blk(ATTN_WIDTH),
        out_shape=jax.ShapeDtypeStruct((bsz, tp, ATTN_WIDTH), F32),
        scratch_shapes=[pltpu.VMEM((Q_BLOCK, tp), I32)],
        compiler_params=_cparams("parallel", "arbitrary"),
        name="dsa_prompt",
    )(q, qi, wi, k, v, ki)


def _dsa_s_scores_kernel(pps, pt_ref, qi_ref, wi_ref, *refs):
    ci_refs, o_ref = refs[:pps], refs[pps]
    qi = qi_ref[0]
    w = wi_ref[0] * IDX_HEADS ** -0.5
    for u in range(pps):
        s = _dot3(qi, ci_refs[u][0, 0], _NT)
        sc = jnp.sum(w * jnp.maximum(s * IDX_DIM ** -0.5, 0.0), axis=0, keepdims=True)
        o_ref[0, :, u * PAGE_SIZE:(u + 1) * PAGE_SIZE] = sc


def _dsa_s_scores(page_table, qi3, wi3, cache_idx, layer):
    bd, npages = page_table.shape
    pps = PAGES_PER_STEP
    page = lambda u: pl.BlockSpec((1, 1, PAGE_SIZE, IDX_DIM),
                                  lambda b, p, pt: (layer, pt[b * npages + p * pps + u], 0, 0))
    gs = pltpu.PrefetchScalarGridSpec(
        num_scalar_prefetch=1,
        grid=(bd, npages // pps),
        in_specs=[pl.BlockSpec((1, IDX_HEADS, IDX_DIM), lambda b, p, pt: (b, 0, 0)),
                  pl.BlockSpec((1, IDX_HEADS, 1), lambda b, p, pt: (b, 0, 0))] + [page(u) for u in range(pps)],
        out_specs=pl.BlockSpec((1, 1, pps * PAGE_SIZE), lambda b, p, pt: (b, 0, p)),
    )
    return pl.pallas_call(
        functools.partial(_dsa_s_scores_kernel, pps),
        grid_spec=gs,
        out_shape=jax.ShapeDtypeStruct((bd, 1, npages * PAGE_SIZE), F32),
        compiler_params=_cparams("parallel", "arbitrary"),
        name="dsa_decode_scores",
    )(page_table.reshape(-1), qi3, wi3, *([cache_idx] * pps))


def _dsa_s_bounds_kernel(past, n_sel, idx_bits, sc_ref, qi_ref, kiw_ref, hm_ref, thr_o, j_o, kn_o, key_ref):
    rows = sc_ref.shape[0]
    kiw = kiw_ref[...]
    lane = lax.broadcasted_iota(I32, kiw.shape, 1)
    rolled = pltpu.roll(kiw, IDX_DIM, 1)
    ki2 = jnp.where(lane < IDX_DIM, kiw, rolled)
    w8 = jnp.where(lane < IDX_HEADS, rolled, 0.0) * IDX_HEADS ** -0.5
    prod = qi_ref[...] * jnp.concatenate([ki2] * (IDX_HEADS // 2), axis=1)
    s = _dot3(prod, hm_ref[...])
    new = jnp.sum(w8 * jnp.maximum(s * IDX_DIM ** -0.5, 0.0), axis=-1, keepdims=True)
    knew = _f2key(new)
    key_ref[:, 0:past] = _f2key(sc_ref[...])
    key_ref[:, past:past + LANE] = jnp.where(lane == 0, knew, INT_MIN)
    thr, jmax = _select_bounds(key_ref, (past + LANE) // LANE, rows, LANE, n_sel, idx_bits)
    thr_o[...] = thr
    j_o[...] = jmax
    kn_o[...] = knew


def _dsa_s_bounds(scores, qi, kiw, hm, past):
    rows = scores.shape[0]
    n_sel = min(TOPK_MAX, (past + 1) // 4)
    idx_bits = (past + LANE - 1).bit_length()
    out = jax.ShapeDtypeStruct((rows, 1), I32)
    return pl.pallas_call(
        functools.partial(_dsa_s_bounds_kernel, past, n_sel, idx_bits),
        out_shape=[out, out, out],
        scratch_shapes=[pltpu.VMEM((rows, past + LANE), I32)],
        compiler_params=pltpu.CompilerParams(vmem_limit_bytes=VMEM_LIMIT),
        name="dsa_decode_bounds",
    )(scores, qi, kiw, hm)


def _dsa_s_attend_kernel(pps, past, pt_ref, thr_ref, j_ref, kn_ref, qe_ref, sc_ref, kn_row_ref, vn_row_ref,
                         fold_ref, *refs):
    ck, cv, o_ref = refs[:pps], refs[pps:2 * pps], refs[2 * pps]
    m_ref, l_ref, acc_ref = refs[2 * pps + 1:]
    b = pl.program_id(0)
    p = pl.program_id(1)

    @pl.when(p == 0)
    def _():
        m_ref[...] = jnp.full(m_ref.shape, NEG_BIG, F32)
        l_ref[...] = jnp.zeros(l_ref.shape, F32)
        acc_ref[...] = jnp.zeros(acc_ref.shape, F32)

    thr, jmax = thr_ref[b], j_ref[b]
    qe = qe_ref[0]
    qeb = qe.astype(BF16)
    lane = lax.broadcasted_iota(I32, (1, PAGE_SIZE), 1)
    for u in range(pps):
        lg = lax.dot_general(qeb, ck[u][0, 0].astype(BF16), _NT,
                             preferred_element_type=F32) * HEAD_DIM ** -0.5
        key = _f2key(sc_ref[0, :, u * PAGE_SIZE:(u + 1) * PAGE_SIZE])
        kidx = (p * pps + u) * PAGE_SIZE + lane
        sel = (key > thr) | ((key == thr) & (kidx <= jmax))
        lg = jnp.where(sel, lg, NEG_BIG)
        m = m_ref[...]
        mn = jnp.maximum(m, jnp.max(lg, axis=-1, keepdims=True))
        pr = jnp.where(sel, jnp.exp(lg - mn), 0.0)
        alpha = jnp.exp(m - mn)
        l_ref[...] = alpha * l_ref[...] + jnp.sum(pr, axis=-1, keepdims=True)
        acc_ref[...] = alpha * acc_ref[...] + jnp.dot(pr.astype(BF16), cv[u][0, 0].astype(BF16),
                                                      preferred_element_type=F32)
        m_ref[...] = mn

    @pl.when(p == pl.num_programs(1) - 1)
    def _():
        knew = kn_ref[b]
        sel_new = (knew > thr) | ((knew == thr) & (past <= jmax))
        lg = jnp.sum(qe * kn_row_ref[0], axis=-1, keepdims=True) * HEAD_DIM ** -0.5
        m = m_ref[...]
        mn = jnp.where(sel_new, jnp.maximum(m, lg), m)
        pr = jnp.where(sel_new, jnp.exp(lg - mn), 0.0)
        alpha = jnp.exp(m - mn)
        l = alpha * l_ref[...] + pr
        acc = alpha * acc_ref[...] + pr * vn_row_ref[0]
        rowi = lax.broadcasted_iota(I32, acc.shape, 0)
        lanei = lax.broadcasted_iota(I32, acc.shape, 1)
        own = jnp.where((lanei >> 6) == (rowi >> 1), acc / l, 0.0)
        o_ref[0] = _dot_sel(own, fold_ref[...])


def _dsa_s_attend(page_table, thr, jmax, knew, qe, scores, k_new, v_new, fold, cache_k, cache_v, layer):
    bd, npages = page_table.shape
    past = npages * PAGE_SIZE
    pps = PAGES_PER_STEP
    page = lambda u: pl.BlockSpec((1, 1, PAGE_SIZE, KV_W),
                                  lambda b, p, pt, t, j, kn: (layer, pt[b * npages + p * pps + u], 0, 0))
    per_b = lambda s: pl.BlockSpec((1,) + s, lambda b, p, pt, t, j, kn: (b, 0, 0))
    gs = pltpu.PrefetchScalarGridSpec(
        num_scalar_prefetch=4,
        grid=(bd, npages // pps),
        in_specs=[per_b((ATTN_HEADS, KV_W)),
                  pl.BlockSpec((1, 1, pps * PAGE_SIZE), lambda b, p, pt, t, j, kn: (b, 0, p)),
                  per_b((1, KV_W)), per_b((1, KV_W)),
                  pl.BlockSpec(fold.shape, lambda b, p, pt, t, j, kn: (0, 0))]
                 + [page(u) for u in range(pps)] * 2,
        out_specs=per_b((ATTN_HEADS, HEAD_DIM)),
        scratch_shapes=[pltpu.VMEM((ATTN_HEADS, 1), F32), pltpu.VMEM((ATTN_HEADS, 1), F32),
                        pltpu.VMEM((ATTN_HEADS, KV_W), F32)],
    )
    return pl.pallas_call(
        functools.partial(_dsa_s_attend_kernel, pps, past),
        grid_spec=gs,
        out_shape=jax.ShapeDtypeStruct((bd, ATTN_HEADS, HEAD_DIM), F32),
        compiler_params=_cparams("parallel", "arbitrary"),
        name="dsa_decode_attend",
    )(page_table.reshape(-1), thr.reshape(-1), jmax.reshape(-1), knew.reshape(-1),
      qe, scores, k_new, v_new, fold, *([cache_k] * pps), *([cache_v] * pps))


def _merge_ln_kernel(o_ref, bonus_ref, g_ref, at_ref, x_ref, wt_ref, wb_ref, xg_ref, xb_ref,
                     lg_ref, lb_ref, bd_ref, out_ref):
    bd = bd_ref[...]
    o = o_ref[...]
    mean = _dot_sel(o, bd) * (1.0 / HEAD_DIM)
    oc = o - mean
    var = _dot_sel(oc * oc, bd) * (1.0 / HEAD_DIM)
    rw = (oc * lax.rsqrt(var + GN_EPS) * xg_ref[...] + xb_ref[...] + bonus_ref[...]) * g_ref[...]
    f = (jnp.dot(rw.astype(BF16), wt_ref[...], preferred_element_type=F32)
         + jnp.dot(at_ref[...].astype(BF16), wb_ref[...], preferred_element_type=F32))
    out_ref[...] = _layer_norm(DEEPNORM_ALPHA * x_ref[...] + f, lg_ref[...], lb_ref[...])


def _merge_ln(o, bonus, g, at, x, wts, tm):
    n = x.shape[0]
    tm = min(tm, n)
    tok = lambda w: pl.BlockSpec((tm, w), lambda i: (i, 0))
    full = lambda a: pl.BlockSpec(a.shape, lambda i: (0, 0))
    ws = [wts["wo_top"], wts["wo_bot"], wts["lnx_g"], wts["lnx_b"], wts["ln1_g"], wts["ln1_b"], wts["bd"]]
    return pl.pallas_call(
        _merge_ln_kernel,
        grid=(n // tm,),
        in_specs=[tok(RWKV_WIDTH)] * 3 + [tok(ATTN_WIDTH), tok(D_MODEL)] + [full(a) for a in ws],
        out_specs=tok(D_MODEL),
        out_shape=jax.ShapeDtypeStruct((n, D_MODEL), F32),
        compiler_params=_cparams("parallel"),
        name="merge_ln",
    )(o, bonus, g, at, x, *ws)


def _take_top(src_ref, n_rows, val_ref, idx_ref):
    shape = src_ref.shape
    row = lax.broadcasted_iota(I32, shape, 1)

    def body(a, carry):
        sv = src_ref[...]
        m = jnp.max(sv, axis=1, keepdims=True)
        idx = jnp.min(jnp.where(sv == m, row, n_rows), axis=1, keepdims=True)
        val_ref[:, pl.ds(a, 1), :] = m
        idx_ref[:, pl.ds(a, 1), :] = idx
        src_ref[...] = jnp.where(row == idx, -jnp.inf, sv)
        return carry

    lax.fori_loop(0, PEER_TOPK, body, 0)


def _peer_route_kernel(x_ref, wq_ref, sk0_ref, sk1_ref, e_o, g_o, s_ref, t_ref, i_ref, c_ref, ts_ref, ic_ref):
    q = jnp.dot(x_ref[...].astype(BF16), wq_ref[...], preferred_element_type=F32)
    for h in range(PEER_HEADS):
        qh = q[:, h * PEER_DKEY:(h + 1) * PEER_DKEY]
        s_ref[h] = _dot3(sk0_ref[...], qh, _NT)
        s_ref[PEER_HEADS + h] = _dot3(sk1_ref[...], qh, _NT)
    _take_top(s_ref, N_KEYS, t_ref, i_ref)
    t1, t2 = t_ref[0:PEER_HEADS], t_ref[PEER_HEADS:2 * PEER_HEADS]
    c_ref[...] = jnp.concatenate([t1[:, a:a + 1, :] + t2 for a in range(PEER_TOPK)], axis=1)
    _take_top(c_ref, PEER_TOPK * PEER_TOPK, ts_ref, ic_ref)
    ic = ic_ref[...]
    i1, i2 = i_ref[0:PEER_HEADS], i_ref[PEER_HEADS:2 * PEER_HEADS]
    ia, ib = ic >> 4, ic & (PEER_TOPK - 1)
    e = jnp.zeros(ic.shape, I32)
    for a in range(PEER_TOPK):
        e = e + jnp.where(ia == a, i1[:, a:a + 1, :] * N_KEYS, 0) + jnp.where(ib == a, i2[:, a:a + 1, :], 0)
    ts = ts_ref[...]
    ex = jnp.exp(ts - jnp.max(ts, axis=1, keepdims=True))
    e_o[0] = e
    g_o[0] = ex / jnp.sum(ex, axis=1, keepdims=True)


def _peer_route(x, wq, sk0, sk1, tt):
    n = x.shape[0]
    nt = n // tt
    full = lambda a: pl.BlockSpec(a.shape, lambda i: (0, 0))
    out = pl.BlockSpec((1, PEER_HEADS, PEER_TOPK, tt), lambda i: (i, 0, 0, 0))
    return pl.pallas_call(
        _peer_route_kernel,
        grid=(nt,),
        in_specs=[pl.BlockSpec((tt, D_MODEL), lambda i: (i, 0)), full(wq), full(sk0), full(sk1)],
        out_specs=[out, out],
        out_shape=[jax.ShapeDtypeStruct((nt, PEER_HEADS, PEER_TOPK, tt), I32),
                   jax.ShapeDtypeStruct((nt, PEER_HEADS, PEER_TOPK, tt), F32)],
        scratch_shapes=[pltpu.VMEM((2 * PEER_HEADS, N_KEYS, tt), F32),
                        pltpu.VMEM((2 * PEER_HEADS, PEER_TOPK, tt), F32),
                        pltpu.VMEM((2 * PEER_HEADS, PEER_TOPK, tt), I32),
                        pltpu.VMEM((PEER_HEADS, PEER_TOPK * PEER_TOPK, tt), F32),
                        pltpu.VMEM((PEER_HEADS, PEER_TOPK, tt), F32),
                        pltpu.VMEM((PEER_HEADS, PEER_TOPK, tt), I32)],
        compiler_params=_cparams("parallel"),
        name="peer_route",
    )(x, wq, sk0, sk1)


def _peer_gather_kernel(tt, n_valid, idx_hbm, x_ref, g_ref, uv_hbm, lg_ref, lb_ref, o_ref,
                        idx_smem, buf, sem, isem, y_ref):
    i = pl.program_id(0)
    icp = pltpu.make_async_copy(idx_hbm.at[i], idx_smem, isem)
    icp.start()
    if n_valid < tt:
        y_ref[...] = jnp.zeros(y_ref.shape, F32)
    icp.wait()

    def issue(t, slot):
        for s in range(PEER_SLOTS):
            e = idx_smem[s * tt + t]
            pltpu.make_async_copy(uv_hbm.at[pl.ds(e, 1)], buf.at[slot, pl.ds(s, 1)], sem.at[slot]).start()

    def wait(slot):
        pltpu.make_async_copy(uv_hbm.at[pl.ds(0, PEER_SLOTS)], buf.at[slot], sem.at[slot]).wait()

    issue(0, 0)
    lane = lax.broadcasted_iota(I32, (PEER_SLOTS, tt), 1)

    def body(t, carry):
        slot = t % 2

        @pl.when(t + 1 < n_valid)
        def _():
            issue(t + 1, 1 - slot)

        wait(slot)
        xrow = x_ref[pl.ds(t, 1), :]
        h = jnp.sum(buf[slot, :, 0:D_MODEL] * xrow, axis=-1, keepdims=True)
        gate = jnp.sum(jnp.where(lane == t, g_ref[0], 0.0), axis=-1, keepdims=True)
        coef = gate * jax.nn.gelu(h)
        y_ref[pl.ds(t, 1), :] = jnp.sum(coef * buf[slot, :, D_MODEL:2 * D_MODEL], axis=0, keepdims=True)
        return carry

    lax.fori_loop(0, n_valid, body, 0)
    o_ref[...] = _layer_norm(DEEPNORM_ALPHA * x_ref[...] + y_ref[...], lg_ref[...], lb_ref[...])


def _peer_gather(idx, gates, x, uv, ln_g, ln_b, tt, n_valid):
    n = x.shape[0]
    nt = n // tt
    row = pl.BlockSpec((1, D_MODEL), lambda i: (0, 0))
    return pl.pallas_call(
        functools.partial(_peer_gather_kernel, tt, n_valid),
        grid=(nt,),
        in_specs=[pl.BlockSpec(memory_space=pl.ANY),
                  pl.BlockSpec((tt, D_MODEL), lambda i: (i, 0)),
                  pl.BlockSpec((1, PEER_SLOTS, tt), lambda i: (i, 0, 0)),
                  pl.BlockSpec(memory_space=pl.ANY), row, row],
        out_specs=pl.BlockSpec((tt, D_MODEL), lambda i: (i, 0)),
        out_shape=jax.ShapeDtypeStruct((n, D_MODEL), F32),
        scratch_shapes=[pltpu.SMEM((PEER_SLOTS * tt,), I32),
                        pltpu.VMEM((2, PEER_SLOTS, 2 * D_MODEL), F32),
                        pltpu.SemaphoreType.DMA((2,)),
                        pltpu.SemaphoreType.DMA(()),
                        pltpu.VMEM((tt, D_MODEL), F32)],
        compiler_params=_cparams("arbitrary"),
        name="peer_gather",
    )(idx, x, gates, uv, ln_g, ln_b)


def _peer(x, wts, tt, n_valid):
    e, gates = _peer_route(x, wts["wq"], wts["sk0"], wts["sk1"], tt)
    nt = x.shape[0] // tt
    return _peer_gather(e.reshape(nt, PEER_SLOTS * tt), gates.reshape(nt, PEER_SLOTS, tt), x,
                        wts["uv"], wts["ln2_g"], wts["ln2_b"], tt, n_valid)


def _layer_weights(l, w_in, shift_mu, decay_w0, decay_up, iclr_a0, iclr_up, gate_up, k_k, k_a, r_k,
                   lnx_g, lnx_b, idx_ln_g, idx_ln_b, w_out, ln1_g, ln1_b, ln2_g, ln2_b,
                   peer_wq, peer_subkeys, peer_u, peer_v):
    row = lambda a: a.reshape(1, -1).astype(F32)
    lora = jnp.zeros((LORA_W, 3 * RWKV_WIDTH), F32)
    lora = lora.at[0:W_LORA, 0:RWKV_WIDTH].set(decay_up[l])
    lora = lora.at[W_LORA:W_LORA + A_LORA, RWKV_WIDTH:2 * RWKV_WIDTH].set(iclr_up[l])
    lora = lora.at[W_LORA + A_LORA:, 2 * RWKV_WIDTH:].set(gate_up[l])
    lora_hi = lora.astype(BF16)
    pad_lane = lambda a: jnp.pad(a.reshape(1, -1), ((0, 0), (0, LANE - a.size)))
    half = PEER_DKEY // 2
    seg = jnp.arange(RWKV_WIDTH) // HEAD_DIM
    return dict(
        w_rw=w_in[l][:, :SHIFT_W].astype(BF16),
        w_at=jnp.pad(w_in[l][:, SHIFT_W:], ((0, 0), (0, AT_PAD_W - ATTN_PROJ_W))).astype(BF16),
        mu=row(shift_mu[l]), w0=row(decay_w0[l]), a0=row(iclr_a0[l]),
        lora_hi=lora_hi, lora_lo=(lora - lora_hi.astype(F32)).astype(BF16),
        k_k=row(k_k[l]), k_a=row(k_a[l]), r_k=row(r_k[l]),
        bd=(seg[:, None] == seg[None, :]).astype(BF16),
        lnx_g=row(lnx_g[l]), lnx_b=row(lnx_b[l]),
        idx_g=pad_lane(idx_ln_g[l]), idx_b=pad_lane(idx_ln_b[l]),
        wo_top=w_out[l][:RWKV_WIDTH].astype(BF16), wo_bot=w_out[l][RWKV_WIDTH:].astype(BF16),
        ln1_g=row(ln1_g[l]), ln1_b=row(ln1_b[l]), ln2_g=row(ln2_g[l]), ln2_b=row(ln2_b[l]),
        wq=peer_wq[l].astype(BF16),
        sk0=jnp.pad(peer_subkeys[l, 0], ((0, 0), (0, half))),
        sk1=jnp.pad(peer_subkeys[l, 1], ((0, 0), (half, 0))),
        uv=jnp.concatenate([peer_u[l], peer_v[l]], axis=1),
    )


def _tile(n, cap):
    if n <= cap:
        return n
    return max(d for d in range(8, cap + 1, 8) if n % d == 0)


def _mixer_front(x, wts, tabs):
    tm = _tile(x.shape[0], 512)
    p_rw = _matmul(x, wts["w_rw"], tm)
    p_at = _matmul(x, wts["w_at"], tm)
    q, k, qi, kiw = _attn_prep(p_at, tabs, wts["idx_g"], wts["idx_b"], _tile(tabs[0].shape[0], 640))
    v = p_at[:, ATTN_WIDTH + KV_W:ATTN_WIDTH + 2 * KV_W]
    return p_rw, q, k, v, qi, kiw


def kernel(x_prompt, x_sample, cache_k, cache_v, cache_idx_k, state_wkv, state_shift, page_table, meta_tokens, w_in, shift_mu, decay_w0, decay_up, iclr_a0, iclr_up, gate_up, k_k, k_a, r_k, lnx_g, lnx_b, idx_ln_g, idx_ln_b, w_out, ln1_g, ln1_b, ln2_g, ln2_b, peer_wq, peer_subkeys, peer_u, peer_v):
    bsz, seq, _ = x_prompt.shape
    t = seq + N_META
    tp = -(-t // LANE) * LANE
    n_p = bsz * tp
    depth = w_in.shape[0]
    bd_, ts_, _ = x_sample.shape
    assert ts_ == 1
    npages = page_table.shape[1]
    past = npages * PAGE_SIZE
    n_pool = cache_k.shape[1]
    peer_tt = LANE
    ns_pad = -(-bd_ // peer_tt) * peer_tt

    xp = jnp.concatenate([jnp.broadcast_to(meta_tokens[None], (bsz, N_META, D_MODEL)), x_prompt], axis=1)
    xp = jnp.pad(xp, ((0, 0), (0, tp - t), (0, 0))).reshape(n_p, D_MODEL)
    xs = x_sample.reshape(bd_, D_MODEL)
    tabs_p = _rope_tables(jnp.arange(tp, dtype=I32))
    tabs_s = _rope_tables(jnp.full((bd_,), past, I32))
    ck = cache_k.reshape(depth, n_pool, PAGE_SIZE, KV_W)
    cv = cache_v.reshape(depth, n_pool, PAGE_SIZE, KV_W)
    hsel = jnp.arange(IDX_HEADS * IDX_DIM)[:, None] // IDX_DIM == jnp.arange(LANE)[None, :]
    hm = hsel.astype(F32)
    fold = (jnp.arange(KV_W)[:, None] % HEAD_DIM == jnp.arange(HEAD_DIM)[None, :]).astype(BF16)
    own = (jnp.arange(KV_W)[None, :] // HEAD_DIM == jnp.arange(ATTN_HEADS)[:, None] // (ATTN_HEADS // KV_HEADS))

    k_p, v_p, ki_p, wkv_p, sh_p = [], [], [], [], []
    k_s, v_s, ki_s, wkv_s, sh_s = [], [], [], [], []
    for l in range(depth):
        wts = _layer_weights(l, w_in, shift_mu, decay_w0, decay_up, iclr_a0, iclr_up, gate_up, k_k, k_a, r_k,
                             lnx_g, lnx_b, idx_ln_g, idx_ln_b, w_out, ln1_g, ln1_b, ln2_g, ln2_b,
                             peer_wq, peer_subkeys, peer_u, peer_v)

        p_rw, q, k, v, qi, kiw = _mixer_front(xp, wts, tabs_p)
        p_rw3 = p_rw.reshape(bsz, tp, SHIFT_W)
        r_, w_, kt_, kk_, b_, vv_, bonus, g = _rwkv_prep(
            p_rw3, jnp.zeros((bsz, 1, SHIFT_W), F32), t, True, wts, LANE)
        o, s_fin = _wkv(r_, w_, kt_, kk_, b_, vv_,
                        jnp.zeros((bsz, RWKV_HEADS, HEAD_DIM, HEAD_DIM), F32), bsz, LANE)
        three = lambda a: a.reshape(bsz, tp, -1)
        at = _dsa_prompt(three(q).astype(BF16), three(qi).astype(BF16),
                         three(kiw)[:, :, IDX_DIM:IDX_DIM + IDX_HEADS],
                         three(k).astype(BF16), three(v).astype(BF16),
                         three(kiw)[:, :, :IDX_DIM].astype(BF16), t)
        flat = lambda a: a.reshape(n_p, -1)
        x1 = _merge_ln(flat(o), flat(bonus), flat(g), flat(at), xp, wts, _tile(n_p, 256))
        xp = _peer(x1, wts, peer_tt, peer_tt)
        k_p.append(three(k)[:, :t].reshape(bsz, t, KV_HEADS, HEAD_DIM))
        v_p.append(three(v)[:, :t].reshape(bsz, t, KV_HEADS, HEAD_DIM))
        ki_p.append(three(kiw)[:, :t, :IDX_DIM])
        wkv_p.append(s_fin)
        sh_p.append(p_rw3[:, t - 1])

        p_rw, q, k, v, qi, kiw = _mixer_front(xs, wts, tabs_s)
        r_, w_, kt_, kk_, b_, vv_, bonus, g = _rwkv_prep(
            p_rw[None], state_shift[l][None], bd_, False, wts, bd_)
        tc_s = 8
        padt = lambda a, c: jnp.pad(a[0][:, None, :], ((0, 0), (0, tc_s - 1), (0, 0)), constant_values=c)
        o, s_fin = _wkv(padt(r_, 0.0), padt(w_, 1.0), padt(kt_, 0.0), padt(kk_, 0.0), padt(b_, 0.0),
                        padt(vv_, 0.0), state_wkv[l].astype(F32), 4, tc_s)
        o = o[:, 0]
        scores = _dsa_s_scores(page_table, qi.reshape(bd_, IDX_HEADS, IDX_DIM),
                               kiw[:, IDX_DIM:IDX_DIM + IDX_HEADS].reshape(bd_, IDX_HEADS, 1),
                               cache_idx_k, l)
        thr, jmax, knew = _dsa_s_bounds(scores.reshape(bd_, past), qi, kiw, hm, past)
        qe = jnp.where(own[None], jnp.tile(q.reshape(bd_, ATTN_HEADS, HEAD_DIM), (1, 1, KV_HEADS)), 0.0)
        at = _dsa_s_attend(page_table, thr, jmax, knew, qe, scores, k[:, None, :], v[:, None, :], fold,
                           ck, cv, l).reshape(bd_, ATTN_WIDTH)
        x1 = _merge_ln(o, bonus[0], g[0], at, xs, wts, bd_)
        x1p = jnp.pad(x1, ((0, ns_pad - bd_), (0, 0)))
        xs = _peer(x1p, wts, peer_tt, bd_)[:bd_]
        k_s.append(k.reshape(bd_, 1, KV_HEADS, HEAD_DIM))
        v_s.append(v.reshape(bd_, 1, KV_HEADS, HEAD_DIM))
        ki_s.append(kiw[:, None, :IDX_DIM])
        wkv_s.append(s_fin)
        sh_s.append(p_rw)

    y_prompt = xp.reshape(bsz, tp, D_MODEL)[:, N_META:t]
    y_sample = xs.reshape(bd_, 1, D_MODEL)
    return (y_prompt, y_sample, jnp.stack(k_p), jnp.stack(v_p), jnp.stack(ki_p),
            jnp.stack(wkv_p).astype(state_wkv.dtype), jnp.stack(sh_p).astype(state_shift.dtype),
            jnp.stack(k_s), jnp.stack(v_s), jnp.stack(ki_s),
            jnp.stack(wkv_s).astype(state_wkv.dtype), jnp.stack(sh_s).astype(state_shift.dtype))
```

```python
import functools
import math

import jax
import jax.numpy as jnp
from jax import lax
from jax.experimental import pallas as pl
from jax.experimental.pallas import tpu as pltpu

F32 = jnp.float32
BF16 = jnp.bfloat16
I32 = jnp.int32

D_MODEL = 1024
N_META = 16
HEAD_DIM = 64
RWKV_WIDTH = D_MODEL // 2
RWKV_HEADS = RWKV_WIDTH // HEAD_DIM
ATTN_WIDTH = D_MODEL - RWKV_WIDTH
ATTN_HEADS = ATTN_WIDTH // HEAD_DIM
KV_HEADS = ATTN_HEADS // 2
KV_W = KV_HEADS * HEAD_DIM
W_LORA = 64
A_LORA = 64
G_LORA = 128
LORA_W = W_LORA + A_LORA + G_LORA
SHIFT_W = 3 * RWKV_WIDTH + LORA_W
IDX_HEADS = 8
IDX_DIM = 64
ATTN_PROJ_W = ATTN_WIDTH + 2 * KV_W + IDX_HEADS * IDX_DIM + IDX_DIM + IDX_HEADS
TOPK_MAX = 256
ROPE_THETA = 500000.0
ROT = HEAD_DIM // 4
ROT_HALF = ROT // 2
PEER_HEADS = 8
PEER_DKEY = 128
N_KEYS = 128
PEER_TOPK = 16
PEER_SLOTS = PEER_HEADS * PEER_TOPK
DEPTH = 2
DEEPNORM_ALPHA = (2.0 * DEPTH) ** 0.25
PAGE_SIZE = 128
LN_EPS = 1e-5
GN_EPS = 64e-5

LANE = 128
SUBLANE = 8
Q_BLOCK = 128
INT_MIN = -(2 ** 31)
NEG_BIG = -1e30
VMEM_LIMIT = 56 * 1024 * 1024
AT_PAD_W = 1664
KI_OFF = ATTN_WIDTH + 2 * KV_W + IDX_HEADS * IDX_DIM
PAGES_PER_STEP = 8

_NT = (((1,), (1,)), ((), ()))


def _cparams(*sem):
    return pltpu.CompilerParams(dimension_semantics=sem, vmem_limit_bytes=VMEM_LIMIT)


def _split2(x):
    hi = x.astype(BF16)
    lo = (x - hi.astype(F32)).astype(BF16)
    return hi, lo


def _split3(x):
    hi = x.astype(BF16)
    r1 = x - hi.astype(F32)
    mid = r1.astype(BF16)
    lo = (r1 - mid.astype(F32)).astype(BF16)
    return hi, mid, lo


def _dot3(a, b, dims=None):
    ah, al = _split2(a)
    bh, bl = _split2(b)
    if dims is None:
        d = lambda p, q: jnp.dot(p, q, preferred_element_type=F32)
    else:
        d = lambda p, q: lax.dot_general(p, q, dims, preferred_element_type=F32)
    return d(ah, bh) + d(al, bh) + d(ah, bl)


def _dot_sel(x, m):
    h, mid, lo = _split3(x)
    d = lambda p: jnp.dot(p, m, preferred_element_type=F32)
    return d(h) + d(mid) + d(lo)


def _f2key(x):
    x = jnp.where(x == 0.0, 0.0, x)
    b = lax.bitcast_convert_type(x, I32)
    return b ^ ((b >> 31) & 0x7FFFFFFF)


def _layer_norm(z, g, b):
    mu = jnp.mean(z, axis=-1, keepdims=True)
    zc = z - mu
    var = jnp.mean(zc * zc, axis=-1, keepdims=True)
    return zc * lax.rsqrt(var + LN_EPS) * g + b


def _mm_kernel(x_ref, w_ref, o_ref):
    o_ref[...] = jnp.dot(x_ref[...].astype(BF16), w_ref[...], preferred_element_type=F32)


def _matmul(x, w, tm):
    m, k = x.shape
    n = w.shape[1]
    tm = min(tm, m)
    return pl.pallas_call(
        _mm_kernel,
        grid=(m // tm,),
        in_specs=[pl.BlockSpec((tm, k), lambda i: (i, 0)),
                  pl.BlockSpec((k, n), lambda i: (0, 0))],
        out_specs=pl.BlockSpec((tm, n), lambda i: (i, 0)),
        out_shape=jax.ShapeDtypeStruct((m, n), F32),
        compiler_params=_cparams("parallel"),
        name="proj_matmul",
    )(x, w)


def _rwkv_prep_kernel(t_real, tt, shift, p_ref, prev_ref, mu_ref, w0_ref, a0_ref, lwh_ref, lwl_ref,
                      kk_ref, ka_ref, rk_ref, bd_ref,
                      r_o, w_o, kt_o, kko_o, b_o, v_o, bonus_o, g_o, carry_ref):
    j = pl.program_id(1)
    pf = p_ref[0]
    if shift:
        @pl.when(j == 0)
        def _():
            carry_ref[...] = prev_ref[0]
        row = lax.broadcasted_iota(I32, pf.shape, 0)
        prev = jnp.where(row == 0, carry_ref[...], pltpu.roll(pf, 1, 0))
        carry_ref[...] = pf[tt - 1:tt, :]
    else:
        prev = prev_ref[0]
    xs = pf + mu_ref[...] * (prev - pf)
    r = xs[:, 0:RWKV_WIDTH]
    k = xs[:, RWKV_WIDTH:2 * RWKV_WIDTH]
    v = xs[:, 2 * RWKV_WIDTH:3 * RWKV_WIDTH]
    z = xs[:, 3 * RWKV_WIDTH:SHIFT_W]
    lane = lax.broadcasted_iota(I32, z.shape, 1)
    zt = jnp.where(lane < W_LORA, jnp.tanh(z),
                   jnp.where(lane < W_LORA + A_LORA, z, jax.nn.sigmoid(z)))
    zh, zl = _split2(zt)
    d = lambda p, q: jnp.dot(p, q, preferred_element_type=F32)
    lo = d(zh, lwh_ref[...]) + d(zl, lwh_ref[...]) + d(zh, lwl_ref[...])
    w_raw = w0_ref[...] + lo[:, 0:RWKV_WIDTH]
    a = jax.nn.sigmoid(a0_ref[...] + lo[:, RWKV_WIDTH:2 * RWKV_WIDTH])
    g = lo[:, 2 * RWKV_WIDTH:3 * RWKV_WIDTH]
    decay = jnp.exp(-math.exp(-0.5) * jax.nn.sigmoid(w_raw))
    bd = bd_ref[...]
    kk = k * kk_ref[...]
    kk = kk / jnp.maximum(jnp.sqrt(_dot_sel(kk * kk, bd)), 1e-12)
    kt = k * (1.0 + (a - 1.0) * ka_ref[...])
    bonus = _dot_sel(r * kt * rk_ref[...], bd) * v
    pos = j * tt + lax.broadcasted_iota(I32, r.shape, 0)
    valid = pos < t_real
    r_o[0] = r
    w_o[0] = jnp.where(valid, decay, 1.0)
    kt_o[0] = jnp.where(valid, kt, 0.0)
    kko_o[0] = jnp.where(valid, kk, 0.0)
    b_o[0] = jnp.where(valid, kk * a, 0.0)
    v_o[0] = v
    bonus_o[0] = bonus
    g_o[0] = g


def _rwkv_prep(p_rw, prev, t_real, shift, wts, tt):
    bsz, tp, _ = p_rw.shape
    tt = min(tt, tp)
    row = lambda n: pl.BlockSpec((1, n), lambda b, j: (0, 0))
    full = lambda a: pl.BlockSpec(a.shape, lambda b, j: (0, 0))
    tok = lambda n: pl.BlockSpec((1, tt, n), lambda b, j: (b, j, 0))
    prev_spec = pl.BlockSpec((1, 1, SHIFT_W), lambda b, j: (b, 0, 0)) if shift else tok(SHIFT_W)
    outs = pl.pallas_call(
        functools.partial(_rwkv_prep_kernel, t_real, tt, shift),
        grid=(bsz, tp // tt),
        in_specs=[tok(SHIFT_W), prev_spec, row(SHIFT_W), row(RWKV_WIDTH), row(RWKV_WIDTH),
                  full(wts["lora_hi"]), full(wts["lora_lo"]),
                  row(RWKV_WIDTH), row(RWKV_WIDTH), row(RWKV_WIDTH), full(wts["bd"])],
        out_specs=[tok(RWKV_WIDTH)] * 8,
        out_shape=[jax.ShapeDtypeStruct((bsz, tp, RWKV_WIDTH), F32)] * 8,
        scratch_shapes=[pltpu.VMEM((1, SHIFT_W), F32)],
        compiler_params=_cparams("parallel", "arbitrary"),
        name="rwkv_prep",
    )(p_rw, prev, wts["mu"], wts["w0"], wts["a0"], wts["lora_hi"], wts["lora_lo"],
      wts["k_k"], wts["k_a"], wts["r_k"], wts["bd"])
    return outs


def _wkv_kernel(bb, tc, r_ref, w_ref, kt_ref, kk_ref, b_ref, v_ref, s0_ref, o_ref, sf_ref, s_ref):
    c = pl.program_id(1)

    @pl.when(c == 0)
    def _():
        s_ref[...] = s0_ref[...]

    lane = lax.broadcasted_iota(I32, (HEAD_DIM, LANE), 1)
    row = lax.broadcasted_iota(I32, (HEAD_DIM, LANE), 0)
    lo = lane < HEAD_DIM
    e0 = lane == row
    e1 = lane == row + HEAD_DIM
    e01 = e0 | e1
    r128 = lax.broadcasted_iota(I32, (LANE, LANE), 0)
    l128 = lax.broadcasted_iota(I32, (LANE, LANE), 1)
    half_ones = ((r128 >> 6) == (l128 >> 6)).astype(BF16)
    npair = RWKV_HEADS // 2

    def half_sums_mxu(parts, n_split):
        res = jnp.dot(jnp.concatenate(parts, axis=0), half_ones, preferred_element_type=F32)
        out = []
        for i in range(len(parts) // n_split):
            acc = res[i * n_split * HEAD_DIM:(i * n_split + 1) * HEAD_DIM]
            for p in range(1, n_split):
                acc = acc + res[(i * n_split + p) * HEAD_DIM:(i * n_split + p + 1) * HEAD_DIM]
            out.append(acc)
        return out

    def group(gi, carry):
        t0 = pl.multiple_of(gi * SUBLANE, SUBLANE)
        rows = pl.ds(t0, SUBLANE)
        for b in range(bb):
            blk = lambda ref: [ref[b, rows, j * LANE:(j + 1) * LANE] for j in range(npair)]
            kk8, w8, b8, kt8, v8, r8 = blk(kk_ref), blk(w_ref), blk(b_ref), blk(kt_ref), blk(v_ref), blk(r_ref)
            parts = []
            for j in range(npair):
                vh = v8[j].astype(BF16).astype(F32)
                r1 = v8[j] - vh
                vm = r1.astype(BF16).astype(F32)
                pieces = (vh, vm, r1 - vm)
                for u in range(SUBLANE):
                    parts += [jnp.where(e01, pc[u:u + 1], 0.0).astype(BF16) for pc in pieces]
            vcols = half_sums_mxu(parts, 3)
            st = [s_ref[b, j] for j in range(npair)]
            qparts = [[] for _ in range(npair)]
            for u in range(SUBLANE):
                prods = [st[j] * kk8[j][u:u + 1] for j in range(npair)]
                sums = [(jnp.sum(jnp.where(lo, p, 0.0), axis=-1, keepdims=True),
                         jnp.sum(jnp.where(lo, 0.0, p), axis=-1, keepdims=True)) for p in prods]
                for j in range(npair):
                    skk = jnp.where(lo, sums[j][0], sums[j][1])
                    s = st[j] * w8[j][u:u + 1] - skk * b8[j][u:u + 1] + vcols[j * SUBLANE + u] * kt8[j][u:u + 1]
                    st[j] = s
                    q = s * r8[j][u:u + 1]
                    qh = q.astype(BF16)
                    qparts[j] += [qh, (q - qh.astype(F32)).astype(BF16)]
            for j in range(npair):
                s_ref[b, j] = st[j]
                ocols = half_sums_mxu(qparts[j], 2)
                orows = [jnp.sum(jnp.where(e01, oc, 0.0), axis=0, keepdims=True) for oc in ocols]
                o_ref[b, rows, j * LANE:(j + 1) * LANE] = jnp.concatenate(orows, axis=0)
        return carry

    lax.fori_loop(0, tc // SUBLANE, group, 0)

    @pl.when(c == pl.num_programs(1) - 1)
    def _():
        sf_ref[...] = s_ref[...]


def _pair_state(s):
    b = s.shape[0]
    return (s.reshape(b, RWKV_HEADS // 2, 2, HEAD_DIM, HEAD_DIM)
            .transpose(0, 1, 3, 2, 4).reshape(b, RWKV_HEADS // 2, HEAD_DIM, LANE))


def _unpair_state(s):
    b = s.shape[0]
    return (s.reshape(b, RWKV_HEADS // 2, HEAD_DIM, 2, HEAD_DIM)
            .transpose(0, 1, 3, 2, 4).reshape(b, RWKV_HEADS, HEAD_DIM, HEAD_DIM))


def _wkv(r, w, kt, kk, bv, v, s0, bb, tc):
    bsz, tp, _ = r.shape
    tc = min(tc, tp)
    tok = pl.BlockSpec((bb, tc, RWKV_WIDTH), lambda i, c: (i, c, 0))
    st = pl.BlockSpec((bb, RWKV_HEADS // 2, HEAD_DIM, LANE), lambda i, c: (i, 0, 0, 0))
    o, sf = pl.pallas_call(
        functools.partial(_wkv_kernel, bb, tc),
        grid=(bsz // bb, tp // tc),
        in_specs=[tok] * 6 + [st],
        out_specs=[tok, st],
        out_shape=[jax.ShapeDtypeStruct((bsz, tp, RWKV_WIDTH), F32),
                   jax.ShapeDtypeStruct((bsz, RWKV_HEADS // 2, HEAD_DIM, LANE), F32)],
        scratch_shapes=[pltpu.VMEM((bb, RWKV_HEADS // 2, HEAD_DIM, LANE), F32)],
        compiler_params=_cparams("parallel", "arbitrary"),
        name="wkv_scan",
    )(r, w, kt, kk, bv, v, _pair_state(s0))
    return o, _unpair_state(sf)


def _rope(x, c, sa, sb):
    w = x.shape[1]
    return x * c + pltpu.roll(x, w - ROT_HALF, 1) * sa + pltpu.roll(x, ROT_HALF, 1) * sb


def _attn_prep_kernel(p_ref, c_ref, sa_ref, sb_ref, g_ref, b_ref, q_o, k_o, qi_o, ki_o):
    c1, sa1, sb1 = c_ref[...], sa_ref[...], sb_ref[...]
    rep = lambda t, n: jnp.concatenate([t] * n, axis=1)
    nq = ATTN_WIDTH // LANE
    nk = KV_W // LANE
    q_o[...] = _rope(p_ref[:, 0:ATTN_WIDTH], rep(c1, nq), rep(sa1, nq), rep(sb1, nq))
    k_o[...] = _rope(p_ref[:, ATTN_WIDTH:ATTN_WIDTH + KV_W], rep(c1, nk), rep(sa1, nk), rep(sb1, nk))
    qi0 = ATTN_WIDTH + 2 * KV_W
    qi_o[...] = _rope(p_ref[:, qi0:qi0 + IDX_HEADS * IDX_DIM], rep(c1, nq), rep(sa1, nq), rep(sb1, nq))
    x = p_ref[:, KI_OFF:KI_OFF + LANE]
    lane = lax.broadcasted_iota(I32, x.shape, 1)
    isk = lane < IDX_DIM
    mu = jnp.sum(jnp.where(isk, x, 0.0), axis=-1, keepdims=True) * (1.0 / IDX_DIM)
    xc = jnp.where(isk, x - mu, 0.0)
    var = jnp.sum(xc * xc, axis=-1, keepdims=True) * (1.0 / IDX_DIM)
    y = xc * lax.rsqrt(var + LN_EPS) * g_ref[...] + b_ref[...]
    y = _rope(y, jnp.where(isk, c1, 1.0), jnp.where(isk, sa1, 0.0), jnp.where(isk, sb1, 0.0))
    ki_o[...] = jnp.where(isk, y, x)


def _attn_prep(p_at, tabs, idx_g, idx_b, tm):
    n = p_at.shape[0]
    tm = min(tm, n)
    tpb = tabs[0].shape[0] // tm
    tok = lambda w: pl.BlockSpec((tm, w), lambda i: (i, 0))
    tab = pl.BlockSpec((tm, LANE), lambda i: (i % tpb, 0))
    row = pl.BlockSpec((1, LANE), lambda i: (0, 0))
    return pl.pallas_call(
        _attn_prep_kernel,
        grid=(n // tm,),
        in_specs=[tok(AT_PAD_W), tab, tab, tab, row, row],
        out_specs=[tok(ATTN_WIDTH), tok(KV_W), tok(IDX_HEADS * IDX_DIM), tok(LANE)],
        out_shape=[jax.ShapeDtypeStruct((n, ATTN_WIDTH), F32), jax.ShapeDtypeStruct((n, KV_W), F32),
                   jax.ShapeDtypeStruct((n, IDX_HEADS * IDX_DIM), F32), jax.ShapeDtypeStruct((n, LANE), F32)],
        compiler_params=_cparams("parallel"),
        name="attn_prep",
    )(p_at, tabs[0], tabs[1], tabs[2], idx_g, idx_b)


def _rope_tables(pos):
    inv = ROPE_THETA ** (-jnp.arange(ROT_HALF, dtype=F32) * 2.0 / ROT)
    ang = pos.astype(F32)[:, None] * inv[None, :]
    cos, sin = jnp.cos(ang), jnp.sin(ang)
    n = pos.shape[0]
    rest = HEAD_DIM - ROT
    c = jnp.concatenate([cos, cos, jnp.ones((n, rest), F32)], axis=1)
    sa = jnp.concatenate([-sin, jnp.zeros((n, rest + ROT_HALF), F32)], axis=1)
    sb = jnp.concatenate([jnp.zeros((n, ROT_HALF), F32), sin, jnp.zeros((n, rest), F32)], axis=1)
    two = lambda t: jnp.concatenate([t, t], axis=1)
    return two(c), two(sa), two(sb)


def _select_bounds(key_ref, n_tiles, rows, kt, k_sel, idx_bits):
    def count(pred):
        def body(i, acc):
            off = pl.multiple_of(i * kt, kt)
            idx = off + lax.broadcasted_iota(I32, (rows, kt), 1)
            hit = jnp.where(pred(key_ref[:, pl.ds(off, kt)], idx), 1.0, 0.0)
            for c in range(kt // LANE):
                acc = acc + hit[:, c * LANE:(c + 1) * LANE]
            return acc
        acc = lax.fori_loop(0, n_tiles, body, jnp.zeros((rows, LANE), F32))
        return jnp.sum(acc, axis=-1, keepdims=True)

    def thr_bit(i, res):
        cand = res + jnp.left_shift(jnp.int32(1), 31 - i)
        c = count(lambda key, idx: key >= cand)
        return jnp.where(c >= k_sel, cand, res)

    thr = lax.fori_loop(0, 32, thr_bit, jnp.full((rows, 1), INT_MIN, I32))
    n_ge = count(lambda key, idx: key >= thr)

    def tie_search():
        need = k_sel - count(lambda key, idx: key > thr)

        def idx_bit(i, res):
            cand = res | jnp.left_shift(jnp.int32(1), idx_bits - 1 - i)
            c = count(lambda key, idx: (key == thr) & (idx < cand))
            return jnp.where(c < need, cand, res)

        return lax.fori_loop(0, idx_bits, idx_bit, jnp.zeros((rows, 1), I32))

    jmax = lax.cond(jnp.max(n_ge) > k_sel, tie_search,
                    lambda: jnp.full((rows, 1), 2 ** idx_bits - 1, I32))
    return thr, jmax


def _dsa_prompt_kernel(kt, n_sel, idx_bits, q_ref, qi_ref, wi_ref, k_ref, v_ref, ki_ref, o_ref,
                       key_ref, m_ref, l_ref, acc_ref):
    i = pl.program_id(1)
    n_kt = (i * Q_BLOCK + Q_BLOCK + kt - 1) // kt
    qi = qi_ref[0]
    qis = jnp.concatenate([qi[:, h * IDX_DIM:(h + 1) * IDX_DIM] for h in range(IDX_HEADS)], axis=0)
    wi = wi_ref[0] * IDX_HEADS ** -0.5
    qpos = i * Q_BLOCK + lax.broadcasted_iota(I32, (Q_BLOCK, kt), 0)
    lane = lax.broadcasted_iota(I32, (Q_BLOCK, kt), 1)

    def scores(t, carry):
        off = pl.multiple_of(t * kt, kt)
        s = lax.dot_general(qis, ki_ref[0, pl.ds(off, kt), :], _NT, preferred_element_type=F32)
        acc = jnp.zeros((Q_BLOCK, kt), F32)
        for h in range(IDX_HEADS):
            acc = acc + wi[:, h:h + 1] * jnp.maximum(s[h * Q_BLOCK:(h + 1) * Q_BLOCK] * IDX_DIM ** -0.5, 0.0)
        key_ref[:, pl.ds(off, kt)] = jnp.where(off + lane <= qpos, _f2key(acc), INT_MIN)
        return carry

    lax.fori_loop(0, n_kt, scores, 0)
    thr, jmax = _select_bounds(key_ref, n_kt, Q_BLOCK, kt, n_sel, idx_bits)

    q = q_ref[0].astype(F32)
    grp = lax.broadcasted_iota(I32, (Q_BLOCK, KV_W), 1) >> 6
    rep = ATTN_HEADS // KV_HEADS

    def expand(h):
        qh = q[:, h * HEAD_DIM:(h + 1) * HEAD_DIM]
        return jnp.where(grp == h // rep, jnp.concatenate([qh] * KV_HEADS, axis=1), 0.0).astype(BF16)

    qe = [expand(h) for h in range(ATTN_HEADS)]
    m_ref[...] = jnp.full(m_ref.shape, NEG_BIG, F32)
    l_ref[...] = jnp.zeros(l_ref.shape, F32)
    acc_ref[...] = jnp.zeros(acc_ref.shape, F32)

    def attend(t, carry):
        off = pl.multiple_of(t * kt, kt)
        key = key_ref[:, pl.ds(off, kt)]
        kidx = off + lane
        sel = (kidx <= qpos) & ((key > thr) | ((key == thr) & (kidx <= jmax)))
        kt_tile = k_ref[0, pl.ds(off, kt), :]
        vt_tile = v_ref[0, pl.ds(off, kt), :]
        qk = lambda h: lax.dot_general(qe[h], kt_tile, _NT, preferred_element_type=F32)

        def finish(h, p, alpha):
            acc_ref[h] = alpha * acc_ref[h] + jnp.dot(p, vt_tile, preferred_element_type=F32)

        nxt = qk(0)
        pending = None
        for h in range(ATTN_HEADS):
            lg = nxt
            if h + 1 < ATTN_HEADS:
                nxt = qk(h + 1)
            lg = jnp.where(sel, lg * HEAD_DIM ** -0.5, NEG_BIG)
            m = m_ref[h]
            mn = jnp.maximum(m, jnp.max(lg, axis=-1, keepdims=True))
            p = jnp.where(sel, jnp.exp(lg - mn), 0.0)
            alpha = jnp.exp(m - mn)
            l_ref[h] = alpha * l_ref[h] + jnp.sum(p, axis=-1, keepdims=True)
            m_ref[h] = mn
            if pending is not None:
                finish(*pending)
            pending = (h, p.astype(BF16), alpha)
        finish(*pending)
        return carry

    lax.fori_loop(0, n_kt, attend, 0)
    pieces = []
    for h in range(ATTN_HEADS):
        g = h // rep
        pieces.append(acc_ref[h][:, g * HEAD_DIM:(g + 1) * HEAD_DIM] / l_ref[h])
    o_ref[0] = jnp.concatenate(pieces, axis=1)


def _dsa_prompt(q, qi, wi, k, v, ki, t_real):
    bsz, tp, _ = q.shape
    kt = 640 if tp % 640 == 0 else LANE
    n_sel = min(TOPK_MAX, t_real // 4)
    idx_bits = max(1, (tp - 1).bit_length())
    blk = lambda w: pl.BlockSpec((1, Q_BLOCK, w), lambda b, i: (b, i, 0))
    seq = lambda w: pl.BlockSpec((1, tp, w), lambda b, i: (b, 0, 0))
    return pl.pallas_call(
        functools.partial(_dsa_prompt_kernel, kt, n_sel, idx_bits),
        grid=(bsz, tp // Q_BLOCK),
        in_specs=[blk(ATTN_WIDTH), blk(IDX_HEADS * IDX_DIM), blk(IDX_HEADS),
                  seq(KV_W), seq(KV_W), seq(IDX_DIM)],
        out_specs=blk(ATTN_WIDTH),
        out_shape=jax.ShapeDtypeStruct((bsz, tp, ATTN_WIDTH), F32),
        scratch_shapes=[pltpu.VMEM((Q_BLOCK, tp), I32),
                        pltpu.VMEM((ATTN_HEADS, Q_BLOCK, 1), F32),
                        pltpu.VMEM((ATTN_HEADS, Q_BLOCK, 1), F32),
                        pltpu.VMEM((ATTN_HEADS, Q_BLOCK, KV_W), F32)],
        compiler_params=_cparams("parallel", "arbitrary"),
        name="dsa_prompt",
    )(q, qi, wi, k, v, ki)


def _dsa_s_scores_kernel(pps, pt_ref, qi_ref, wi_ref, *refs):
    ci_refs, o_ref = refs[:pps], refs[pps]
    qi = qi_ref[0]
    w = wi_ref[0] * IDX_HEADS ** -0.5
    for u in range(pps):
        s = _dot3(qi, ci_refs[u][0, 0], _NT)
        sc = jnp.sum(w * jnp.maximum(s * IDX_DIM ** -0.5, 0.0), axis=0, keepdims=True)
        o_ref[0, :, u * PAGE_SIZE:(u + 1) * PAGE_SIZE] = sc


def _dsa_s_scores(page_table, qi3, wi3, cache_idx, layer):
    bd, npages = page_table.shape
    pps = PAGES_PER_STEP
    page = lambda u: pl.BlockSpec((1, 1, PAGE_SIZE, IDX_DIM),
                                  lambda b, p, pt: (layer, pt[b * npages + p * pps + u], 0, 0))
    gs = pltpu.PrefetchScalarGridSpec(
        num_scalar_prefetch=1,
        grid=(bd, npages // pps),
        in_specs=[pl.BlockSpec((1, IDX_HEADS, IDX_DIM), lambda b, p, pt: (b, 0, 0)),
                  pl.BlockSpec((1, IDX_HEADS, 1), lambda b, p, pt: (b, 0, 0))] + [page(u) for u in range(pps)],
        out_specs=pl.BlockSpec((1, 1, pps * PAGE_SIZE), lambda b, p, pt: (b, 0, p)),
    )
    return pl.pallas_call(
        functools.partial(_dsa_s_scores_kernel, pps),
        grid_spec=gs,
        out_shape=jax.ShapeDtypeStruct((bd, 1, npages * PAGE_SIZE), F32),
        compiler_params=_cparams("parallel", "arbitrary"),
        name="dsa_decode_scores",
    )(page_table.reshape(-1), qi3, wi3, *([cache_idx] * pps))


def _dsa_s_bounds_kernel(past, n_sel, idx_bits, sc_ref, qi_ref, kiw_ref, hm_ref, thr_o, j_o, kn_o, key_ref):
    rows = sc_ref.shape[0]
    kiw = kiw_ref[...]
    lane = lax.broadcasted_iota(I32, kiw.shape, 1)
    rolled = pltpu.roll(kiw, IDX_DIM, 1)
    ki2 = jnp.where(lane < IDX_DIM, kiw, rolled)
    w8 = jnp.where(lane < IDX_HEADS, rolled, 0.0) * IDX_HEADS ** -0.5
    prod = qi_ref[...] * jnp.concatenate([ki2] * (IDX_HEADS // 2), axis=1)
    s = _dot3(prod, hm_ref[...])
    new = jnp.sum(w8 * jnp.maximum(s * IDX_DIM ** -0.5, 0.0), axis=-1, keepdims=True)
    knew = _f2key(new)
    key_ref[:, 0:past] = _f2key(sc_ref[...])
    key_ref[:, past:past + LANE] = jnp.where(lane == 0, knew, INT_MIN)
    thr, jmax = _select_bounds(key_ref, (past + LANE) // LANE, rows, LANE, n_sel, idx_bits)
    thr_o[...] = thr
    j_o[...] = jmax
    kn_o[...] = knew


def _dsa_s_bounds(scores, qi, kiw, hm, past):
    rows = scores.shape[0]
    n_sel = min(TOPK_MAX, (past + 1) // 4)
    idx_bits = (past + LANE - 1).bit_length()
    out = jax.ShapeDtypeStruct((rows, 1), I32)
    return pl.pallas_call(
        functools.partial(_dsa_s_bounds_kernel, past, n_sel, idx_bits),
        out_shape=[out, out, out],
        scratch_shapes=[pltpu.VMEM((rows, past + LANE), I32)],
        compiler_params=pltpu.CompilerParams(vmem_limit_bytes=VMEM_LIMIT),
        name="dsa_decode_bounds",
    )(scores, qi, kiw, hm)


def _dsa_s_attend_kernel(pps, past, pt_ref, thr_ref, j_ref, kn_ref, qe_ref, sc_ref, kn_row_ref, vn_row_ref,
                         fold_ref, *refs):
    ck, cv, o_ref = refs[:pps], refs[pps:2 * pps], refs[2 * pps]
    m_ref, l_ref, acc_ref = refs[2 * pps + 1:]
    b = pl.program_id(0)
    p = pl.program_id(1)

    @pl.when(p == 0)
    def _():
        m_ref[...] = jnp.full(m_ref.shape, NEG_BIG, F32)
        l_ref[...] = jnp.zeros(l_ref.shape, F32)
        acc_ref[...] = jnp.zeros(acc_ref.shape, F32)

    thr, jmax = thr_ref[b], j_ref[b]
    qe = qe_ref[0]
    qeb = qe.astype(BF16)
    lane = lax.broadcasted_iota(I32, (1, PAGE_SIZE), 1)
    for u in range(pps):
        lg = lax.dot_general(qeb, ck[u][0, 0].astype(BF16), _NT,
                             preferred_element_type=F32) * HEAD_DIM ** -0.5
        key = _f2key(sc_ref[0, :, u * PAGE_SIZE:(u + 1) * PAGE_SIZE])
        kidx = (p * pps + u) * PAGE_SIZE + lane
        sel = (key > thr) | ((key == thr) & (kidx <= jmax))
        lg = jnp.where(sel, lg, NEG_BIG)
        m = m_ref[...]
        mn = jnp.maximum(m, jnp.max(lg, axis=-1, keepdims=True))
        pr = jnp.where(sel, jnp.exp(lg - mn), 0.0)
        alpha = jnp.exp(m - mn)
        l_ref[...] = alpha * l_ref[...] + jnp.sum(pr, axis=-1, keepdims=True)
        acc_ref[...] = alpha * acc_ref[...] + jnp.dot(pr.astype(BF16), cv[u][0, 0].astype(BF16),
                                                      preferred_element_type=F32)
        m_ref[...] = mn

    @pl.when(p == pl.num_programs(1) - 1)
    def _():
        knew = kn_ref[b]
        sel_new = (knew > thr) | ((knew == thr) & (past <= jmax))
        lg = jnp.sum(qe * kn_row_ref[0], axis=-1, keepdims=True) * HEAD_DIM ** -0.5
        m = m_ref[...]
        mn = jnp.where(sel_new, jnp.maximum(m, lg), m)
        pr = jnp.where(sel_new, jnp.exp(lg - mn), 0.0)
        alpha = jnp.exp(m - mn)
        l = alpha * l_ref[...] + pr
        acc = alpha * acc_ref[...] + pr * vn_row_ref[0]
        rowi = lax.broadcasted_iota(I32, acc.shape, 0)
        lanei = lax.broadcasted_iota(I32, acc.shape, 1)
        own = jnp.where((lanei >> 6) == (rowi >> 1), acc / l, 0.0)
        o_ref[0] = _dot_sel(own, fold_ref[...])


def _dsa_s_attend(page_table, thr, jmax, knew, qe, scores, k_new, v_new, fold, cache_k, cache_v, layer):
    bd, npages = page_table.shape
    past = npages * PAGE_SIZE
    pps = PAGES_PER_STEP
    page = lambda u: pl.BlockSpec((1, 1, PAGE_SIZE, KV_W),
                                  lambda b, p, pt, t, j, kn: (layer, pt[b * npages + p * pps + u], 0, 0))
    per_b = lambda s: pl.BlockSpec((1,) + s, lambda b, p, pt, t, j, kn: (b, 0, 0))
    gs = pltpu.PrefetchScalarGridSpec(
        num_scalar_prefetch=4,
        grid=(bd, npages // pps),
        in_specs=[per_b((ATTN_HEADS, KV_W)),
                  pl.BlockSpec((1, 1, pps * PAGE_SIZE), lambda b, p, pt, t, j, kn: (b, 0, p)),
                  per_b((1, KV_W)), per_b((1, KV_W)),
                  pl.BlockSpec(fold.shape, lambda b, p, pt, t, j, kn: (0, 0))]
                 + [page(u) for u in range(pps)] * 2,
        out_specs=per_b((ATTN_HEADS, HEAD_DIM)),
        scratch_shapes=[pltpu.VMEM((ATTN_HEADS, 1), F32), pltpu.VMEM((ATTN_HEADS, 1), F32),
                        pltpu.VMEM((ATTN_HEADS, KV_W), F32)],
    )
    return pl.pallas_call(
        functools.partial(_dsa_s_attend_kernel, pps, past),
        grid_spec=gs,
        out_shape=jax.ShapeDtypeStruct((bd, ATTN_HEADS, HEAD_DIM), F32),
        compiler_params=_cparams("parallel", "arbitrary"),
        name="dsa_decode_attend",
    )(page_table.reshape(-1), thr.reshape(-1), jmax.reshape(-1), knew.reshape(-1),
      qe, scores, k_new, v_new, fold, *([cache_k] * pps), *([cache_v] * pps))


def _merge_ln_kernel(o_ref, bonus_ref, g_ref, at_ref, x_ref, wt_ref, wb_ref, xg_ref, xb_ref,
                     lg_ref, lb_ref, bd_ref, out_ref):
    bd = bd_ref[...]
    o = o_ref[...]
    mean = _dot_sel(o, bd) * (1.0 / HEAD_DIM)
    oc = o - mean
    var = _dot_sel(oc * oc, bd) * (1.0 / HEAD_DIM)
    rw = (oc * lax.rsqrt(var + GN_EPS) * xg_ref[...] + xb_ref[...] + bonus_ref[...]) * g_ref[...]
    f = (jnp.dot(rw.astype(BF16), wt_ref[...], preferred_element_type=F32)
         + jnp.dot(at_ref[...].astype(BF16), wb_ref[...], preferred_element_type=F32))
    out_ref[...] = _layer_norm(DEEPNORM_ALPHA * x_ref[...] + f, lg_ref[...], lb_ref[...])


def _merge_ln(o, bonus, g, at, x, wts, tm):
    n = x.shape[0]
    tm = min(tm, n)
    tok = lambda w: pl.BlockSpec((tm, w), lambda i: (i, 0))
    full = lambda a: pl.BlockSpec(a.shape, lambda i: (0, 0))
    ws = [wts["wo_top"], wts["wo_bot"], wts["lnx_g"], wts["lnx_b"], wts["ln1_g"], wts["ln1_b"], wts["bd"]]
    return pl.pallas_call(
        _merge_ln_kernel,
        grid=(n // tm,),
        in_specs=[tok(RWKV_WIDTH)] * 3 + [tok(ATTN_WIDTH), tok(D_MODEL)] + [full(a) for a in ws],
        out_specs=tok(D_MODEL),
        out_shape=jax.ShapeDtypeStruct((n, D_MODEL), F32),
        compiler_params=_cparams("parallel"),
        name="merge_ln",
    )(o, bonus, g, at, x, *ws)


def _take_top(src_ref, n_rows, val_ref, idx_ref):
    shape = src_ref.shape
    row = lax.broadcasted_iota(I32, shape, 1)

    def body(a, carry):
        sv = src_ref[...]
        m = jnp.max(sv, axis=1, keepdims=True)
        idx = jnp.min(jnp.where(sv == m, row, n_rows), axis=1, keepdims=True)
        val_ref[:, pl.ds(a, 1), :] = m
        idx_ref[:, pl.ds(a, 1), :] = idx
        src_ref[...] = jnp.where(row == idx, -jnp.inf, sv)
        return carry

    lax.fori_loop(0, PEER_TOPK, body, 0)


def _peer_route_kernel(x_ref, wq_ref, sk0_ref, sk1_ref, e_o, g_o, s_ref, t_ref, i_ref, c_ref, ts_ref, ic_ref):
    q = jnp.dot(x_ref[...].astype(BF16), wq_ref[...], preferred_element_type=F32)
    for h in range(PEER_HEADS):
        qh = q[:, h * PEER_DKEY:(h + 1) * PEER_DKEY]
        s_ref[h] = _dot3(sk0_ref[...], qh, _NT)
        s_ref[PEER_HEADS + h] = _dot3(sk1_ref[...], qh, _NT)
    _take_top(s_ref, N_KEYS, t_ref, i_ref)
    t1, t2 = t_ref[0:PEER_HEADS], t_ref[PEER_HEADS:2 * PEER_HEADS]
    c_ref[...] = jnp.concatenate([t1[:, a:a + 1, :] + t2 for a in range(PEER_TOPK)], axis=1)
    _take_top(c_ref, PEER_TOPK * PEER_TOPK, ts_ref, ic_ref)
    ic = ic_ref[...]
    i1, i2 = i_ref[0:PEER_HEADS], i_ref[PEER_HEADS:2 * PEER_HEADS]
    ia, ib = ic >> 4, ic & (PEER_TOPK - 1)
    e = jnp.zeros(ic.shape, I32)
    for a in range(PEER_TOPK):
        e = e + jnp.where(ia == a, i1[:, a:a + 1, :] * N_KEYS, 0) + jnp.where(ib == a, i2[:, a:a + 1, :], 0)
    ts = ts_ref[...]
    ex = jnp.exp(ts - jnp.max(ts, axis=1, keepdims=True))
    e_o[0] = e
    g_o[0] = ex / jnp.sum(ex, axis=1, keepdims=True)


def _peer_route(x, wq, sk0, sk1, tt):
    n = x.shape[0]
    nt = n // tt
    full = lambda a: pl.BlockSpec(a.shape, lambda i: (0, 0))
    out = pl.BlockSpec((1, PEER_HEADS, PEER_TOPK, tt), lambda i: (i, 0, 0, 0))
    return pl.pallas_call(
        _peer_route_kernel,
        grid=(nt,),
        in_specs=[pl.BlockSpec((tt, D_MODEL), lambda i: (i, 0)), full(wq), full(sk0), full(sk1)],
        out_specs=[out, out],
        out_shape=[jax.ShapeDtypeStruct((nt, PEER_HEADS, PEER_TOPK, tt), I32),
                   jax.ShapeDtypeStruct((nt, PEER_HEADS, PEER_TOPK, tt), F32)],
        scratch_shapes=[pltpu.VMEM((2 * PEER_HEADS, N_KEYS, tt), F32),
                        pltpu.VMEM((2 * PEER_HEADS, PEER_TOPK, tt), F32),
                        pltpu.VMEM((2 * PEER_HEADS, PEER_TOPK, tt), I32),
                        pltpu.VMEM((PEER_HEADS, PEER_TOPK * PEER_TOPK, tt), F32),
                        pltpu.VMEM((PEER_HEADS, PEER_TOPK, tt), F32),
                        pltpu.VMEM((PEER_HEADS, PEER_TOPK, tt), I32)],
        compiler_params=_cparams("parallel"),
        name="peer_route",
    )(x, wq, sk0, sk1)


def _peer_gather_kernel(tt, n_valid, idx_hbm, x_ref, g_ref, uv_hbm, lg_ref, lb_ref, o_ref,
                        idx_smem, buf0, buf1, buf2, sem, isem, y_ref):
    i = pl.program_id(0)
    n_idx = PEER_SLOTS * tt
    islot = i % 2
    bufs = (buf0, buf1, buf2)

    def idx_copy(tile, slot):
        return pltpu.make_async_copy(idx_hbm.at[tile], idx_smem.at[pl.ds(slot * n_idx, n_idx)], isem.at[slot])

    @pl.when(i == 0)
    def _():
        idx_copy(0, 0).start()

    if n_valid < tt:
        y_ref[...] = jnp.zeros(y_ref.shape, F32)
    idx_copy(i, islot).wait()

    @pl.when(i + 1 < pl.num_programs(0))
    def _():
        idx_copy(i + 1, 1 - islot).start()

    base = islot * n_idx

    def issue(t, k):
        for s in range(PEER_SLOTS):
            e = idx_smem[base + s * tt + t]
            pltpu.make_async_copy(uv_hbm.at[pl.ds(e, 1)], bufs[k].at[pl.ds(s, 1)], sem.at[k]).start()

    def wait(k):
        pltpu.make_async_copy(uv_hbm.at[pl.ds(0, PEER_SLOTS)], bufs[k], sem.at[k]).wait()

    lane = lax.broadcasted_iota(I32, (PEER_SLOTS, tt), 1)

    def compute(t, k):
        xrow = x_ref[pl.ds(t, 1), :]
        h = jnp.sum(bufs[k][:, 0:D_MODEL] * xrow, axis=-1, keepdims=True)
        gate = jnp.sum(jnp.where(lane == t, g_ref[0], 0.0), axis=-1, keepdims=True)
        coef = gate * jax.nn.gelu(h)
        y_ref[pl.ds(t, 1), :] = jnp.sum(coef * bufs[k][:, D_MODEL:2 * D_MODEL], axis=0, keepdims=True)

    depth = len(bufs)
    ahead = depth - 1
    n_main = n_valid - ahead
    assert n_main >= 0 and n_main % depth == 0
    for t in range(ahead):
        issue(t, t)

    def body(i3, carry):
        for k in range(depth):
            t = i3 * depth + k
            wait(k)
            issue(t + ahead, (k + ahead) % depth)
            compute(t, k)
        return carry

    lax.fori_loop(0, n_main // depth, body, 0)
    for t in range(n_main, n_valid):
        wait(t % depth)
        compute(t, t % depth)
    o_ref[...] = _layer_norm(DEEPNORM_ALPHA * x_ref[...] + y_ref[...], lg_ref[...], lb_ref[...])


def _peer_gather(idx, gates, x, uv, ln_g, ln_b, tt, n_valid):
    n = x.shape[0]
    nt = n // tt
    row = pl.BlockSpec((1, D_MODEL), lambda i: (0, 0))
    return pl.pallas_call(
        functools.partial(_peer_gather_kernel, tt, n_valid),
        grid=(nt,),
        in_specs=[pl.BlockSpec(memory_space=pl.ANY),
                  pl.BlockSpec((tt, D_MODEL), lambda i: (i, 0)),
                  pl.BlockSpec((1, PEER_SLOTS, tt), lambda i: (i, 0, 0)),
                  pl.BlockSpec(memory_space=pl.ANY), row, row],
        out_specs=pl.BlockSpec((tt, D_MODEL), lambda i: (i, 0)),
        out_shape=jax.ShapeDtypeStruct((n, D_MODEL), F32),
        scratch_shapes=[pltpu.SMEM((2 * PEER_SLOTS * tt,), I32)]
                       + [pltpu.VMEM((PEER_SLOTS, 2 * D_MODEL), F32)] * 3
                       + [pltpu.SemaphoreType.DMA((3,)),
                          pltpu.SemaphoreType.DMA((2,)),
                          pltpu.VMEM((tt, D_MODEL), F32)],
        compiler_params=_cparams("arbitrary"),
        name="peer_gather",
    )(idx, x, gates, uv, ln_g, ln_b)


def _peer(x, wts, tt, n_valid):
    e, gates = _peer_route(x, wts["wq"], wts["sk0"], wts["sk1"], tt)
    nt = x.shape[0] // tt
    return _peer_gather(e.reshape(nt, PEER_SLOTS * tt), gates.reshape(nt, PEER_SLOTS, tt), x,
                        wts["uv"], wts["ln2_g"], wts["ln2_b"], tt, n_valid)


def _layer_weights(l, w_in, shift_mu, decay_w0, decay_up, iclr_a0, iclr_up, gate_up, k_k, k_a, r_k,
                   lnx_g, lnx_b, idx_ln_g, idx_ln_b, w_out, ln1_g, ln1_b, ln2_g, ln2_b,
                   peer_wq, peer_subkeys, peer_u, peer_v):
    row = lambda a: a.reshape(1, -1).astype(F32)
    lora = jnp.zeros((LORA_W, 3 * RWKV_WIDTH), F32)
    lora = lora.at[0:W_LORA, 0:RWKV_WIDTH].set(decay_up[l])
    lora = lora.at[W_LORA:W_LORA + A_LORA, RWKV_WIDTH:2 * RWKV_WIDTH].set(iclr_up[l])
    lora = lora.at[W_LORA + A_LORA:, 2 * RWKV_WIDTH:].set(gate_up[l])
    lora_hi = lora.astype(BF16)
    pad_lane = lambda a: jnp.pad(a.reshape(1, -1), ((0, 0), (0, LANE - a.size)))
    half = PEER_DKEY // 2
    seg = jnp.arange(RWKV_WIDTH) // HEAD_DIM
    return dict(
        w_rw=w_in[l][:, :SHIFT_W].astype(BF16),
        w_at=jnp.pad(w_in[l][:, SHIFT_W:], ((0, 0), (0, AT_PAD_W - ATTN_PROJ_W))).astype(BF16),
        mu=row(shift_mu[l]), w0=row(decay_w0[l]), a0=row(iclr_a0[l]),
        lora_hi=lora_hi, lora_lo=(lora - lora_hi.astype(F32)).astype(BF16),
        k_k=row(k_k[l]), k_a=row(k_a[l]), r_k=row(r_k[l]),
        bd=(seg[:, None] == seg[None, :]).astype(BF16),
        lnx_g=row(lnx_g[l]), lnx_b=row(lnx_b[l]),
        idx_g=pad_lane(idx_ln_g[l]), idx_b=pad_lane(idx_ln_b[l]),
        wo_top=w_out[l][:RWKV_WIDTH].astype(BF16), wo_bot=w_out[l][RWKV_WIDTH:].astype(BF16),
        ln1_g=row(ln1_g[l]), ln1_b=row(ln1_b[l]), ln2_g=row(ln2_g[l]), ln2_b=row(ln2_b[l]),
        wq=peer_wq[l].astype(BF16),
        sk0=jnp.pad(peer_subkeys[l, 0], ((0, 0), (0, half))),
        sk1=jnp.pad(peer_subkeys[l, 1], ((0, 0), (half, 0))),
        uv=jnp.concatenate([peer_u[l], peer_v[l]], axis=1),
    )


def _tile(n, cap):
    if n <= cap:
        return n
    return max(d for d in range(8, cap + 1, 8) if n % d == 0)


def _mixer_front(x, wts, tabs):
    tm = _tile(x.shape[0], 512)
    p_rw = _matmul(x, wts["w_rw"], tm)
    p_at = _matmul(x, wts["w_at"], tm)
    q, k, qi, kiw = _attn_prep(p_at, tabs, wts["idx_g"], wts["idx_b"], _tile(tabs[0].shape[0], 640))
    v = p_at[:, ATTN_WIDTH + KV_W:ATTN_WIDTH + 2 * KV_W]
    return p_rw, q, k, v, qi, kiw


def kernel(x_prompt, x_sample, cache_k, cache_v, cache_idx_k, state_wkv, state_shift, page_table, meta_tokens, w_in, shift_mu, decay_w0, decay_up, iclr_a0, iclr_up, gate_up, k_k, k_a, r_k, lnx_g, lnx_b, idx_ln_g, idx_ln_b, w_out, ln1_g, ln1_b, ln2_g, ln2_b, peer_wq, peer_subkeys, peer_u, peer_v):
    bsz, seq, _ = x_prompt.shape
    t = seq + N_META
    tp = -(-t // LANE) * LANE
    n_p = bsz * tp
    depth = w_in.shape[0]
    bd_, ts_, _ = x_sample.shape
    assert ts_ == 1
    npages = page_table.shape[1]
    past = npages * PAGE_SIZE
    n_pool = cache_k.shape[1]
    peer_tt = LANE
    ns_pad = -(-bd_ // peer_tt) * peer_tt

    xp = jnp.concatenate([jnp.broadcast_to(meta_tokens[None], (bsz, N_META, D_MODEL)), x_prompt], axis=1)
    xp = jnp.pad(xp, ((0, 0), (0, tp - t), (0, 0))).reshape(n_p, D_MODEL)
    xs = x_sample.reshape(bd_, D_MODEL)
    tabs_p = _rope_tables(jnp.arange(tp, dtype=I32))
    tabs_s = _rope_tables(jnp.full((bd_,), past, I32))
    ck = cache_k.reshape(depth, n_pool, PAGE_SIZE, KV_W)
    cv = cache_v.reshape(depth, n_pool, PAGE_SIZE, KV_W)
    hsel = jnp.arange(IDX_HEADS * IDX_DIM)[:, None] // IDX_DIM == jnp.arange(LANE)[None, :]
    hm = hsel.astype(F32)
    fold = (jnp.arange(KV_W)[:, None] % HEAD_DIM == jnp.arange(HEAD_DIM)[None, :]).astype(BF16)
    own = (jnp.arange(KV_W)[None, :] // HEAD_DIM == jnp.arange(ATTN_HEADS)[:, None] // (ATTN_HEADS // KV_HEADS))

    k_p, v_p, ki_p, wkv_p, sh_p = [], [], [], [], []
    k_s, v_s, ki_s, wkv_s, sh_s = [], [], [], [], []
    for l in range(depth):
        wts = _layer_weights(l, w_in, shift_mu, decay_w0, decay_up, iclr_a0, iclr_up, gate_up, k_k, k_a, r_k,
                             lnx_g, lnx_b, idx_ln_g, idx_ln_b, w_out, ln1_g, ln1_b, ln2_g, ln2_b,
                             peer_wq, peer_subkeys, peer_u, peer_v)

        p_rw, q, k, v, qi, kiw = _mixer_front(xp, wts, tabs_p)
        p_rw3 = p_rw.reshape(bsz, tp, SHIFT_W)
        r_, w_, kt_, kk_, b_, vv_, bonus, g = _rwkv_prep(
            p_rw3, jnp.zeros((bsz, 1, SHIFT_W), F32), t, True, wts, LANE)
        o, s_fin = _wkv(r_, w_, kt_, kk_, b_, vv_,
                        jnp.zeros((bsz, RWKV_HEADS, HEAD_DIM, HEAD_DIM), F32), bsz, LANE)
        three = lambda a: a.reshape(bsz, tp, -1)
        at = _dsa_prompt(three(q).astype(BF16), three(qi).astype(BF16),
                         three(kiw)[:, :, IDX_DIM:IDX_DIM + IDX_HEADS],
                         three(k).astype(BF16), three(v).astype(BF16),
                         three(kiw)[:, :, :IDX_DIM].astype(BF16), t)
        flat = lambda a: a.reshape(n_p, -1)
        x1 = _merge_ln(flat(o), flat(bonus), flat(g), flat(at), xp, wts, _tile(n_p, 256))
        xp = _peer(x1, wts, peer_tt, peer_tt)
        k_p.append(three(k)[:, :t].reshape(bsz, t, KV_HEADS, HEAD_DIM))
        v_p.append(three(v)[:, :t].reshape(bsz, t, KV_HEADS, HEAD_DIM))
        ki_p.append(three(kiw)[:, :t, :IDX_DIM])
        wkv_p.append(s_fin)
        sh_p.append(p_rw3[:, t - 1])

        p_rw, q, k, v, qi, kiw = _mixer_front(xs, wts, tabs_s)
        r_, w_, kt_, kk_, b_, vv_, bonus, g = _rwkv_prep(
            p_rw[None], state_shift[l][None], bd_, False, wts, bd_)
        tc_s = 8
        padt = lambda a, c: jnp.pad(a[0][:, None, :], ((0, 0), (0, tc_s - 1), (0, 0)), constant_values=c)
        o, s_fin = _wkv(padt(r_, 0.0), padt(w_, 1.0), padt(kt_, 0.0), padt(kk_, 0.0), padt(b_, 0.0),
                        padt(vv_, 0.0), state_wkv[l].astype(F32), 4, tc_s)
        o = o[:, 0]
        scores = _dsa_s_scores(page_table, qi.reshape(bd_, IDX_HEADS, IDX_DIM),
                               kiw[:, IDX_DIM:IDX_DIM + IDX_HEADS].reshape(bd_, IDX_HEADS, 1),
                               cache_idx_k, l)
        thr, jmax, knew = _dsa_s_bounds(scores.reshape(bd_, past), qi, kiw, hm, past)
        qe = jnp.where(own[None], jnp.tile(q.reshape(bd_, ATTN_HEADS, HEAD_DIM), (1, 1, KV_HEADS)), 0.0)
        at = _dsa_s_attend(page_table, thr, jmax, knew, qe, scores, k[:, None, :], v[:, None, :], fold,
                           ck, cv, l).reshape(bd_, ATTN_WIDTH)
        x1 = _merge_ln(o, bonus[0], g[0], at, xs, wts, bd_)
        x1p = jnp.pad(x1, ((0, ns_pad - bd_), (0, 0)))
        xs = _peer(x1p, wts, peer_tt, bd_)[:bd_]
        k_s.append(k.reshape(bd_, 1, KV_HEADS, HEAD_DIM))
        v_s.append(v.reshape(bd_, 1, KV_HEADS, HEAD_DIM))
        ki_s.append(kiw[:, None, :IDX_DIM])
        wkv_s.append(s_fin)
        sh_s.append(p_rw)

    y_prompt = xp.reshape(bsz, tp, D_MODEL)[:, N_META:t]
    y_sample = xs.reshape(bd_, 1, D_MODEL)
    return (y_prompt, y_sample, jnp.stack(k_p), jnp.stack(v_p), jnp.stack(ki_p),
            jnp.stack(wkv_p).astype(state_wkv.dtype), jnp.stack(sh_p).astype(state_shift.dtype),
            jnp.stack(k_s), jnp.stack(v_s), jnp.stack(ki_s),
            jnp.stack(wkv_s).astype(state_wkv.dtype), jnp.stack(sh_s).astype(state_shift.dtype))
```

```python
import functools
import math

import jax
import jax.numpy as jnp
from jax import lax
from jax.experimental import pallas as pl
from jax.experimental.pallas import tpu as pltpu

F32 = jnp.float32
BF16 = jnp.bfloat16
I32 = jnp.int32

D_MODEL = 1024
N_META = 16
HEAD_DIM = 64
RWKV_WIDTH = D_MODEL // 2
RWKV_HEADS = RWKV_WIDTH // HEAD_DIM
ATTN_WIDTH = D_MODEL - RWKV_WIDTH
ATTN_HEADS = ATTN_WIDTH // HEAD_DIM
KV_HEADS = ATTN_HEADS // 2
KV_W = KV_HEADS * HEAD_DIM
W_LORA = 64
A_LORA = 64
G_LORA = 128
LORA_W = W_LORA + A_LORA + G_LORA
SHIFT_W = 3 * RWKV_WIDTH + LORA_W
IDX_HEADS = 8
IDX_DIM = 64
ATTN_PROJ_W = ATTN_WIDTH + 2 * KV_W + IDX_HEADS * IDX_DIM + IDX_DIM + IDX_HEADS
TOPK_MAX = 256
ROPE_THETA = 500000.0
ROT = HEAD_DIM // 4
ROT_HALF = ROT // 2
PEER_HEADS = 8
PEER_DKEY = 128
N_KEYS = 128
PEER_TOPK = 16
PEER_SLOTS = PEER_HEADS * PEER_TOPK
DEPTH = 2
DEEPNORM_ALPHA = (2.0 * DEPTH) ** 0.25
PAGE_SIZE = 128
LN_EPS = 1e-5
GN_EPS = 64e-5

LANE = 128
SUBLANE = 8
Q_BLOCK = 128
INT_MIN = -(2 ** 31)
NEG_BIG = -1e30
VMEM_LIMIT = 56 * 1024 * 1024
AT_PAD_W = 1664
KI_OFF = ATTN_WIDTH + 2 * KV_W + IDX_HEADS * IDX_DIM
PAGES_PER_STEP = 8

_NT = (((1,), (1,)), ((), ()))


def _cparams(*sem):
    return pltpu.CompilerParams(dimension_semantics=sem, vmem_limit_bytes=VMEM_LIMIT)


def _split2(x):
    hi = x.astype(BF16)
    lo = (x - hi.astype(F32)).astype(BF16)
    return hi, lo


def _split3(x):
    hi = x.astype(BF16)
    r1 = x - hi.astype(F32)
    mid = r1.astype(BF16)
    lo = (r1 - mid.astype(F32)).astype(BF16)
    return hi, mid, lo


def _dot3(a, b, dims=None):
    ah, al = _split2(a)
    bh, bl = _split2(b)
    if dims is None:
        d = lambda p, q: jnp.dot(p, q, preferred_element_type=F32)
    else:
        d = lambda p, q: lax.dot_general(p, q, dims, preferred_element_type=F32)
    return d(ah, bh) + d(al, bh) + d(ah, bl)


def _dot_sel(x, m):
    h, mid, lo = _split3(x)
    d = lambda p: jnp.dot(p, m, preferred_element_type=F32)
    return d(h) + d(mid) + d(lo)


def _f2key(x):
    x = jnp.where(x == 0.0, 0.0, x)
    b = lax.bitcast_convert_type(x, I32)
    return b ^ ((b >> 31) & 0x7FFFFFFF)


def _layer_norm(z, g, b):
    mu = jnp.mean(z, axis=-1, keepdims=True)
    zc = z - mu
    var = jnp.mean(zc * zc, axis=-1, keepdims=True)
    return zc * lax.rsqrt(var + LN_EPS) * g + b


def _mm_kernel(x_ref, w_ref, o_ref):
    o_ref[...] = jnp.dot(x_ref[...].astype(BF16), w_ref[...], preferred_element_type=F32)


def _matmul(x, w, tm):
    m, k = x.shape
    n = w.shape[1]
    tm = min(tm, m)
    return pl.pallas_call(
        _mm_kernel,
        grid=(m // tm,),
        in_specs=[pl.BlockSpec((tm, k), lambda i: (i, 0)),
                  pl.BlockSpec((k, n), lambda i: (0, 0))],
        out_specs=pl.BlockSpec((tm, n), lambda i: (i, 0)),
        out_shape=jax.ShapeDtypeStruct((m, n), F32),
        compiler_params=_cparams("parallel"),
        name="proj_matmul",
    )(x, w)


def _rwkv_prep_kernel(t_real, tt, shift, p_ref, prev_ref, mu_ref, w0_ref, a0_ref, lwh_ref, lwl_ref,
                      kk_ref, ka_ref, rk_ref, bd_ref,
                      r_o, w_o, kt_o, kko_o, b_o, v_o, bonus_o, g_o, carry_ref):
    j = pl.program_id(1)
    pf = p_ref[0]
    if shift:
        @pl.when(j == 0)
        def _():
            carry_ref[...] = prev_ref[0]
        row = lax.broadcasted_iota(I32, pf.shape, 0)
        prev = jnp.where(row == 0, carry_ref[...], pltpu.roll(pf, 1, 0))
        carry_ref[...] = pf[tt - 1:tt, :]
    else:
        prev = prev_ref[0]
    xs = pf + mu_ref[...] * (prev - pf)
    r = xs[:, 0:RWKV_WIDTH]
    k = xs[:, RWKV_WIDTH:2 * RWKV_WIDTH]
    v = xs[:, 2 * RWKV_WIDTH:3 * RWKV_WIDTH]
    z = xs[:, 3 * RWKV_WIDTH:SHIFT_W]
    lane = lax.broadcasted_iota(I32, z.shape, 1)
    zt = jnp.where(lane < W_LORA, jnp.tanh(z),
                   jnp.where(lane < W_LORA + A_LORA, z, jax.nn.sigmoid(z)))
    zh, zl = _split2(zt)
    d = lambda p, q: jnp.dot(p, q, preferred_element_type=F32)
    lo = d(zh, lwh_ref[...]) + d(zl, lwh_ref[...]) + d(zh, lwl_ref[...])
    w_raw = w0_ref[...] + lo[:, 0:RWKV_WIDTH]
    a = jax.nn.sigmoid(a0_ref[...] + lo[:, RWKV_WIDTH:2 * RWKV_WIDTH])
    g = lo[:, 2 * RWKV_WIDTH:3 * RWKV_WIDTH]
    decay = jnp.exp(-math.exp(-0.5) * jax.nn.sigmoid(w_raw))
    bd = bd_ref[...]
    kk = k * kk_ref[...]
    kk = kk / jnp.maximum(jnp.sqrt(_dot_sel(kk * kk, bd)), 1e-12)
    kt = k * (1.0 + (a - 1.0) * ka_ref[...])
    bonus = _dot_sel(r * kt * rk_ref[...], bd) * v
    pos = j * tt + lax.broadcasted_iota(I32, r.shape, 0)
    valid = pos < t_real
    r_o[0] = r
    w_o[0] = jnp.where(valid, decay, 1.0)
    kt_o[0] = jnp.where(valid, kt, 0.0)
    kko_o[0] = jnp.where(valid, kk, 0.0)
    b_o[0] = jnp.where(valid, kk * a, 0.0)
    v_o[0] = v
    bonus_o[0] = bonus
    g_o[0] = g


def _rwkv_prep(p_rw, prev, t_real, shift, wts, tt):
    bsz, tp, _ = p_rw.shape
    tt = min(tt, tp)
    row = lambda n: pl.BlockSpec((1, n), lambda b, j: (0, 0))
    full = lambda a: pl.BlockSpec(a.shape, lambda b, j: (0, 0))
    tok = lambda n: pl.BlockSpec((1, tt, n), lambda b, j: (b, j, 0))
    prev_spec = pl.BlockSpec((1, 1, SHIFT_W), lambda b, j: (b, 0, 0)) if shift else tok(SHIFT_W)
    outs = pl.pallas_call(
        functools.partial(_rwkv_prep_kernel, t_real, tt, shift),
        grid=(bsz, tp // tt),
        in_specs=[tok(SHIFT_W), prev_spec, row(SHIFT_W), row(RWKV_WIDTH), row(RWKV_WIDTH),
                  full(wts["lora_hi"]), full(wts["lora_lo"]),
                  row(RWKV_WIDTH), row(RWKV_WIDTH), row(RWKV_WIDTH), full(wts["bd"])],
        out_specs=[tok(RWKV_WIDTH)] * 8,
        out_shape=[jax.ShapeDtypeStruct((bsz, tp, RWKV_WIDTH), F32)] * 8,
        scratch_shapes=[pltpu.VMEM((1, SHIFT_W), F32)],
        compiler_params=_cparams("parallel", "arbitrary"),
        name="rwkv_prep",
    )(p_rw, prev, wts["mu"], wts["w0"], wts["a0"], wts["lora_hi"], wts["lora_lo"],
      wts["k_k"], wts["k_a"], wts["r_k"], wts["bd"])
    return outs


def _wkv_kernel(bb, tc, r_ref, w_ref, kt_ref, kk_ref, b_ref, v_ref, s0_ref, o_ref, sf_ref, s_ref):
    c = pl.program_id(1)

    @pl.when(c == 0)
    def _():
        s_ref[...] = s0_ref[...]

    lane = lax.broadcasted_iota(I32, (HEAD_DIM, LANE), 1)
    row = lax.broadcasted_iota(I32, (HEAD_DIM, LANE), 0)
    lo = lane < HEAD_DIM
    e0 = lane == row
    e1 = lane == row + HEAD_DIM
    e01 = e0 | e1
    r128 = lax.broadcasted_iota(I32, (LANE, LANE), 0)
    l128 = lax.broadcasted_iota(I32, (LANE, LANE), 1)
    half_ones = ((r128 >> 6) == (l128 >> 6)).astype(BF16)
    npair = RWKV_HEADS // 2

    def half_sums_mxu(parts, n_split):
        res = jnp.dot(jnp.concatenate(parts, axis=0), half_ones, preferred_element_type=F32)
        out = []
        for i in range(len(parts) // n_split):
            acc = res[i * n_split * HEAD_DIM:(i * n_split + 1) * HEAD_DIM]
            for p in range(1, n_split):
                acc = acc + res[(i * n_split + p) * HEAD_DIM:(i * n_split + p + 1) * HEAD_DIM]
            out.append(acc)
        return out

    def group(gi, carry):
        t0 = pl.multiple_of(gi * SUBLANE, SUBLANE)
        rows = pl.ds(t0, SUBLANE)
        for b in range(bb):
            blk = lambda ref: [ref[b, rows, j * LANE:(j + 1) * LANE] for j in range(npair)]
            kk8, w8, b8, kt8, v8, r8 = blk(kk_ref), blk(w_ref), blk(b_ref), blk(kt_ref), blk(v_ref), blk(r_ref)
            parts = []
            for j in range(npair):
                vh = v8[j].astype(BF16).astype(F32)
                r1 = v8[j] - vh
                vm = r1.astype(BF16).astype(F32)
                pieces = (vh, vm, r1 - vm)
                for u in range(SUBLANE):
                    parts += [jnp.where(e01, pc[u:u + 1], 0.0).astype(BF16) for pc in pieces]
            vcols = half_sums_mxu(parts, 3)
            st = [s_ref[b, j] for j in range(npair)]
            qparts = [[] for _ in range(npair)]
            for u in range(SUBLANE):
                prods = [st[j] * kk8[j][u:u + 1] for j in range(npair)]
                sums = [(jnp.sum(jnp.where(lo, p, 0.0), axis=-1, keepdims=True),
                         jnp.sum(jnp.where(lo, 0.0, p), axis=-1, keepdims=True)) for p in prods]
                for j in range(npair):
                    skk = jnp.where(lo, sums[j][0], sums[j][1])
                    s = st[j] * w8[j][u:u + 1] - skk * b8[j][u:u + 1] + vcols[j * SUBLANE + u] * kt8[j][u:u + 1]
                    st[j] = s
                    q = s * r8[j][u:u + 1]
                    qh = q.astype(BF16)
                    qparts[j] += [qh, (q - qh.astype(F32)).astype(BF16)]
            for j in range(npair):
                s_ref[b, j] = st[j]
                ocols = half_sums_mxu(qparts[j], 2)
                orows = [jnp.sum(jnp.where(e01, oc, 0.0), axis=0, keepdims=True) for oc in ocols]
                o_ref[b, rows, j * LANE:(j + 1) * LANE] = jnp.concatenate(orows, axis=0)
        return carry

    lax.fori_loop(0, tc // SUBLANE, group, 0)

    @pl.when(c == pl.num_programs(1) - 1)
    def _():
        sf_ref[...] = s_ref[...]


def _pair_state(s):
    b = s.shape[0]
    return (s.reshape(b, RWKV_HEADS // 2, 2, HEAD_DIM, HEAD_DIM)
            .transpose(0, 1, 3, 2, 4).reshape(b, RWKV_HEADS // 2, HEAD_DIM, LANE))


def _unpair_state(s):
    b = s.shape[0]
    return (s.reshape(b, RWKV_HEADS // 2, HEAD_DIM, 2, HEAD_DIM)
            .transpose(0, 1, 3, 2, 4).reshape(b, RWKV_HEADS, HEAD_DIM, HEAD_DIM))


def _wkv(r, w, kt, kk, bv, v, s0, bb, tc):
    bsz, tp, _ = r.shape
    tc = min(tc, tp)
    tok = pl.BlockSpec((bb, tc, RWKV_WIDTH), lambda i, c: (i, c, 0))
    st = pl.BlockSpec((bb, RWKV_HEADS // 2, HEAD_DIM, LANE), lambda i, c: (i, 0, 0, 0))
    o, sf = pl.pallas_call(
        functools.partial(_wkv_kernel, bb, tc),
        grid=(bsz // bb, tp // tc),
        in_specs=[tok] * 6 + [st],
        out_specs=[tok, st],
        out_shape=[jax.ShapeDtypeStruct((bsz, tp, RWKV_WIDTH), F32),
                   jax.ShapeDtypeStruct((bsz, RWKV_HEADS // 2, HEAD_DIM, LANE), F32)],
        scratch_shapes=[pltpu.VMEM((bb, RWKV_HEADS // 2, HEAD_DIM, LANE), F32)],
        compiler_params=_cparams("parallel", "arbitrary"),
        name="wkv_scan",
    )(r, w, kt, kk, bv, v, _pair_state(s0))
    return o, _unpair_state(sf)


def _rope(x, c, sa, sb):
    w = x.shape[1]
    return x * c + pltpu.roll(x, w - ROT_HALF, 1) * sa + pltpu.roll(x, ROT_HALF, 1) * sb


def _attn_prep_kernel(p_ref, c_ref, sa_ref, sb_ref, g_ref, b_ref, q_o, k_o, qi_o, ki_o):
    c1, sa1, sb1 = c_ref[...], sa_ref[...], sb_ref[...]
    rep = lambda t, n: jnp.concatenate([t] * n, axis=1)
    nq = ATTN_WIDTH // LANE
    nk = KV_W // LANE
    q_o[...] = _rope(p_ref[:, 0:ATTN_WIDTH], rep(c1, nq), rep(sa1, nq), rep(sb1, nq))
    k_o[...] = _rope(p_ref[:, ATTN_WIDTH:ATTN_WIDTH + KV_W], rep(c1, nk), rep(sa1, nk), rep(sb1, nk))
    qi0 = ATTN_WIDTH + 2 * KV_W
    qi_o[...] = _rope(p_ref[:, qi0:qi0 + IDX_HEADS * IDX_DIM], rep(c1, nq), rep(sa1, nq), rep(sb1, nq))
    x = p_ref[:, KI_OFF:KI_OFF + LANE]
    lane = lax.broadcasted_iota(I32, x.shape, 1)
    isk = lane < IDX_DIM
    mu = jnp.sum(jnp.where(isk, x, 0.0), axis=-1, keepdims=True) * (1.0 / IDX_DIM)
    xc = jnp.where(isk, x - mu, 0.0)
    var = jnp.sum(xc * xc, axis=-1, keepdims=True) * (1.0 / IDX_DIM)
    y = xc * lax.rsqrt(var + LN_EPS) * g_ref[...] + b_ref[...]
    y = _rope(y, jnp.where(isk, c1, 1.0), jnp.where(isk, sa1, 0.0), jnp.where(isk, sb1, 0.0))
    ki_o[...] = jnp.where(isk, y, x)


def _attn_prep(p_at, tabs, idx_g, idx_b, tm):
    n = p_at.shape[0]
    tm = min(tm, n)
    tpb = tabs[0].shape[0] // tm
    tok = lambda w: pl.BlockSpec((tm, w), lambda i: (i, 0))
    tab = pl.BlockSpec((tm, LANE), lambda i: (i % tpb, 0))
    row = pl.BlockSpec((1, LANE), lambda i: (0, 0))
    return pl.pallas_call(
        _attn_prep_kernel,
        grid=(n // tm,),
        in_specs=[tok(AT_PAD_W), tab, tab, tab, row, row],
        out_specs=[tok(ATTN_WIDTH), tok(KV_W), tok(IDX_HEADS * IDX_DIM), tok(LANE)],
        out_shape=[jax.ShapeDtypeStruct((n, ATTN_WIDTH), F32), jax.ShapeDtypeStruct((n, KV_W), F32),
                   jax.ShapeDtypeStruct((n, IDX_HEADS * IDX_DIM), F32), jax.ShapeDtypeStruct((n, LANE), F32)],
        compiler_params=_cparams("parallel"),
        name="attn_prep",
    )(p_at, tabs[0], tabs[1], tabs[2], idx_g, idx_b)


def _rope_tables(pos):
    inv = ROPE_THETA ** (-jnp.arange(ROT_HALF, dtype=F32) * 2.0 / ROT)
    ang = pos.astype(F32)[:, None] * inv[None, :]
    cos, sin = jnp.cos(ang), jnp.sin(ang)
    n = pos.shape[0]
    rest = HEAD_DIM - ROT
    c = jnp.concatenate([cos, cos, jnp.ones((n, rest), F32)], axis=1)
    sa = jnp.concatenate([-sin, jnp.zeros((n, rest + ROT_HALF), F32)], axis=1)
    sb = jnp.concatenate([jnp.zeros((n, ROT_HALF), F32), sin, jnp.zeros((n, rest), F32)], axis=1)
    two = lambda t: jnp.concatenate([t, t], axis=1)
    return two(c), two(sa), two(sb)


def _select_bounds(key_ref, n_tiles, rows, kt, k_sel, idx_bits):
    def count(pred):
        def body(i, acc):
            off = pl.multiple_of(i * kt, kt)
            idx = off + lax.broadcasted_iota(I32, (rows, kt), 1)
            hit = jnp.where(pred(key_ref[:, pl.ds(off, kt)], idx), 1.0, 0.0)
            for c in range(kt // LANE):
                acc = acc + hit[:, c * LANE:(c + 1) * LANE]
            return acc
        acc = lax.fori_loop(0, n_tiles, body, jnp.zeros((rows, LANE), F32))
        return jnp.sum(acc, axis=-1, keepdims=True)

    def thr_bit(i, res):
        cand = res + jnp.left_shift(jnp.int32(1), 31 - i)
        c = count(lambda key, idx: key >= cand)
        return jnp.where(c >= k_sel, cand, res)

    thr = lax.fori_loop(0, 32, thr_bit, jnp.full((rows, 1), INT_MIN, I32))
    n_ge = count(lambda key, idx: key >= thr)

    def tie_search():
        need = k_sel - count(lambda key, idx: key > thr)

        def idx_bit(i, res):
            cand = res | jnp.left_shift(jnp.int32(1), idx_bits - 1 - i)
            c = count(lambda key, idx: (key == thr) & (idx < cand))
            return jnp.where(c < need, cand, res)

        return lax.fori_loop(0, idx_bits, idx_bit, jnp.zeros((rows, 1), I32))

    jmax = lax.cond(jnp.max(n_ge) > k_sel, tie_search,
                    lambda: jnp.full((rows, 1), 2 ** idx_bits - 1, I32))
    return thr, jmax


def _dsa_prompt_kernel(kt, n_sel, idx_bits, q_ref, qi_ref, wi_ref, k_ref, v_ref, ki_ref, o_ref,
                       key_ref, m_ref, l_ref, acc_ref):
    i = pl.program_id(1)
    n_kt = (i * Q_BLOCK + Q_BLOCK + kt - 1) // kt
    qi = qi_ref[0]
    qis = jnp.concatenate([qi[:, h * IDX_DIM:(h + 1) * IDX_DIM] for h in range(IDX_HEADS)], axis=0)
    wi = wi_ref[0] * IDX_HEADS ** -0.5
    qpos = i * Q_BLOCK + lax.broadcasted_iota(I32, (Q_BLOCK, kt), 0)
    lane = lax.broadcasted_iota(I32, (Q_BLOCK, kt), 1)

    def scores(t, carry):
        off = pl.multiple_of(t * kt, kt)
        s = lax.dot_general(qis, ki_ref[0, pl.ds(off, kt), :], _NT, preferred_element_type=F32)
        acc = jnp.zeros((Q_BLOCK, kt), F32)
        for h in range(IDX_HEADS):
            acc = acc + wi[:, h:h + 1] * jnp.maximum(s[h * Q_BLOCK:(h + 1) * Q_BLOCK] * IDX_DIM ** -0.5, 0.0)
        key_ref[:, pl.ds(off, kt)] = jnp.where(off + lane <= qpos, _f2key(acc), INT_MIN)
        return carry

    lax.fori_loop(0, n_kt, scores, 0)
    thr, jmax = _select_bounds(key_ref, n_kt, Q_BLOCK, kt, n_sel, idx_bits)

    q = q_ref[0].astype(F32)
    grp = lax.broadcasted_iota(I32, (Q_BLOCK, KV_W), 1) >> 6
    rep = ATTN_HEADS // KV_HEADS

    def expand(h):
        qh = q[:, h * HEAD_DIM:(h + 1) * HEAD_DIM]
        return jnp.where(grp == h // rep, jnp.concatenate([qh] * KV_HEADS, axis=1), 0.0).astype(BF16)

    qe = [expand(h) for h in range(ATTN_HEADS)]
    m_ref[...] = jnp.full(m_ref.shape, NEG_BIG, F32)
    l_ref[...] = jnp.zeros(l_ref.shape, F32)
    acc_ref[...] = jnp.zeros(acc_ref.shape, F32)

    def attend(t, carry):
        off = pl.multiple_of(t * kt, kt)
        key = key_ref[:, pl.ds(off, kt)]
        kidx = off + lane
        sel = (kidx <= qpos) & ((key > thr) | ((key == thr) & (kidx <= jmax)))
        kt_tile = k_ref[0, pl.ds(off, kt), :]
        vt_tile = v_ref[0, pl.ds(off, kt), :]
        qk = lambda h: lax.dot_general(qe[h], kt_tile, _NT, preferred_element_type=F32)

        def finish(h, p, alpha):
            acc_ref[h] = alpha * acc_ref[h] + jnp.dot(p, vt_tile, preferred_element_type=F32)

        nxt = qk(0)
        pending = None
        for h in range(ATTN_HEADS):
            lg = nxt
            if h + 1 < ATTN_HEADS:
                nxt = qk(h + 1)
            lg = jnp.where(sel, lg * HEAD_DIM ** -0.5, NEG_BIG)
            m = m_ref[h]
            mn = jnp.maximum(m, jnp.max(lg, axis=-1, keepdims=True))
            p = jnp.where(sel, jnp.exp(lg - mn), 0.0)
            alpha = jnp.exp(m - mn)
            l_ref[h] = alpha * l_ref[h] + jnp.sum(p, axis=-1, keepdims=True)
            m_ref[h] = mn
            if pending is not None:
                finish(*pending)
            pending = (h, p.astype(BF16), alpha)
        finish(*pending)
        return carry

    lax.fori_loop(0, n_kt, attend, 0)
    pieces = []
    for h in range(ATTN_HEADS):
        g = h // rep
        pieces.append(acc_ref[h][:, g * HEAD_DIM:(g + 1) * HEAD_DIM] / l_ref[h])
    o_ref[0] = jnp.concatenate(pieces, axis=1)


def _dsa_prompt(q, qi, wi, k, v, ki, t_real):
    bsz, tp, _ = q.shape
    kt = 640 if tp % 640 == 0 else LANE
    n_sel = min(TOPK_MAX, t_real // 4)
    idx_bits = max(1, (tp - 1).bit_length())
    blk = lambda w: pl.BlockSpec((1, Q_BLOCK, w), lambda b, i: (b, i, 0))
    seq = lambda w: pl.BlockSpec((1, tp, w), lambda b, i: (b, 0, 0))
    return pl.pallas_call(
        functools.partial(_dsa_prompt_kernel, kt, n_sel, idx_bits),
        grid=(bsz, tp // Q_BLOCK),
        in_specs=[blk(ATTN_WIDTH), blk(IDX_HEADS * IDX_DIM), blk(IDX_HEADS),
                  seq(KV_W), seq(KV_W), seq(IDX_DIM)],
        out_specs=blk(ATTN_WIDTH),
        out_shape=jax.ShapeDtypeStruct((bsz, tp, ATTN_WIDTH), F32),
        scratch_shapes=[pltpu.VMEM((Q_BLOCK, tp), I32),
                        pltpu.VMEM((ATTN_HEADS, Q_BLOCK, 1), F32),
                        pltpu.VMEM((ATTN_HEADS, Q_BLOCK, 1), F32),
                        pltpu.VMEM((ATTN_HEADS, Q_BLOCK, KV_W), F32)],
        compiler_params=_cparams("parallel", "arbitrary"),
        name="dsa_prompt",
    )(q, qi, wi, k, v, ki)


def _dsa_s_scores_kernel(pps, pt_ref, qi_ref, wi_ref, *refs):
    ci_refs, o_ref = refs[:pps], refs[pps]
    qi = qi_ref[0]
    w = wi_ref[0] * IDX_HEADS ** -0.5
    for u in range(pps):
        s = _dot3(qi, ci_refs[u][0, 0], _NT)
        sc = jnp.sum(w * jnp.maximum(s * IDX_DIM ** -0.5, 0.0), axis=0, keepdims=True)
        o_ref[0, :, u * PAGE_SIZE:(u + 1) * PAGE_SIZE] = sc


def _dsa_s_scores(page_table, qi3, wi3, cache_idx, layer):
    bd, npages = page_table.shape
    pps = PAGES_PER_STEP
    page = lambda u: pl.BlockSpec((1, 1, PAGE_SIZE, IDX_DIM),
                                  lambda b, p, pt: (layer, pt[b * npages + p * pps + u], 0, 0))
    gs = pltpu.PrefetchScalarGridSpec(
        num_scalar_prefetch=1,
        grid=(bd, npages // pps),
        in_specs=[pl.BlockSpec((1, IDX_HEADS, IDX_DIM), lambda b, p, pt: (b, 0, 0)),
                  pl.BlockSpec((1, IDX_HEADS, 1), lambda b, p, pt: (b, 0, 0))] + [page(u) for u in range(pps)],
        out_specs=pl.BlockSpec((1, 1, pps * PAGE_SIZE), lambda b, p, pt: (b, 0, p)),
    )
    return pl.pallas_call(
        functools.partial(_dsa_s_scores_kernel, pps),
        grid_spec=gs,
        out_shape=jax.ShapeDtypeStruct((bd, 1, npages * PAGE_SIZE), F32),
        compiler_params=_cparams("parallel", "arbitrary"),
        name="dsa_decode_scores",
    )(page_table.reshape(-1), qi3, wi3, *([cache_idx] * pps))


def _dsa_s_bounds_kernel(past, n_sel, idx_bits, sc_ref, qi_ref, kiw_ref, hm_ref, thr_o, j_o, kn_o, key_ref):
    rows = sc_ref.shape[0]
    kiw = kiw_ref[...]
    lane = lax.broadcasted_iota(I32, kiw.shape, 1)
    rolled = pltpu.roll(kiw, IDX_DIM, 1)
    ki2 = jnp.where(lane < IDX_DIM, kiw, rolled)
    w8 = jnp.where(lane < IDX_HEADS, rolled, 0.0) * IDX_HEADS ** -0.5
    prod = qi_ref[...] * jnp.concatenate([ki2] * (IDX_HEADS // 2), axis=1)
    s = _dot3(prod, hm_ref[...])
    new = jnp.sum(w8 * jnp.maximum(s * IDX_DIM ** -0.5, 0.0), axis=-1, keepdims=True)
    knew = _f2key(new)
    key_ref[:, 0:past] = _f2key(sc_ref[...])
    key_ref[:, past:past + LANE] = jnp.where(lane == 0, knew, INT_MIN)
    thr, jmax = _select_bounds(key_ref, (past + LANE) // LANE, rows, LANE, n_sel, idx_bits)
    thr_o[...] = thr
    j_o[...] = jmax
    kn_o[...] = knew


def _dsa_s_bounds(scores, qi, kiw, hm, past):
    rows = scores.shape[0]
    n_sel = min(TOPK_MAX, (past + 1) // 4)
    idx_bits = (past + LANE - 1).bit_length()
    out = jax.ShapeDtypeStruct((rows, 1), I32)
    return pl.pallas_call(
        functools.partial(_dsa_s_bounds_kernel, past, n_sel, idx_bits),
        out_shape=[out, out, out],
        scratch_shapes=[pltpu.VMEM((rows, past + LANE), I32)],
        compiler_params=pltpu.CompilerParams(vmem_limit_bytes=VMEM_LIMIT),
        name="dsa_decode_bounds",
    )(scores, qi, kiw, hm)


def _dsa_s_attend_kernel(pps, past, pt_ref, thr_ref, j_ref, kn_ref, qe_ref, sc_ref, kn_row_ref, vn_row_ref,
                         fold_ref, *refs):
    ck, cv, o_ref = refs[:pps], refs[pps:2 * pps], refs[2 * pps]
    m_ref, l_ref, acc_ref = refs[2 * pps + 1:]
    b = pl.program_id(0)
    p = pl.program_id(1)

    @pl.when(p == 0)
    def _():
        m_ref[...] = jnp.full(m_ref.shape, NEG_BIG, F32)
        l_ref[...] = jnp.zeros(l_ref.shape, F32)
        acc_ref[...] = jnp.zeros(acc_ref.shape, F32)

    thr, jmax = thr_ref[b], j_ref[b]
    qe = qe_ref[0]
    qeb = qe.astype(BF16)
    kidx = p * pps * PAGE_SIZE + lax.broadcasted_iota(I32, (1, pps * PAGE_SIZE), 1)
    kb = jnp.concatenate([ck[u][0, 0].astype(BF16) for u in range(pps)], axis=0)
    vb = jnp.concatenate([cv[u][0, 0].astype(BF16) for u in range(pps)], axis=0)
    lg = lax.dot_general(qeb, kb, _NT, preferred_element_type=F32) * HEAD_DIM ** -0.5
    key = _f2key(sc_ref[0])
    sel = (key > thr) | ((key == thr) & (kidx <= jmax))
    lg = jnp.where(sel, lg, NEG_BIG)
    m = m_ref[...]
    mn = jnp.maximum(m, jnp.max(lg, axis=-1, keepdims=True))
    pr = jnp.where(sel, jnp.exp(lg - mn), 0.0)
    alpha = jnp.exp(m - mn)
    l_ref[...] = alpha * l_ref[...] + jnp.sum(pr, axis=-1, keepdims=True)
    acc_ref[...] = alpha * acc_ref[...] + jnp.dot(pr.astype(BF16), vb, preferred_element_type=F32)
    m_ref[...] = mn

    @pl.when(p == pl.num_programs(1) - 1)
    def _():
        knew = kn_ref[b]
        sel_new = (knew > thr) | ((knew == thr) & (past <= jmax))
        lg = jnp.sum(qe * kn_row_ref[0], axis=-1, keepdims=True) * HEAD_DIM ** -0.5
        m = m_ref[...]
        mn = jnp.where(sel_new, jnp.maximum(m, lg), m)
        pr = jnp.where(sel_new, jnp.exp(lg - mn), 0.0)
        alpha = jnp.exp(m - mn)
        l = alpha * l_ref[...] + pr
        acc = alpha * acc_ref[...] + pr * vn_row_ref[0]
        rowi = lax.broadcasted_iota(I32, acc.shape, 0)
        lanei = lax.broadcasted_iota(I32, acc.shape, 1)
        own = jnp.where((lanei >> 6) == (rowi >> 1), acc / l, 0.0)
        o_ref[0] = _dot_sel(own, fold_ref[...])


def _dsa_s_attend(page_table, thr, jmax, knew, qe, scores, k_new, v_new, fold, cache_k, cache_v, layer):
    bd, npages = page_table.shape
    past = npages * PAGE_SIZE
    pps = PAGES_PER_STEP
    page = lambda u: pl.BlockSpec((1, 1, PAGE_SIZE, KV_W),
                                  lambda b, p, pt, t, j, kn: (layer, pt[b * npages + p * pps + u], 0, 0))
    per_b = lambda s: pl.BlockSpec((1,) + s, lambda b, p, pt, t, j, kn: (b, 0, 0))
    gs = pltpu.PrefetchScalarGridSpec(
        num_scalar_prefetch=4,
        grid=(bd, npages // pps),
        in_specs=[per_b((ATTN_HEADS, KV_W)),
                  pl.BlockSpec((1, 1, pps * PAGE_SIZE), lambda b, p, pt, t, j, kn: (b, 0, p)),
                  per_b((1, KV_W)), per_b((1, KV_W)),
                  pl.BlockSpec(fold.shape, lambda b, p, pt, t, j, kn: (0, 0))]
                 + [page(u) for u in range(pps)] * 2,
        out_specs=per_b((ATTN_HEADS, HEAD_DIM)),
        scratch_shapes=[pltpu.VMEM((ATTN_HEADS, 1), F32), pltpu.VMEM((ATTN_HEADS, 1), F32),
                        pltpu.VMEM((ATTN_HEADS, KV_W), F32)],
    )
    return pl.pallas_call(
        functools.partial(_dsa_s_attend_kernel, pps, past),
        grid_spec=gs,
        out_shape=jax.ShapeDtypeStruct((bd, ATTN_HEADS, HEAD_DIM), F32),
        compiler_params=_cparams("parallel", "arbitrary"),
        name="dsa_decode_attend",
    )(page_table.reshape(-1), thr.reshape(-1), jmax.reshape(-1), knew.reshape(-1),
      qe, scores, k_new, v_new, fold, *([cache_k] * pps), *([cache_v] * pps))


def _merge_ln_kernel(o_ref, bonus_ref, g_ref, at_ref, x_ref, wt_ref, wb_ref, xg_ref, xb_ref,
                     lg_ref, lb_ref, bd_ref, out_ref):
    bd = bd_ref[...]
    o = o_ref[...]
    mean = _dot_sel(o, bd) * (1.0 / HEAD_DIM)
    oc = o - mean
    var = _dot_sel(oc * oc, bd) * (1.0 / HEAD_DIM)
    rw = (oc * lax.rsqrt(var + GN_EPS) * xg_ref[...] + xb_ref[...] + bonus_ref[...]) * g_ref[...]
    f = (jnp.dot(rw.astype(BF16), wt_ref[...], preferred_element_type=F32)
         + jnp.dot(at_ref[...].astype(BF16), wb_ref[...], preferred_element_type=F32))
    out_ref[...] = _layer_norm(DEEPNORM_ALPHA * x_ref[...] + f, lg_ref[...], lb_ref[...])


def _merge_ln(o, bonus, g, at, x, wts, tm):
    n = x.shape[0]
    tm = min(tm, n)
    tok = lambda w: pl.BlockSpec((tm, w), lambda i: (i, 0))
    full = lambda a: pl.BlockSpec(a.shape, lambda i: (0, 0))
    ws = [wts["wo_top"], wts["wo_bot"], wts["lnx_g"], wts["lnx_b"], wts["ln1_g"], wts["ln1_b"], wts["bd"]]
    return pl.pallas_call(
        _merge_ln_kernel,
        grid=(n // tm,),
        in_specs=[tok(RWKV_WIDTH)] * 3 + [tok(ATTN_WIDTH), tok(D_MODEL)] + [full(a) for a in ws],
        out_specs=tok(D_MODEL),
        out_shape=jax.ShapeDtypeStruct((n, D_MODEL), F32),
        compiler_params=_cparams("parallel"),
        name="merge_ln",
    )(o, bonus, g, at, x, *ws)


def _take_top(src_ref, n_rows, val_ref, idx_ref):
    shape = src_ref.shape
    row = lax.broadcasted_iota(I32, shape, 1)

    def body(a, carry):
        sv = src_ref[...]
        m = jnp.max(sv, axis=1, keepdims=True)
        idx = jnp.min(jnp.where(sv == m, row, n_rows), axis=1, keepdims=True)
        val_ref[:, pl.ds(a, 1), :] = m
        idx_ref[:, pl.ds(a, 1), :] = idx
        src_ref[...] = jnp.where(row == idx, -jnp.inf, sv)
        return carry

    lax.fori_loop(0, PEER_TOPK, body, 0)


def _peer_route_kernel(x_ref, wq_ref, sk0_ref, sk1_ref, e_o, g_o, s_ref, t_ref, i_ref, c_ref, ts_ref, ic_ref):
    q = jnp.dot(x_ref[...].astype(BF16), wq_ref[...], preferred_element_type=F32)
    for h in range(PEER_HEADS):
        qh = q[:, h * PEER_DKEY:(h + 1) * PEER_DKEY]
        s_ref[h] = _dot3(sk0_ref[...], qh, _NT)
        s_ref[PEER_HEADS + h] = _dot3(sk1_ref[...], qh, _NT)
    _take_top(s_ref, N_KEYS, t_ref, i_ref)
    t1, t2 = t_ref[0:PEER_HEADS], t_ref[PEER_HEADS:2 * PEER_HEADS]
    c_ref[...] = jnp.concatenate([t1[:, a:a + 1, :] + t2 for a in range(PEER_TOPK)], axis=1)
    _take_top(c_ref, PEER_TOPK * PEER_TOPK, ts_ref, ic_ref)
    ic = ic_ref[...]
    i1, i2 = i_ref[0:PEER_HEADS], i_ref[PEER_HEADS:2 * PEER_HEADS]
    ia, ib = ic >> 4, ic & (PEER_TOPK - 1)
    e = jnp.zeros(ic.shape, I32)
    for a in range(PEER_TOPK):
        e = e + jnp.where(ia == a, i1[:, a:a + 1, :] * N_KEYS, 0) + jnp.where(ib == a, i2[:, a:a + 1, :], 0)
    ts = ts_ref[...]
    ex = jnp.exp(ts - jnp.max(ts, axis=1, keepdims=True))
    e_o[0] = e
    g_o[0] = ex / jnp.sum(ex, axis=1, keepdims=True)


def _peer_route(x, wq, sk0, sk1, tt):
    n = x.shape[0]
    nt = n // tt
    full = lambda a: pl.BlockSpec(a.shape, lambda i: (0, 0))
    out = pl.BlockSpec((1, PEER_HEADS, PEER_TOPK, tt), lambda i: (i, 0, 0, 0))
    return pl.pallas_call(
        _peer_route_kernel,
        grid=(nt,),
        in_specs=[pl.BlockSpec((tt, D_MODEL), lambda i: (i, 0)), full(wq), full(sk0), full(sk1)],
        out_specs=[out, out],
        out_shape=[jax.ShapeDtypeStruct((nt, PEER_HEADS, PEER_TOPK, tt), I32),
                   jax.ShapeDtypeStruct((nt, PEER_HEADS, PEER_TOPK, tt), F32)],
        scratch_shapes=[pltpu.VMEM((2 * PEER_HEADS, N_KEYS, tt), F32),
                        pltpu.VMEM((2 * PEER_HEADS, PEER_TOPK, tt), F32),
                        pltpu.VMEM((2 * PEER_HEADS, PEER_TOPK, tt), I32),
                        pltpu.VMEM((PEER_HEADS, PEER_TOPK * PEER_TOPK, tt), F32),
                        pltpu.VMEM((PEER_HEADS, PEER_TOPK, tt), F32),
                        pltpu.VMEM((PEER_HEADS, PEER_TOPK, tt), I32)],
        compiler_params=_cparams("parallel"),
        name="peer_route",
    )(x, wq, sk0, sk1)


def _peer_gather_kernel(tt, n_valid, idx_hbm, x_ref, g_ref, uv_hbm, lg_ref, lb_ref, o_ref,
                        idx_smem, buf0, buf1, buf2, sem, isem, y_ref):
    i = pl.program_id(0)
    n_idx = PEER_SLOTS * tt
    islot = i % 2
    bufs = (buf0, buf1, buf2)

    def idx_copy(tile, slot):
        return pltpu.make_async_copy(idx_hbm.at[tile], idx_smem.at[pl.ds(slot * n_idx, n_idx)], isem.at[slot])

    @pl.when(i == 0)
    def _():
        idx_copy(0, 0).start()

    if n_valid < tt:
        y_ref[...] = jnp.zeros(y_ref.shape, F32)
    idx_copy(i, islot).wait()

    @pl.when(i + 1 < pl.num_programs(0))
    def _():
        idx_copy(i + 1, 1 - islot).start()

    base = islot * n_idx

    def issue(t, k):
        for s in range(PEER_SLOTS):
            e = idx_smem[base + s * tt + t]
            pltpu.make_async_copy(uv_hbm.at[pl.ds(e, 1)], bufs[k].at[pl.ds(s, 1)], sem.at[k]).start()

    def wait(k):
        pltpu.make_async_copy(uv_hbm.at[pl.ds(0, PEER_SLOTS)], bufs[k], sem.at[k]).wait()

    lane = lax.broadcasted_iota(I32, (PEER_SLOTS, tt), 1)

    def compute(t, k):
        xrow = x_ref[pl.ds(t, 1), :]
        word = bufs[k][...]
        u = lax.bitcast_convert_type(word << 16, F32)
        v = lax.bitcast_convert_type(word & jnp.int32(-65536), F32)
        h = jnp.sum(u * xrow, axis=-1, keepdims=True)
        gate = jnp.sum(jnp.where(lane == t, g_ref[0], 0.0), axis=-1, keepdims=True)
        coef = gate * jax.nn.gelu(h)
        y_ref[pl.ds(t, 1), :] = jnp.sum(coef * v, axis=0, keepdims=True)

    depth = len(bufs)
    ahead = depth - 1
    n_main = n_valid - ahead
    assert n_main >= 0 and n_main % depth == 0
    for t in range(ahead):
        issue(t, t)

    def body(i3, carry):
        for k in range(depth):
            t = i3 * depth + k
            wait(k)
            issue(t + ahead, (k + ahead) % depth)
            compute(t, k)
        return carry

    lax.fori_loop(0, n_main // depth, body, 0)
    for t in range(n_main, n_valid):
        wait(t % depth)
        compute(t, t % depth)
    o_ref[...] = _layer_norm(DEEPNORM_ALPHA * x_ref[...] + y_ref[...], lg_ref[...], lb_ref[...])


def _peer_gather(idx, gates, x, uv, ln_g, ln_b, tt, n_valid):
    n = x.shape[0]
    nt = n // tt
    row = pl.BlockSpec((1, D_MODEL), lambda i: (0, 0))
    return pl.pallas_call(
        functools.partial(_peer_gather_kernel, tt, n_valid),
        grid=(nt,),
        in_specs=[pl.BlockSpec(memory_space=pl.ANY),
                  pl.BlockSpec((tt, D_MODEL), lambda i: (i, 0)),
                  pl.BlockSpec((1, PEER_SLOTS, tt), lambda i: (i, 0, 0)),
                  pl.BlockSpec(memory_space=pl.ANY), row, row],
        out_specs=pl.BlockSpec((tt, D_MODEL), lambda i: (i, 0)),
        out_shape=jax.ShapeDtypeStruct((n, D_MODEL), F32),
        scratch_shapes=[pltpu.SMEM((2 * PEER_SLOTS * tt,), I32)]
                       + [pltpu.VMEM((PEER_SLOTS, D_MODEL), I32)] * 3
                       + [pltpu.SemaphoreType.DMA((3,)),
                          pltpu.SemaphoreType.DMA((2,)),
                          pltpu.VMEM((tt, D_MODEL), F32)],
        compiler_params=_cparams("arbitrary"),
        name="peer_gather",
    )(idx, x, gates, uv, ln_g, ln_b)


def _peer(x, wts, tt, n_valid):
    e, gates = _peer_route(x, wts["wq"], wts["sk0"], wts["sk1"], tt)
    nt = x.shape[0] // tt
    return _peer_gather(e.reshape(nt, PEER_SLOTS * tt), gates.reshape(nt, PEER_SLOTS, tt), x,
                        wts["uv"], wts["ln2_g"], wts["ln2_b"], tt, n_valid)


def _pack_bf16_pair(lo, hi):
    bits = lambda a: lax.bitcast_convert_type(a.astype(BF16), jnp.uint16).astype(jnp.uint32)
    return lax.bitcast_convert_type(bits(lo) | (bits(hi) << 16), I32)


def _layer_weights(l, w_in, shift_mu, decay_w0, decay_up, iclr_a0, iclr_up, gate_up, k_k, k_a, r_k,
                   lnx_g, lnx_b, idx_ln_g, idx_ln_b, w_out, ln1_g, ln1_b, ln2_g, ln2_b,
                   peer_wq, peer_subkeys, peer_u, peer_v):
    row = lambda a: a.reshape(1, -1).astype(F32)
    lora = jnp.zeros((LORA_W, 3 * RWKV_WIDTH), F32)
    lora = lora.at[0:W_LORA, 0:RWKV_WIDTH].set(decay_up[l])
    lora = lora.at[W_LORA:W_LORA + A_LORA, RWKV_WIDTH:2 * RWKV_WIDTH].set(iclr_up[l])
    lora = lora.at[W_LORA + A_LORA:, 2 * RWKV_WIDTH:].set(gate_up[l])
    lora_hi = lora.astype(BF16)
    pad_lane = lambda a: jnp.pad(a.reshape(1, -1), ((0, 0), (0, LANE - a.size)))
    half = PEER_DKEY // 2
    seg = jnp.arange(RWKV_WIDTH) // HEAD_DIM
    return dict(
        w_rw=w_in[l][:, :SHIFT_W].astype(BF16),
        w_at=jnp.pad(w_in[l][:, SHIFT_W:], ((0, 0), (0, AT_PAD_W - ATTN_PROJ_W))).astype(BF16),
        mu=row(shift_mu[l]), w0=row(decay_w0[l]), a0=row(iclr_a0[l]),
        lora_hi=lora_hi, lora_lo=(lora - lora_hi.astype(F32)).astype(BF16),
        k_k=row(k_k[l]), k_a=row(k_a[l]), r_k=row(r_k[l]),
        bd=(seg[:, None] == seg[None, :]).astype(BF16),
        lnx_g=row(lnx_g[l]), lnx_b=row(lnx_b[l]),
        idx_g=pad_lane(idx_ln_g[l]), idx_b=pad_lane(idx_ln_b[l]),
        wo_top=w_out[l][:RWKV_WIDTH].astype(BF16), wo_bot=w_out[l][RWKV_WIDTH:].astype(BF16),
        ln1_g=row(ln1_g[l]), ln1_b=row(ln1_b[l]), ln2_g=row(ln2_g[l]), ln2_b=row(ln2_b[l]),
        wq=peer_wq[l].astype(BF16),
        sk0=jnp.pad(peer_subkeys[l, 0], ((0, 0), (0, half))),
        sk1=jnp.pad(peer_subkeys[l, 1], ((0, 0), (half, 0))),
        uv=_pack_bf16_pair(peer_u[l], peer_v[l]),
    )


def _tile(n, cap):
    if n <= cap:
        return n
    return max(d for d in range(8, cap + 1, 8) if n % d == 0)


def _mixer_front(x, wts, tabs):
    tm = _tile(x.shape[0], 512)
    p_rw = _matmul(x, wts["w_rw"], tm)
    p_at = _matmul(x, wts["w_at"], tm)
    q, k, qi, kiw = _attn_prep(p_at, tabs, wts["idx_g"], wts["idx_b"], _tile(tabs[0].shape[0], 640))
    v = p_at[:, ATTN_WIDTH + KV_W:ATTN_WIDTH + 2 * KV_W]
    return p_rw, q, k, v, qi, kiw


def kernel(x_prompt, x_sample, cache_k, cache_v, cache_idx_k, state_wkv, state_shift, page_table, meta_tokens, w_in, shift_mu, decay_w0, decay_up, iclr_a0, iclr_up, gate_up, k_k, k_a, r_k, lnx_g, lnx_b, idx_ln_g, idx_ln_b, w_out, ln1_g, ln1_b, ln2_g, ln2_b, peer_wq, peer_subkeys, peer_u, peer_v):
    bsz, seq, _ = x_prompt.shape
    t = seq + N_META
    tp = -(-t // LANE) * LANE
    n_p = bsz * tp
    depth = w_in.shape[0]
    bd_, ts_, _ = x_sample.shape
    assert ts_ == 1
    npages = page_table.shape[1]
    past = npages * PAGE_SIZE
    n_pool = cache_k.shape[1]
    peer_tt = LANE
    ns_pad = -(-bd_ // peer_tt) * peer_tt

    xp = jnp.concatenate([jnp.broadcast_to(meta_tokens[None], (bsz, N_META, D_MODEL)), x_prompt], axis=1)
    xp = jnp.pad(xp, ((0, 0), (0, tp - t), (0, 0))).reshape(n_p, D_MODEL)
    xs = x_sample.reshape(bd_, D_MODEL)
    tabs_p = _rope_tables(jnp.arange(tp, dtype=I32))
    tabs_s = _rope_tables(jnp.full((bd_,), past, I32))
    ck = cache_k.reshape(depth, n_pool, PAGE_SIZE, KV_W)
    cv = cache_v.reshape(depth, n_pool, PAGE_SIZE, KV_W)
    hsel = jnp.arange(IDX_HEADS * IDX_DIM)[:, None] // IDX_DIM == jnp.arange(LANE)[None, :]
    hm = hsel.astype(F32)
    fold = (jnp.arange(KV_W)[:, None] % HEAD_DIM == jnp.arange(HEAD_DIM)[None, :]).astype(BF16)
    own = (jnp.arange(KV_W)[None, :] // HEAD_DIM == jnp.arange(ATTN_HEADS)[:, None] // (ATTN_HEADS // KV_HEADS))

    k_p, v_p, ki_p, wkv_p, sh_p = [], [], [], [], []
    k_s, v_s, ki_s, wkv_s, sh_s = [], [], [], [], []
    for l in range(depth):
        wts = _layer_weights(l, w_in, shift_mu, decay_w0, decay_up, iclr_a0, iclr_up, gate_up, k_k, k_a, r_k,
                             lnx_g, lnx_b, idx_ln_g, idx_ln_b, w_out, ln1_g, ln1_b, ln2_g, ln2_b,
                             peer_wq, peer_subkeys, peer_u, peer_v)

        p_rw, q, k, v, qi, kiw = _mixer_front(xp, wts, tabs_p)
        p_rw3 = p_rw.reshape(bsz, tp, SHIFT_W)
        r_, w_, kt_, kk_, b_, vv_, bonus, g = _rwkv_prep(
            p_rw3, jnp.zeros((bsz, 1, SHIFT_W), F32), t, True, wts, LANE)
        o, s_fin = _wkv(r_, w_, kt_, kk_, b_, vv_,
                        jnp.zeros((bsz, RWKV_HEADS, HEAD_DIM, HEAD_DIM), F32), bsz, LANE)
        three = lambda a: a.reshape(bsz, tp, -1)
        at = _dsa_prompt(three(q).astype(BF16), three(qi).astype(BF16),
                         three(kiw)[:, :, IDX_DIM:IDX_DIM + IDX_HEADS],
                         three(k).astype(BF16), three(v).astype(BF16),
                         three(kiw)[:, :, :IDX_DIM].astype(BF16), t)
        flat = lambda a: a.reshape(n_p, -1)
        x1 = _merge_ln(flat(o), flat(bonus), flat(g), flat(at), xp, wts, _tile(n_p, 256))
        xp = _peer(x1, wts, peer_tt, peer_tt)
        k_p.append(three(k)[:, :t].reshape(bsz, t, KV_HEADS, HEAD_DIM))
        v_p.append(three(v)[:, :t].reshape(bsz, t, KV_HEADS, HEAD_DIM))
        ki_p.append(three(kiw)[:, :t, :IDX_DIM])
        wkv_p.append(s_fin)
        sh_p.append(p_rw3[:, t - 1])

        p_rw, q, k, v, qi, kiw = _mixer_front(xs, wts, tabs_s)
        r_, w_, kt_, kk_, b_, vv_, bonus, g = _rwkv_prep(
            p_rw[None], state_shift[l][None], bd_, False, wts, bd_)
        tc_s = 8
        padt = lambda a, c: jnp.pad(a[0][:, None, :], ((0, 0), (0, tc_s - 1), (0, 0)), constant_values=c)
        o, s_fin = _wkv(padt(r_, 0.0), padt(w_, 1.0), padt(kt_, 0.0), padt(kk_, 0.0), padt(b_, 0.0),
                        padt(vv_, 0.0), state_wkv[l].astype(F32), 4, tc_s)
        o = o[:, 0]
        scores = _dsa_s_scores(page_table, qi.reshape(bd_, IDX_HEADS, IDX_DIM),
                               kiw[:, IDX_DIM:IDX_DIM + IDX_HEADS].reshape(bd_, IDX_HEADS, 1),
                               cache_idx_k, l)
        thr, jmax, knew = _dsa_s_bounds(scores.reshape(bd_, past), qi, kiw, hm, past)
        qe = jnp.where(own[None], jnp.tile(q.reshape(bd_, ATTN_HEADS, HEAD_DIM), (1, 1, KV_HEADS)), 0.0)
        at = _dsa_s_attend(page_table, thr, jmax, knew, qe, scores, k[:, None, :], v[:, None, :], fold,
                           ck, cv, l).reshape(bd_, ATTN_WIDTH)
        x1 = _merge_ln(o, bonus[0], g[0], at, xs, wts, bd_)
        x1p = jnp.pad(x1, ((0, ns_pad - bd_), (0, 0)))
        xs = _peer(x1p, wts, peer_tt, bd_)[:bd_]
        k_s.append(k.reshape(bd_, 1, KV_HEADS, HEAD_DIM))
        v_s.append(v.reshape(bd_, 1, KV_HEADS, HEAD_DIM))
        ki_s.append(kiw[:, None, :IDX_DIM])
        wkv_s.append(s_fin)
        sh_s.append(p_rw)

    y_prompt = xp.reshape(bsz, tp, D_MODEL)[:, N_META:t]
    y_sample = xs.reshape(bd_, 1, D_MODEL)
    return (y_prompt, y_sample, jnp.stack(k_p), jnp.stack(v_p), jnp.stack(ki_p),
            jnp.stack(wkv_p).astype(state_wkv.dtype), jnp.stack(sh_p).astype(state_shift.dtype),
            jnp.stack(k_s), jnp.stack(v_s), jnp.stack(ki_s),
            jnp.stack(wkv_s).astype(state_wkv.dtype), jnp.stack(sh_s).astype(state_shift.dtype))
```

```python
import functools
import math

import jax
import jax.numpy as jnp
from jax import lax
from jax.experimental import pallas as pl
from jax.experimental.pallas import tpu as pltpu

F32 = jnp.float32
BF16 = jnp.bfloat16
I32 = jnp.int32

D_MODEL = 1024
N_META = 16
HEAD_DIM = 64
RWKV_WIDTH = D_MODEL // 2
RWKV_HEADS = RWKV_WIDTH // HEAD_DIM
ATTN_WIDTH = D_MODEL - RWKV_WIDTH
ATTN_HEADS = ATTN_WIDTH // HEAD_DIM
KV_HEADS = ATTN_HEADS // 2
KV_W = KV_HEADS * HEAD_DIM
W_LORA = 64
A_LORA = 64
G_LORA = 128
LORA_W = W_LORA + A_LORA + G_LORA
SHIFT_W = 3 * RWKV_WIDTH + LORA_W
IDX_HEADS = 8
IDX_DIM = 64
ATTN_PROJ_W = ATTN_WIDTH + 2 * KV_W + IDX_HEADS * IDX_DIM + IDX_DIM + IDX_HEADS
TOPK_MAX = 256
ROPE_THETA = 500000.0
ROT = HEAD_DIM // 4
ROT_HALF = ROT // 2
PEER_HEADS = 8
PEER_DKEY = 128
N_KEYS = 128
PEER_TOPK = 16
PEER_SLOTS = PEER_HEADS * PEER_TOPK
DEPTH = 2
DEEPNORM_ALPHA = (2.0 * DEPTH) ** 0.25
PAGE_SIZE = 128
LN_EPS = 1e-5
GN_EPS = 64e-5

LANE = 128
SUBLANE = 8
Q_BLOCK = 128
INT_MIN = -(2 ** 31)
NEG_BIG = -1e30
VMEM_LIMIT = 56 * 1024 * 1024
AT_PAD_W = 1664
KI_OFF = ATTN_WIDTH + 2 * KV_W + IDX_HEADS * IDX_DIM
PAGES_PER_STEP = 8

_NT = (((1,), (1,)), ((), ()))


def _cparams(*sem):
    return pltpu.CompilerParams(dimension_semantics=sem, vmem_limit_bytes=VMEM_LIMIT)


def _split2(x):
    hi = x.astype(BF16)
    lo = (x - hi.astype(F32)).astype(BF16)
    return hi, lo


def _split3(x):
    hi = x.astype(BF16)
    r1 = x - hi.astype(F32)
    mid = r1.astype(BF16)
    lo = (r1 - mid.astype(F32)).astype(BF16)
    return hi, mid, lo


def _dot3(a, b, dims=None):
    ah, al = _split2(a)
    bh, bl = _split2(b)
    if dims is None:
        d = lambda p, q: jnp.dot(p, q, preferred_element_type=F32)
    else:
        d = lambda p, q: lax.dot_general(p, q, dims, preferred_element_type=F32)
    return d(ah, bh) + d(al, bh) + d(ah, bl)


def _dot_sel(x, m):
    h, mid, lo = _split3(x)
    d = lambda p: jnp.dot(p, m, preferred_element_type=F32)
    return d(h) + d(mid) + d(lo)


def _f2key(x):
    x = jnp.where(x == 0.0, 0.0, x)
    b = lax.bitcast_convert_type(x, I32)
    return b ^ ((b >> 31) & 0x7FFFFFFF)


def _layer_norm(z, g, b):
    mu = jnp.mean(z, axis=-1, keepdims=True)
    zc = z - mu
    var = jnp.mean(zc * zc, axis=-1, keepdims=True)
    return zc * lax.rsqrt(var + LN_EPS) * g + b


def _mm_kernel(x_ref, w_ref, o_ref):
    o_ref[...] = jnp.dot(x_ref[...].astype(BF16), w_ref[...], preferred_element_type=F32)


def _matmul(x, w, tm):
    m, k = x.shape
    n = w.shape[1]
    tm = min(tm, m)
    return pl.pallas_call(
        _mm_kernel,
        grid=(m // tm,),
        in_specs=[pl.BlockSpec((tm, k), lambda i: (i, 0)),
                  pl.BlockSpec((k, n), lambda i: (0, 0))],
        out_specs=pl.BlockSpec((tm, n), lambda i: (i, 0)),
        out_shape=jax.ShapeDtypeStruct((m, n), F32),
        compiler_params=_cparams("parallel"),
        name="proj_matmul",
    )(x, w)


def _rwkv_prep_kernel(t_real, tt, shift, p_ref, prev_ref, mu_ref, w0_ref, a0_ref, lwh_ref, lwl_ref,
                      kk_ref, ka_ref, rk_ref, bd_ref,
                      r_o, w_o, kt_o, kko_o, b_o, v_o, bonus_o, g_o, carry_ref):
    j = pl.program_id(1)
    pf = p_ref[0]
    if shift:
        @pl.when(j == 0)
        def _():
            carry_ref[...] = prev_ref[0]
        row = lax.broadcasted_iota(I32, pf.shape, 0)
        prev = jnp.where(row == 0, carry_ref[...], pltpu.roll(pf, 1, 0))
        carry_ref[...] = pf[tt - 1:tt, :]
    else:
        prev = prev_ref[0]
    xs = pf + mu_ref[...] * (prev - pf)
    r = xs[:, 0:RWKV_WIDTH]
    k = xs[:, RWKV_WIDTH:2 * RWKV_WIDTH]
    v = xs[:, 2 * RWKV_WIDTH:3 * RWKV_WIDTH]
    z = xs[:, 3 * RWKV_WIDTH:SHIFT_W]
    lane = lax.broadcasted_iota(I32, z.shape, 1)
    zt = jnp.where(lane < W_LORA, jnp.tanh(z),
                   jnp.where(lane < W_LORA + A_LORA, z, jax.nn.sigmoid(z)))
    zh, zl = _split2(zt)
    d = lambda p, q: jnp.dot(p, q, preferred_element_type=F32)
    lo = d(zh, lwh_ref[...]) + d(zl, lwh_ref[...]) + d(zh, lwl_ref[...])
    w_raw = w0_ref[...] + lo[:, 0:RWKV_WIDTH]
    a = jax.nn.sigmoid(a0_ref[...] + lo[:, RWKV_WIDTH:2 * RWKV_WIDTH])
    g = lo[:, 2 * RWKV_WIDTH:3 * RWKV_WIDTH]
    decay = jnp.exp(-math.exp(-0.5) * jax.nn.sigmoid(w_raw))
    bd = bd_ref[...]
    kk = k * kk_ref[...]
    kk = kk / jnp.maximum(jnp.sqrt(_dot_sel(kk * kk, bd)), 1e-12)
    kt = k * (1.0 + (a - 1.0) * ka_ref[...])
    bonus = _dot_sel(r * kt * rk_ref[...], bd) * v
    pos = j * tt + lax.broadcasted_iota(I32, r.shape, 0)
    valid = pos < t_real
    r_o[0] = r
    w_o[0] = jnp.where(valid, decay, 1.0)
    kt_o[0] = jnp.where(valid, kt, 0.0)
    kko_o[0] = jnp.where(valid, kk, 0.0)
    b_o[0] = jnp.where(valid, kk * a, 0.0)
    v_o[0] = v
    bonus_o[0] = bonus
    g_o[0] = g


def _rwkv_prep(p_rw, prev, t_real, shift, wts, tt):
    bsz, tp, _ = p_rw.shape
    tt = min(tt, tp)
    row = lambda n: pl.BlockSpec((1, n), lambda b, j: (0, 0))
    full = lambda a: pl.BlockSpec(a.shape, lambda b, j: (0, 0))
    tok = lambda n: pl.BlockSpec((1, tt, n), lambda b, j: (b, j, 0))
    prev_spec = pl.BlockSpec((1, 1, SHIFT_W), lambda b, j: (b, 0, 0)) if shift else tok(SHIFT_W)
    outs = pl.pallas_call(
        functools.partial(_rwkv_prep_kernel, t_real, tt, shift),
        grid=(bsz, tp // tt),
        in_specs=[tok(SHIFT_W), prev_spec, row(SHIFT_W), row(RWKV_WIDTH), row(RWKV_WIDTH),
                  full(wts["lora_hi"]), full(wts["lora_lo"]),
                  row(RWKV_WIDTH), row(RWKV_WIDTH), row(RWKV_WIDTH), full(wts["bd"])],
        out_specs=[tok(RWKV_WIDTH)] * 8,
        out_shape=[jax.ShapeDtypeStruct((bsz, tp, RWKV_WIDTH), F32)] * 8,
        scratch_shapes=[pltpu.VMEM((1, SHIFT_W), F32)],
        compiler_params=_cparams("parallel", "arbitrary"),
        name="rwkv_prep",
    )(p_rw, prev, wts["mu"], wts["w0"], wts["a0"], wts["lora_hi"], wts["lora_lo"],
      wts["k_k"], wts["k_a"], wts["r_k"], wts["bd"])
    return outs


def _wkv_kernel(bb, tc, r_ref, w_ref, kt_ref, kk_ref, b_ref, v_ref, s0_ref, o_ref, sf_ref, s_ref):
    c = pl.program_id(1)

    @pl.when(c == 0)
    def _():
        s_ref[...] = s0_ref[...]

    lane = lax.broadcasted_iota(I32, (HEAD_DIM, LANE), 1)
    row = lax.broadcasted_iota(I32, (HEAD_DIM, LANE), 0)
    lo = lane < HEAD_DIM
    e0 = lane == row
    e1 = lane == row + HEAD_DIM
    e01 = e0 | e1
    r128 = lax.broadcasted_iota(I32, (LANE, LANE), 0)
    l128 = lax.broadcasted_iota(I32, (LANE, LANE), 1)
    half_ones = ((r128 >> 6) == (l128 >> 6)).astype(BF16)
    npair = RWKV_HEADS // 2

    def half_sums_mxu(parts, n_split):
        res = jnp.dot(jnp.concatenate(parts, axis=0), half_ones, preferred_element_type=F32)
        out = []
        for i in range(len(parts) // n_split):
            acc = res[i * n_split * HEAD_DIM:(i * n_split + 1) * HEAD_DIM]
            for p in range(1, n_split):
                acc = acc + res[(i * n_split + p) * HEAD_DIM:(i * n_split + p + 1) * HEAD_DIM]
            out.append(acc)
        return out

    def group(gi, carry):
        t0 = pl.multiple_of(gi * SUBLANE, SUBLANE)
        rows = pl.ds(t0, SUBLANE)
        blk = lambda ref, b: [ref[b, rows, j * LANE:(j + 1) * LANE] for j in range(npair)]

        def v_pieces(b):
            out = []
            for v8 in blk(v_ref, b):
                vh = v8.astype(BF16).astype(F32)
                r1 = v8 - vh
                vm = r1.astype(BF16).astype(F32)
                out.append((vh, vm, r1 - vm))
            return out

        def v_columns(pieces, u):
            parts = []
            for j in range(npair):
                parts += [jnp.where(e01, pc[u:u + 1], 0.0).astype(BF16) for pc in pieces[j]]
            return half_sums_mxu(parts, 3)

        def out_rows(qparts):
            return [jnp.sum(jnp.where(e01, oc, 0.0), axis=0, keepdims=True) for oc in half_sums_mxu(qparts, 2)]

        pieces = v_pieces(0)
        vcols = [v_columns(pieces, u) for u in range(SUBLANE)]
        pending = None
        for b in range(bb + 1):
            if b < bb:
                kk8, w8, b8, kt8, r8 = blk(kk_ref, b), blk(w_ref, b), blk(b_ref, b), blk(kt_ref, b), blk(r_ref, b)
                st = [s_ref[b, j] for j in range(npair)]
                nxt_pieces = v_pieces(b + 1) if b + 1 < bb else None
            nxt_vcols, qsteps, orows = [], [], []
            for u in range(SUBLANE):
                if b < bb:
                    prods = [st[j] * kk8[j][u:u + 1] for j in range(npair)]
                    sums = [(jnp.sum(jnp.where(lo, p, 0.0), axis=-1, keepdims=True),
                             jnp.sum(jnp.where(lo, 0.0, p), axis=-1, keepdims=True)) for p in prods]
                if nxt_pieces is not None:
                    nxt_vcols.append(v_columns(nxt_pieces, u))
                if pending is not None:
                    orows.append(out_rows(pending[1][u]))
                if b < bb:
                    qs = []
                    for j in range(npair):
                        skk = jnp.where(lo, sums[j][0], sums[j][1])
                        s = st[j] * w8[j][u:u + 1] - skk * b8[j][u:u + 1] + vcols[u][j] * kt8[j][u:u + 1]
                        st[j] = s
                        q = s * r8[j][u:u + 1]
                        qh = q.astype(BF16)
                        qs += [qh, (q - qh.astype(F32)).astype(BF16)]
                    qsteps.append(qs)
            if pending is not None:
                pb = pending[0]
                for j in range(npair):
                    o_ref[pb, rows, j * LANE:(j + 1) * LANE] = jnp.concatenate([orows[u][j] for u in range(SUBLANE)],
                                                                               axis=0)
            if b < bb:
                for j in range(npair):
                    s_ref[b, j] = st[j]
                pending = (b, qsteps)
                vcols = nxt_vcols
                nxt_pieces = None
        return carry

    lax.fori_loop(0, tc // SUBLANE, group, 0)

    @pl.when(c == pl.num_programs(1) - 1)
    def _():
        sf_ref[...] = s_ref[...]


def _pair_state(s):
    b = s.shape[0]
    return (s.reshape(b, RWKV_HEADS // 2, 2, HEAD_DIM, HEAD_DIM)
            .transpose(0, 1, 3, 2, 4).reshape(b, RWKV_HEADS // 2, HEAD_DIM, LANE))


def _unpair_state(s):
    b = s.shape[0]
    return (s.reshape(b, RWKV_HEADS // 2, HEAD_DIM, 2, HEAD_DIM)
            .transpose(0, 1, 3, 2, 4).reshape(b, RWKV_HEADS, HEAD_DIM, HEAD_DIM))


def _wkv(r, w, kt, kk, bv, v, s0, bb, tc):
    bsz, tp, _ = r.shape
    tc = min(tc, tp)
    tok = pl.BlockSpec((bb, tc, RWKV_WIDTH), lambda i, c: (i, c, 0))
    st = pl.BlockSpec((bb, RWKV_HEADS // 2, HEAD_DIM, LANE), lambda i, c: (i, 0, 0, 0))
    o, sf = pl.pallas_call(
        functools.partial(_wkv_kernel, bb, tc),
        grid=(bsz // bb, tp // tc),
        in_specs=[tok] * 6 + [st],
        out_specs=[tok, st],
        out_shape=[jax.ShapeDtypeStruct((bsz, tp, RWKV_WIDTH), F32),
                   jax.ShapeDtypeStruct((bsz, RWKV_HEADS // 2, HEAD_DIM, LANE), F32)],
        scratch_shapes=[pltpu.VMEM((bb, RWKV_HEADS // 2, HEAD_DIM, LANE), F32)],
        compiler_params=_cparams("parallel", "arbitrary"),
        name="wkv_scan",
    )(r, w, kt, kk, bv, v, _pair_state(s0))
    return o, _unpair_state(sf)


def _rope(x, c, sa, sb):
    w = x.shape[1]
    return x * c + pltpu.roll(x, w - ROT_HALF, 1) * sa + pltpu.roll(x, ROT_HALF, 1) * sb


def _attn_prep_kernel(p_ref, c_ref, sa_ref, sb_ref, g_ref, b_ref, q_o, k_o, qi_o, ki_o):
    c1, sa1, sb1 = c_ref[...], sa_ref[...], sb_ref[...]
    rep = lambda t, n: jnp.concatenate([t] * n, axis=1)
    nq = ATTN_WIDTH // LANE
    nk = KV_W // LANE
    q_o[...] = _rope(p_ref[:, 0:ATTN_WIDTH], rep(c1, nq), rep(sa1, nq), rep(sb1, nq))
    k_o[...] = _rope(p_ref[:, ATTN_WIDTH:ATTN_WIDTH + KV_W], rep(c1, nk), rep(sa1, nk), rep(sb1, nk))
    qi0 = ATTN_WIDTH + 2 * KV_W
    qi_o[...] = _rope(p_ref[:, qi0:qi0 + IDX_HEADS * IDX_DIM], rep(c1, nq), rep(sa1, nq), rep(sb1, nq))
    x = p_ref[:, KI_OFF:KI_OFF + LANE]
    lane = lax.broadcasted_iota(I32, x.shape, 1)
    isk = lane < IDX_DIM
    mu = jnp.sum(jnp.where(isk, x, 0.0), axis=-1, keepdims=True) * (1.0 / IDX_DIM)
    xc = jnp.where(isk, x - mu, 0.0)
    var = jnp.sum(xc * xc, axis=-1, keepdims=True) * (1.0 / IDX_DIM)
    y = xc * lax.rsqrt(var + LN_EPS) * g_ref[...] + b_ref[...]
    y = _rope(y, jnp.where(isk, c1, 1.0), jnp.where(isk, sa1, 0.0), jnp.where(isk, sb1, 0.0))
    ki_o[...] = jnp.where(isk, y, x)


def _attn_prep(p_at, tabs, idx_g, idx_b, tm):
    n = p_at.shape[0]
    tm = min(tm, n)
    tpb = tabs[0].shape[0] // tm
    tok = lambda w: pl.BlockSpec((tm, w), lambda i: (i, 0))
    tab = pl.BlockSpec((tm, LANE), lambda i: (i % tpb, 0))
    row = pl.BlockSpec((1, LANE), lambda i: (0, 0))
    return pl.pallas_call(
        _attn_prep_kernel,
        grid=(n // tm,),
        in_specs=[tok(AT_PAD_W), tab, tab, tab, row, row],
        out_specs=[tok(ATTN_WIDTH), tok(KV_W), tok(IDX_HEADS * IDX_DIM), tok(LANE)],
        out_shape=[jax.ShapeDtypeStruct((n, ATTN_WIDTH), F32), jax.ShapeDtypeStruct((n, KV_W), F32),
                   jax.ShapeDtypeStruct((n, IDX_HEADS * IDX_DIM), F32), jax.ShapeDtypeStruct((n, LANE), F32)],
        compiler_params=_cparams("parallel"),
        name="attn_prep",
    )(p_at, tabs[0], tabs[1], tabs[2], idx_g, idx_b)


def _rope_tables(pos):
    inv = ROPE_THETA ** (-jnp.arange(ROT_HALF, dtype=F32) * 2.0 / ROT)
    ang = pos.astype(F32)[:, None] * inv[None, :]
    cos, sin = jnp.cos(ang), jnp.sin(ang)
    n = pos.shape[0]
    rest = HEAD_DIM - ROT
    c = jnp.concatenate([cos, cos, jnp.ones((n, rest), F32)], axis=1)
    sa = jnp.concatenate([-sin, jnp.zeros((n, rest + ROT_HALF), F32)], axis=1)
    sb = jnp.concatenate([jnp.zeros((n, ROT_HALF), F32), sin, jnp.zeros((n, rest), F32)], axis=1)
    two = lambda t: jnp.concatenate([t, t], axis=1)
    return two(c), two(sa), two(sb)


def _select_bounds(key_ref, n_tiles, rows, kt, k_sel, idx_bits):
    def count(pred):
        def body(i, acc):
            off = pl.multiple_of(i * kt, kt)
            idx = off + lax.broadcasted_iota(I32, (rows, kt), 1)
            hit = jnp.where(pred(key_ref[:, pl.ds(off, kt)], idx), 1.0, 0.0)
            for c in range(kt // LANE):
                acc = acc + hit[:, c * LANE:(c + 1) * LANE]
            return acc
        acc = lax.fori_loop(0, n_tiles, body, jnp.zeros((rows, LANE), F32))
        return jnp.sum(acc, axis=-1, keepdims=True)

    def thr_bit(i, res):
        cand = res + jnp.left_shift(jnp.int32(1), 31 - i)
        c = count(lambda key, idx: key >= cand)
        return jnp.where(c >= k_sel, cand, res)

    thr = lax.fori_loop(0, 32, thr_bit, jnp.full((rows, 1), INT_MIN, I32))
    n_ge = count(lambda key, idx: key >= thr)

    def tie_search():
        need = k_sel - count(lambda key, idx: key > thr)

        def idx_bit(i, res):
            cand = res | jnp.left_shift(jnp.int32(1), idx_bits - 1 - i)
            c = count(lambda key, idx: (key == thr) & (idx < cand))
            return jnp.where(c < need, cand, res)

        return lax.fori_loop(0, idx_bits, idx_bit, jnp.zeros((rows, 1), I32))

    jmax = lax.cond(jnp.max(n_ge) > k_sel, tie_search,
                    lambda: jnp.full((rows, 1), 2 ** idx_bits - 1, I32))
    return thr, jmax


def _dsa_prompt_kernel(kt, n_sel, idx_bits, q_ref, qi_ref, wi_ref, k_ref, v_ref, ki_ref, o_ref,
                       key_ref, m_ref, l_ref, acc_ref):
    i = pl.program_id(1)
    n_kt = (i * Q_BLOCK + Q_BLOCK + kt - 1) // kt
    qi = qi_ref[0]
    qis = jnp.concatenate([qi[:, h * IDX_DIM:(h + 1) * IDX_DIM] for h in range(IDX_HEADS)], axis=0)
    wi = wi_ref[0] * IDX_HEADS ** -0.5
    qpos = i * Q_BLOCK + lax.broadcasted_iota(I32, (Q_BLOCK, kt), 0)
    lane = lax.broadcasted_iota(I32, (Q_BLOCK, kt), 1)

    def scores(t, carry):
        off = pl.multiple_of(t * kt, kt)
        s = lax.dot_general(qis, ki_ref[0, pl.ds(off, kt), :], _NT, preferred_element_type=F32)
        acc = jnp.zeros((Q_BLOCK, kt), F32)
        for h in range(IDX_HEADS):
            acc = acc + wi[:, h:h + 1] * jnp.maximum(s[h * Q_BLOCK:(h + 1) * Q_BLOCK] * IDX_DIM ** -0.5, 0.0)
        key_ref[:, pl.ds(off, kt)] = jnp.where(off + lane <= qpos, _f2key(acc), INT_MIN)
        return carry

    lax.fori_loop(0, n_kt, scores, 0)
    thr, jmax = _select_bounds(key_ref, n_kt, Q_BLOCK, kt, n_sel, idx_bits)

    q = q_ref[0].astype(F32) * HEAD_DIM ** -0.5
    grp = lax.broadcasted_iota(I32, (Q_BLOCK, KV_W), 1) >> 6
    rep = ATTN_HEADS // KV_HEADS

    def expand(h):
        qh = q[:, h * HEAD_DIM:(h + 1) * HEAD_DIM]
        return jnp.where(grp == h // rep, jnp.concatenate([qh] * KV_HEADS, axis=1), 0.0).astype(BF16)

    qe = [expand(h) for h in range(ATTN_HEADS)]
    m_ref[...] = jnp.full(m_ref.shape, NEG_BIG, F32)
    l_ref[...] = jnp.zeros(l_ref.shape, F32)
    acc_ref[...] = jnp.zeros(acc_ref.shape, F32)

    def attend(t, carry):
        off = pl.multiple_of(t * kt, kt)
        key = key_ref[:, pl.ds(off, kt)]
        kidx = off + lane
        sel = (kidx <= qpos) & ((key > thr) | ((key == thr) & (kidx <= jmax)))
        kt_tile = k_ref[0, pl.ds(off, kt), :]
        vt_tile = v_ref[0, pl.ds(off, kt), :]
        qk = lambda h: lax.dot_general(qe[h], kt_tile, _NT, preferred_element_type=F32)

        def finish(h, p, alpha):
            acc_ref[h] = alpha * acc_ref[h] + jnp.dot(p, vt_tile, preferred_element_type=F32)

        nxt = qk(0)
        pending = None
        for h in range(ATTN_HEADS):
            lg = nxt
            if h + 1 < ATTN_HEADS:
                nxt = qk(h + 1)
            lg = jnp.where(sel, lg, NEG_BIG)
            m = m_ref[h]
            mn = jnp.maximum(m, jnp.max(lg, axis=-1, keepdims=True))
            p = jnp.where(sel, jnp.exp(lg - mn), 0.0)
            alpha = jnp.exp(m - mn)
            l_ref[h] = alpha * l_ref[h] + jnp.sum(p, axis=-1, keepdims=True)
            m_ref[h] = mn
            if pending is not None:
                finish(*pending)
            pending = (h, p.astype(BF16), alpha)
        finish(*pending)
        return carry

    lax.fori_loop(0, n_kt, attend, 0)
    pieces = []
    for h in range(ATTN_HEADS):
        g = h // rep
        pieces.append(acc_ref[h][:, g * HEAD_DIM:(g + 1) * HEAD_DIM] / l_ref[h])
    o_ref[0] = jnp.concatenate(pieces, axis=1)


def _dsa_prompt(q, qi, wi, k, v, ki, t_real):
    bsz, tp, _ = q.shape
    kt = 640 if tp % 640 == 0 else LANE
    n_sel = min(TOPK_MAX, t_real // 4)
    idx_bits = max(1, (tp - 1).bit_length())
    blk = lambda w: pl.BlockSpec((1, Q_BLOCK, w), lambda b, i: (b, i, 0))
    seq = lambda w: pl.BlockSpec((1, tp, w), lambda b, i: (b, 0, 0))
    return pl.pallas_call(
        functools.partial(_dsa_prompt_kernel, kt, n_sel, idx_bits),
        grid=(bsz, tp // Q_BLOCK),
        in_specs=[blk(ATTN_WIDTH), blk(IDX_HEADS * IDX_DIM), blk(IDX_HEADS),
                  seq(KV_W), seq(KV_W), seq(IDX_DIM)],
        out_specs=blk(ATTN_WIDTH),
        out_shape=jax.ShapeDtypeStruct((bsz, tp, ATTN_WIDTH), F32),
        scratch_shapes=[pltpu.VMEM((Q_BLOCK, tp), I32),
                        pltpu.VMEM((ATTN_HEADS, Q_BLOCK, 1), F32),
                        pltpu.VMEM((ATTN_HEADS, Q_BLOCK, 1), F32),
                        pltpu.VMEM((ATTN_HEADS, Q_BLOCK, KV_W), F32)],
        compiler_params=_cparams("parallel", "arbitrary"),
        name="dsa_prompt",
    )(q, qi, wi, k, v, ki)


def _dsa_s_scores_kernel(pps, pt_ref, qi_ref, wi_ref, *refs):
    ci_refs, o_ref = refs[:pps], refs[pps]
    qi = qi_ref[0]
    w = wi_ref[0] * IDX_HEADS ** -0.5
    for u in range(pps):
        s = _dot3(qi, ci_refs[u][0, 0], _NT)
        sc = jnp.sum(w * jnp.maximum(s * IDX_DIM ** -0.5, 0.0), axis=0, keepdims=True)
        o_ref[0, :, u * PAGE_SIZE:(u + 1) * PAGE_SIZE] = sc


def _dsa_s_scores(page_table, qi3, wi3, cache_idx, layer):
    bd, npages = page_table.shape
    pps = PAGES_PER_STEP
    page = lambda u: pl.BlockSpec((1, 1, PAGE_SIZE, IDX_DIM),
                                  lambda b, p, pt: (layer, pt[b * npages + p * pps + u], 0, 0))
    gs = pltpu.PrefetchScalarGridSpec(
        num_scalar_prefetch=1,
        grid=(bd, npages // pps),
        in_specs=[pl.BlockSpec((1, IDX_HEADS, IDX_DIM), lambda b, p, pt: (b, 0, 0)),
                  pl.BlockSpec((1, IDX_HEADS, 1), lambda b, p, pt: (b, 0, 0))] + [page(u) for u in range(pps)],
        out_specs=pl.BlockSpec((1, 1, pps * PAGE_SIZE), lambda b, p, pt: (b, 0, p)),
    )
    return pl.pallas_call(
        functools.partial(_dsa_s_scores_kernel, pps),
        grid_spec=gs,
        out_shape=jax.ShapeDtypeStruct((bd, 1, npages * PAGE_SIZE), F32),
        compiler_params=_cparams("parallel", "arbitrary"),
        name="dsa_decode_scores",
    )(page_table.reshape(-1), qi3, wi3, *([cache_idx] * pps))


def _dsa_s_bounds_kernel(past, n_sel, idx_bits, sc_ref, qi_ref, kiw_ref, hm_ref, thr_o, j_o, kn_o, key_ref):
    rows = sc_ref.shape[0]
    kiw = kiw_ref[...]
    lane = lax.broadcasted_iota(I32, kiw.shape, 1)
    rolled = pltpu.roll(kiw, IDX_DIM, 1)
    ki2 = jnp.where(lane < IDX_DIM, kiw, rolled)
    w8 = jnp.where(lane < IDX_HEADS, rolled, 0.0) * IDX_HEADS ** -0.5
    prod = qi_ref[...] * jnp.concatenate([ki2] * (IDX_HEADS // 2), axis=1)
    s = _dot3(prod, hm_ref[...])
    new = jnp.sum(w8 * jnp.maximum(s * IDX_DIM ** -0.5, 0.0), axis=-1, keepdims=True)
    knew = _f2key(new)
    key_ref[:, 0:past] = _f2key(sc_ref[...])
    key_ref[:, past:past + LANE] = jnp.where(lane == 0, knew, INT_MIN)
    thr, jmax = _select_bounds(key_ref, (past + LANE) // LANE, rows, LANE, n_sel, idx_bits)
    thr_o[...] = thr
    j_o[...] = jmax
    kn_o[...] = knew


def _dsa_s_bounds(scores, qi, kiw, hm, past):
    rows = scores.shape[0]
    n_sel = min(TOPK_MAX, (past + 1) // 4)
    idx_bits = (past + LANE - 1).bit_length()
    out = jax.ShapeDtypeStruct((rows, 1), I32)
    return pl.pallas_call(
        functools.partial(_dsa_s_bounds_kernel, past, n_sel, idx_bits),
        out_shape=[out, out, out],
        scratch_shapes=[pltpu.VMEM((rows, past + LANE), I32)],
        compiler_params=pltpu.CompilerParams(vmem_limit_bytes=VMEM_LIMIT),
        name="dsa_decode_bounds",
    )(scores, qi, kiw, hm)


def _dsa_s_attend_kernel(pps, past, pt_ref, thr_ref, j_ref, kn_ref, qe_ref, sc_ref, kn_row_ref, vn_row_ref,
                         fold_ref, *refs):
    ck, cv, o_ref = refs[:pps], refs[pps:2 * pps], refs[2 * pps]
    m_ref, l_ref, acc_ref = refs[2 * pps + 1:]
    b = pl.program_id(0)
    p = pl.program_id(1)

    @pl.when(p == 0)
    def _():
        m_ref[...] = jnp.full(m_ref.shape, NEG_BIG, F32)
        l_ref[...] = jnp.zeros(l_ref.shape, F32)
        acc_ref[...] = jnp.zeros(acc_ref.shape, F32)

    thr, jmax = thr_ref[b], j_ref[b]
    qe = qe_ref[0]
    qeb = qe.astype(BF16)
    kidx = p * pps * PAGE_SIZE + lax.broadcasted_iota(I32, (1, pps * PAGE_SIZE), 1)
    kb = jnp.concatenate([ck[u][0, 0].astype(BF16) for u in range(pps)], axis=0)
    vb = jnp.concatenate([cv[u][0, 0].astype(BF16) for u in range(pps)], axis=0)
    lg = lax.dot_general(qeb, kb, _NT, preferred_element_type=F32) * HEAD_DIM ** -0.5
    key = _f2key(sc_ref[0])
    sel = (key > thr) | ((key == thr) & (kidx <= jmax))
    lg = jnp.where(sel, lg, NEG_BIG)
    m = m_ref[...]
    mn = jnp.maximum(m, jnp.max(lg, axis=-1, keepdims=True))
    pr = jnp.where(sel, jnp.exp(lg - mn), 0.0)
    alpha = jnp.exp(m - mn)
    l_ref[...] = alpha * l_ref[...] + jnp.sum(pr, axis=-1, keepdims=True)
    acc_ref[...] = alpha * acc_ref[...] + jnp.dot(pr.astype(BF16), vb, preferred_element_type=F32)
    m_ref[...] = mn

    @pl.when(p == pl.num_programs(1) - 1)
    def _():
        knew = kn_ref[b]
        sel_new = (knew > thr) | ((knew == thr) & (past <= jmax))
        lg = jnp.sum(qe * kn_row_ref[0], axis=-1, keepdims=True) * HEAD_DIM ** -0.5
        m = m_ref[...]
        mn = jnp.where(sel_new, jnp.maximum(m, lg), m)
        pr = jnp.where(sel_new, jnp.exp(lg - mn), 0.0)
        alpha = jnp.exp(m - mn)
        l = alpha * l_ref[...] + pr
        acc = alpha * acc_ref[...] + pr * vn_row_ref[0]
        rowi = lax.broadcasted_iota(I32, acc.shape, 0)
        lanei = lax.broadcasted_iota(I32, acc.shape, 1)
        own = jnp.where((lanei >> 6) == (rowi >> 1), acc / l, 0.0)
        o_ref[0] = _dot_sel(own, fold_ref[...])


def _dsa_s_attend(page_table, thr, jmax, knew, qe, scores, k_new, v_new, fold, cache_k, cache_v, layer):
    bd, npages = page_table.shape
    past = npages * PAGE_SIZE
    pps = PAGES_PER_STEP
    page = lambda u: pl.BlockSpec((1, 1, PAGE_SIZE, KV_W),
                                  lambda b, p, pt, t, j, kn: (layer, pt[b * npages + p * pps + u], 0, 0))
    per_b = lambda s: pl.BlockSpec((1,) + s, lambda b, p, pt, t, j, kn: (b, 0, 0))
    gs = pltpu.PrefetchScalarGridSpec(
        num_scalar_prefetch=4,
        grid=(bd, npages // pps),
        in_specs=[per_b((ATTN_HEADS, KV_W)),
                  pl.BlockSpec((1, 1, pps * PAGE_SIZE), lambda b, p, pt, t, j, kn: (b, 0, p)),
                  per_b((1, KV_W)), per_b((1, KV_W)),
                  pl.BlockSpec(fold.shape, lambda b, p, pt, t, j, kn: (0, 0))]
                 + [page(u) for u in range(pps)] * 2,
        out_specs=per_b((ATTN_HEADS, HEAD_DIM)),
        scratch_shapes=[pltpu.VMEM((ATTN_HEADS, 1), F32), pltpu.VMEM((ATTN_HEADS, 1), F32),
                        pltpu.VMEM((ATTN_HEADS, KV_W), F32)],
    )
    return pl.pallas_call(
        functools.partial(_dsa_s_attend_kernel, pps, past),
        grid_spec=gs,
        out_shape=jax.ShapeDtypeStruct((bd, ATTN_HEADS, HEAD_DIM), F32),
        compiler_params=_cparams("parallel", "arbitrary"),
        name="dsa_decode_attend",
    )(page_table.reshape(-1), thr.reshape(-1), jmax.reshape(-1), knew.reshape(-1),
      qe, scores, k_new, v_new, fold, *([cache_k] * pps), *([cache_v] * pps))


def _merge_ln_kernel(o_ref, bonus_ref, g_ref, at_ref, x_ref, wt_ref, wb_ref, xg_ref, xb_ref,
                     lg_ref, lb_ref, bd_ref, out_ref):
    bd = bd_ref[...]
    o = o_ref[...]
    mean = _dot_sel(o, bd) * (1.0 / HEAD_DIM)
    oc = o - mean
    var = _dot_sel(oc * oc, bd) * (1.0 / HEAD_DIM)
    rw = (oc * lax.rsqrt(var + GN_EPS) * xg_ref[...] + xb_ref[...] + bonus_ref[...]) * g_ref[...]
    f = (jnp.dot(rw.astype(BF16), wt_ref[...], preferred_element_type=F32)
         + jnp.dot(at_ref[...].astype(BF16), wb_ref[...], preferred_element_type=F32))
    out_ref[...] = _layer_norm(DEEPNORM_ALPHA * x_ref[...] + f, lg_ref[...], lb_ref[...])


def _merge_ln(o, bonus, g, at, x, wts, tm):
    n = x.shape[0]
    tm = min(tm, n)
    tok = lambda w: pl.BlockSpec((tm, w), lambda i: (i, 0))
    full = lambda a: pl.BlockSpec(a.shape, lambda i: (0, 0))
    ws = [wts["wo_top"], wts["wo_bot"], wts["lnx_g"], wts["lnx_b"], wts["ln1_g"], wts["ln1_b"], wts["bd"]]
    return pl.pallas_call(
        _merge_ln_kernel,
        grid=(n // tm,),
        in_specs=[tok(RWKV_WIDTH)] * 3 + [tok(ATTN_WIDTH), tok(D_MODEL)] + [full(a) for a in ws],
        out_specs=tok(D_MODEL),
        out_shape=jax.ShapeDtypeStruct((n, D_MODEL), F32),
        compiler_params=_cparams("parallel"),
        name="merge_ln",
    )(o, bonus, g, at, x, *ws)


def _take_top(src_ref, n_rows, val_ref, idx_ref):
    shape = src_ref.shape
    row = lax.broadcasted_iota(I32, shape, 1)

    def body(a, carry):
        sv = src_ref[...]
        m = jnp.max(sv, axis=1, keepdims=True)
        idx = jnp.min(jnp.where(sv == m, row, n_rows), axis=1, keepdims=True)
        val_ref[:, pl.ds(a, 1), :] = m
        idx_ref[:, pl.ds(a, 1), :] = idx
        src_ref[...] = jnp.where(row == idx, -jnp.inf, sv)
        return carry

    lax.fori_loop(0, PEER_TOPK, body, 0)


def _peer_route_kernel(x_ref, wq_ref, sk0_ref, sk1_ref, e_o, g_o, s_ref, t_ref, i_ref, c_ref, ts_ref, ic_ref):
    q = jnp.dot(x_ref[...].astype(BF16), wq_ref[...], preferred_element_type=F32)
    for h in range(PEER_HEADS):
        qh = q[:, h * PEER_DKEY:(h + 1) * PEER_DKEY]
        s_ref[h] = _dot3(sk0_ref[...], qh, _NT)
        s_ref[PEER_HEADS + h] = _dot3(sk1_ref[...], qh, _NT)
    _take_top(s_ref, N_KEYS, t_ref, i_ref)
    t1, t2 = t_ref[0:PEER_HEADS], t_ref[PEER_HEADS:2 * PEER_HEADS]
    c_ref[...] = jnp.concatenate([t1[:, a:a + 1, :] + t2 for a in range(PEER_TOPK)], axis=1)
    _take_top(c_ref, PEER_TOPK * PEER_TOPK, ts_ref, ic_ref)
    ic = ic_ref[...]
    i1, i2 = i_ref[0:PEER_HEADS], i_ref[PEER_HEADS:2 * PEER_HEADS]
    ia, ib = ic >> 4, ic & (PEER_TOPK - 1)
    e = jnp.zeros(ic.shape, I32)
    for a in range(PEER_TOPK):
        e = e + jnp.where(ia == a, i1[:, a:a + 1, :] * N_KEYS, 0) + jnp.where(ib == a, i2[:, a:a + 1, :], 0)
    ts = ts_ref[...]
    ex = jnp.exp(ts - jnp.max(ts, axis=1, keepdims=True))
    e_o[0] = e
    g_o[0] = ex / jnp.sum(ex, axis=1, keepdims=True)


def _peer_route(x, wq, sk0, sk1, tt):
    n = x.shape[0]
    nt = n // tt
    full = lambda a: pl.BlockSpec(a.shape, lambda i: (0, 0))
    out = pl.BlockSpec((1, PEER_HEADS, PEER_TOPK, tt), lambda i: (i, 0, 0, 0))
    return pl.pallas_call(
        _peer_route_kernel,
        grid=(nt,),
        in_specs=[pl.BlockSpec((tt, D_MODEL), lambda i: (i, 0)), full(wq), full(sk0), full(sk1)],
        out_specs=[out, out],
        out_shape=[jax.ShapeDtypeStruct((nt, PEER_HEADS, PEER_TOPK, tt), I32),
                   jax.ShapeDtypeStruct((nt, PEER_HEADS, PEER_TOPK, tt), F32)],
        scratch_shapes=[pltpu.VMEM((2 * PEER_HEADS, N_KEYS, tt), F32),
                        pltpu.VMEM((2 * PEER_HEADS, PEER_TOPK, tt), F32),
                        pltpu.VMEM((2 * PEER_HEADS, PEER_TOPK, tt), I32),
                        pltpu.VMEM((PEER_HEADS, PEER_TOPK * PEER_TOPK, tt), F32),
                        pltpu.VMEM((PEER_HEADS, PEER_TOPK, tt), F32),
                        pltpu.VMEM((PEER_HEADS, PEER_TOPK, tt), I32)],
        compiler_params=_cparams("parallel"),
        name="peer_route",
    )(x, wq, sk0, sk1)


def _peer_gather_kernel(tt, n_valid, idx_hbm, x_ref, g_ref, uv_hbm, lg_ref, lb_ref, o_ref,
                        idx_smem, buf0, buf1, buf2, sem, isem, y_ref):
    i = pl.program_id(0)
    n_idx = PEER_SLOTS * tt
    islot = i % 2
    bufs = (buf0, buf1, buf2)

    def idx_copy(tile, slot):
        return pltpu.make_async_copy(idx_hbm.at[tile], idx_smem.at[pl.ds(slot * n_idx, n_idx)], isem.at[slot])

    @pl.when(i == 0)
    def _():
        idx_copy(0, 0).start()

    if n_valid < tt:
        y_ref[...] = jnp.zeros(y_ref.shape, F32)
    idx_copy(i, islot).wait()

    @pl.when(i + 1 < pl.num_programs(0))
    def _():
        idx_copy(i + 1, 1 - islot).start()

    base = islot * n_idx

    def issue(t, k):
        for s in range(PEER_SLOTS):
            e = idx_smem[base + s * tt + t]
            pltpu.make_async_copy(uv_hbm.at[pl.ds(e, 1)], bufs[k].at[pl.ds(s, 1)], sem.at[k]).start()

    def wait(k):
        pltpu.make_async_copy(uv_hbm.at[pl.ds(0, PEER_SLOTS)], bufs[k], sem.at[k]).wait()

    lane = lax.broadcasted_iota(I32, (PEER_SLOTS, tt), 1)

    def compute(t, k):
        xrow = x_ref[pl.ds(t, 1), :]
        word = bufs[k][...]
        u = lax.bitcast_convert_type(word << 16, F32)
        v = lax.bitcast_convert_type(word & jnp.int32(-65536), F32)
        h = jnp.sum(u * xrow, axis=-1, keepdims=True)
        gate = jnp.sum(jnp.where(lane == t, g_ref[0], 0.0), axis=-1, keepdims=True)
        coef = gate * jax.nn.gelu(h)
        y_ref[pl.ds(t, 1), :] = jnp.sum(coef * v, axis=0, keepdims=True)

    depth = len(bufs)
    ahead = depth - 1
    n_main = n_valid - ahead
    assert n_main >= 0 and n_main % depth == 0
    for t in range(ahead):
        issue(t, t)

    def body(i3, carry):
        for k in range(depth):
            t = i3 * depth + k
            wait(k)
            issue(t + ahead, (k + ahead) % depth)
            compute(t, k)
        return carry

    lax.fori_loop(0, n_main // depth, body, 0)
    for t in range(n_main, n_valid):
        wait(t % depth)
        compute(t, t % depth)
    o_ref[...] = _layer_norm(DEEPNORM_ALPHA * x_ref[...] + y_ref[...], lg_ref[...], lb_ref[...])


def _peer_gather(idx, gates, x, uv, ln_g, ln_b, tt, n_valid):
    n = x.shape[0]
    nt = n // tt
    row = pl.BlockSpec((1, D_MODEL), lambda i: (0, 0))
    return pl.pallas_call(
        functools.partial(_peer_gather_kernel, tt, n_valid),
        grid=(nt,),
        in_specs=[pl.BlockSpec(memory_space=pl.ANY),
                  pl.BlockSpec((tt, D_MODEL), lambda i: (i, 0)),
                  pl.BlockSpec((1, PEER_SLOTS, tt), lambda i: (i, 0, 0)),
                  pl.BlockSpec(memory_space=pl.ANY), row, row],
        out_specs=pl.BlockSpec((tt, D_MODEL), lambda i: (i, 0)),
        out_shape=jax.ShapeDtypeStruct((n, D_MODEL), F32),
        scratch_shapes=[pltpu.SMEM((2 * PEER_SLOTS * tt,), I32)]
                       + [pltpu.VMEM((PEER_SLOTS, D_MODEL), I32)] * 3
                       + [pltpu.SemaphoreType.DMA((3,)),
                          pltpu.SemaphoreType.DMA((2,)),
                          pltpu.VMEM((tt, D_MODEL), F32)],
        compiler_params=_cparams("arbitrary"),
        name="peer_gather",
    )(idx, x, gates, uv, ln_g, ln_b)


def _peer(x, wts, tt, n_valid):
    e, gates = _peer_route(x, wts["wq"], wts["sk0"], wts["sk1"], tt)
    nt = x.shape[0] // tt
    return _peer_gather(e.reshape(nt, PEER_SLOTS * tt), gates.reshape(nt, PEER_SLOTS, tt), x,
                        wts["uv"], wts["ln2_g"], wts["ln2_b"], tt, n_valid)


def _pack_bf16_pair(lo, hi):
    bits = lambda a: lax.bitcast_convert_type(a.astype(BF16), jnp.uint16).astype(jnp.uint32)
    return lax.bitcast_convert_type(bits(lo) | (bits(hi) << 16), I32)


def _layer_weights(l, w_in, shift_mu, decay_w0, decay_up, iclr_a0, iclr_up, gate_up, k_k, k_a, r_k,
                   lnx_g, lnx_b, idx_ln_g, idx_ln_b, w_out, ln1_g, ln1_b, ln2_g, ln2_b,
                   peer_wq, peer_subkeys, peer_u, peer_v):
    row = lambda a: a.reshape(1, -1).astype(F32)
    lora = jnp.zeros((LORA_W, 3 * RWKV_WIDTH), F32)
    lora = lora.at[0:W_LORA, 0:RWKV_WIDTH].set(decay_up[l])
    lora = lora.at[W_LORA:W_LORA + A_LORA, RWKV_WIDTH:2 * RWKV_WIDTH].set(iclr_up[l])
    lora = lora.at[W_LORA + A_LORA:, 2 * RWKV_WIDTH:].set(gate_up[l])
    lora_hi = lora.astype(BF16)
    pad_lane = lambda a: jnp.pad(a.reshape(1, -1), ((0, 0), (0, LANE - a.size)))
    half = PEER_DKEY // 2
    seg = jnp.arange(RWKV_WIDTH) // HEAD_DIM
    return dict(
        w_rw=w_in[l][:, :SHIFT_W].astype(BF16),
        w_at=jnp.pad(w_in[l][:, SHIFT_W:], ((0, 0), (0, AT_PAD_W - ATTN_PROJ_W))).astype(BF16),
        mu=row(shift_mu[l]), w0=row(decay_w0[l]), a0=row(iclr_a0[l]),
        lora_hi=lora_hi, lora_lo=(lora - lora_hi.astype(F32)).astype(BF16),
        k_k=row(k_k[l]), k_a=row(k_a[l]), r_k=row(r_k[l]),
        bd=(seg[:, None] == seg[None, :]).astype(BF16),
        lnx_g=row(lnx_g[l]), lnx_b=row(lnx_b[l]),
        idx_g=pad_lane(idx_ln_g[l]), idx_b=pad_lane(idx_ln_b[l]),
        wo_top=w_out[l][:RWKV_WIDTH].astype(BF16), wo_bot=w_out[l][RWKV_WIDTH:].astype(BF16),
        ln1_g=row(ln1_g[l]), ln1_b=row(ln1_b[l]), ln2_g=row(ln2_g[l]), ln2_b=row(ln2_b[l]),
        wq=peer_wq[l].astype(BF16),
        sk0=jnp.pad(peer_subkeys[l, 0], ((0, 0), (0, half))),
        sk1=jnp.pad(peer_subkeys[l, 1], ((0, 0), (half, 0))),
        uv=_pack_bf16_pair(peer_u[l], peer_v[l]),
    )


def _tile(n, cap):
    if n <= cap:
        return n
    return max(d for d in range(8, cap + 1, 8) if n % d == 0)


def _mixer_front(x, wts, tabs):
    tm = _tile(x.shape[0], 512)
    p_rw = _matmul(x, wts["w_rw"], tm)
    p_at = _matmul(x, wts["w_at"], tm)
    q, k, qi, kiw = _attn_prep(p_at, tabs, wts["idx_g"], wts["idx_b"], _tile(tabs[0].shape[0], 640))
    v = p_at[:, ATTN_WIDTH + KV_W:ATTN_WIDTH + 2 * KV_W]
    return p_rw, q, k, v, qi, kiw


def kernel(x_prompt, x_sample, cache_k, cache_v, cache_idx_k, state_wkv, state_shift, page_table, meta_tokens, w_in, shift_mu, decay_w0, decay_up, iclr_a0, iclr_up, gate_up, k_k, k_a, r_k, lnx_g, lnx_b, idx_ln_g, idx_ln_b, w_out, ln1_g, ln1_b, ln2_g, ln2_b, peer_wq, peer_subkeys, peer_u, peer_v):
    bsz, seq, _ = x_prompt.shape
    t = seq + N_META
    tp = -(-t // LANE) * LANE
    n_p = bsz * tp
    depth = w_in.shape[0]
    bd_, ts_, _ = x_sample.shape
    assert ts_ == 1
    npages = page_table.shape[1]
    past = npages * PAGE_SIZE
    n_pool = cache_k.shape[1]
    peer_tt = LANE
    ns_pad = -(-bd_ // peer_tt) * peer_tt

    xp = jnp.concatenate([jnp.broadcast_to(meta_tokens[None], (bsz, N_META, D_MODEL)), x_prompt], axis=1)
    xp = jnp.pad(xp, ((0, 0), (0, tp - t), (0, 0))).reshape(n_p, D_MODEL)
    xs = x_sample.reshape(bd_, D_MODEL)
    tabs_p = _rope_tables(jnp.arange(tp, dtype=I32))
    tabs_s = _rope_tables(jnp.full((bd_,), past, I32))
    ck = cache_k.reshape(depth, n_pool, PAGE_SIZE, KV_W)
    cv = cache_v.reshape(depth, n_pool, PAGE_SIZE, KV_W)
    hsel = jnp.arange(IDX_HEADS * IDX_DIM)[:, None] // IDX_DIM == jnp.arange(LANE)[None, :]
    hm = hsel.astype(F32)
    fold = (jnp.arange(KV_W)[:, None] % HEAD_DIM == jnp.arange(HEAD_DIM)[None, :]).astype(BF16)
    own = (jnp.arange(KV_W)[None, :] // HEAD_DIM == jnp.arange(ATTN_HEADS)[:, None] // (ATTN_HEADS // KV_HEADS))

    k_p, v_p, ki_p, wkv_p, sh_p = [], [], [], [], []
    k_s, v_s, ki_s, wkv_s, sh_s = [], [], [], [], []
    for l in range(depth):
        wts = _layer_weights(l, w_in, shift_mu, decay_w0, decay_up, iclr_a0, iclr_up, gate_up, k_k, k_a, r_k,
                             lnx_g, lnx_b, idx_ln_g, idx_ln_b, w_out, ln1_g, ln1_b, ln2_g, ln2_b,
                             peer_wq, peer_subkeys, peer_u, peer_v)

        p_rw, q, k, v, qi, kiw = _mixer_front(xp, wts, tabs_p)
        p_rw3 = p_rw.reshape(bsz, tp, SHIFT_W)
        r_, w_, kt_, kk_, b_, vv_, bonus, g = _rwkv_prep(
            p_rw3, jnp.zeros((bsz, 1, SHIFT_W), F32), t, True, wts, LANE)
        o, s_fin = _wkv(r_, w_, kt_, kk_, b_, vv_,
                        jnp.zeros((bsz, RWKV_HEADS, HEAD_DIM, HEAD_DIM), F32), bsz, LANE)
        three = lambda a: a.reshape(bsz, tp, -1)
        at = _dsa_prompt(three(q).astype(BF16), three(qi).astype(BF16),
                         three(kiw)[:, :, IDX_DIM:IDX_DIM + IDX_HEADS],
                         three(k).astype(BF16), three(v).astype(BF16),
                         three(kiw)[:, :, :IDX_DIM].astype(BF16), t)
        flat = lambda a: a.reshape(n_p, -1)
        x1 = _merge_ln(flat(o), flat(bonus), flat(g), flat(at), xp, wts, _tile(n_p, 256))
        xp = _peer(x1, wts, peer_tt, peer_tt)
        k_p.append(three(k)[:, :t].reshape(bsz, t, KV_HEADS, HEAD_DIM))
        v_p.append(three(v)[:, :t].reshape(bsz, t, KV_HEADS, HEAD_DIM))
        ki_p.append(three(kiw)[:, :t, :IDX_DIM])
        wkv_p.append(s_fin)
        sh_p.append(p_rw3[:, t - 1])

        p_rw, q, k, v, qi, kiw = _mixer_front(xs, wts, tabs_s)
        r_, w_, kt_, kk_, b_, vv_, bonus, g = _rwkv_prep(
            p_rw[None], state_shift[l][None], bd_, False, wts, bd_)
        tc_s = 8
        padt = lambda a, c: jnp.pad(a[0][:, None, :], ((0, 0), (0, tc_s - 1), (0, 0)), constant_values=c)
        o, s_fin = _wkv(padt(r_, 0.0), padt(w_, 1.0), padt(kt_, 0.0), padt(kk_, 0.0), padt(b_, 0.0),
                        padt(vv_, 0.0), state_wkv[l].astype(F32), 4, tc_s)
        o = o[:, 0]
        scores = _dsa_s_scores(page_table, qi.reshape(bd_, IDX_HEADS, IDX_DIM),
                               kiw[:, IDX_DIM:IDX_DIM + IDX_HEADS].reshape(bd_, IDX_HEADS, 1),
                               cache_idx_k, l)
        thr, jmax, knew = _dsa_s_bounds(scores.reshape(bd_, past), qi, kiw, hm, past)
        qe = jnp.where(own[None], jnp.tile(q.reshape(bd_, ATTN_HEADS, HEAD_DIM), (1, 1, KV_HEADS)), 0.0)
        at = _dsa_s_attend(page_table, thr, jmax, knew, qe, scores, k[:, None, :], v[:, None, :], fold,
                           ck, cv, l).reshape(bd_, ATTN_WIDTH)
        x1 = _merge_ln(o, bonus[0], g[0], at, xs, wts, bd_)
        x1p = jnp.pad(x1, ((0, ns_pad - bd_), (0, 0)))
        xs = _peer(x1p, wts, peer_tt, bd_)[:bd_]
        k_s.append(k.reshape(bd_, 1, KV_HEADS, HEAD_DIM))
        v_s.append(v.reshape(bd_, 1, KV_HEADS, HEAD_DIM))
        ki_s.append(kiw[:, None, :IDX_DIM])
        wkv_s.append(s_fin)
        sh_s.append(p_rw)

    y_prompt = xp.reshape(bsz, tp, D_MODEL)[:, N_META:t]
    y_sample = xs.reshape(bd_, 1, D_MODEL)
    return (y_prompt, y_sample, jnp.stack(k_p), jnp.stack(v_p), jnp.stack(ki_p),
            jnp.stack(wkv_p).astype(state_wkv.dtype), jnp.stack(sh_p).astype(state_shift.dtype),
            jnp.stack(k_s), jnp.stack(v_s), jnp.stack(ki_s),
            jnp.stack(wkv_s).astype(state_wkv.dtype), jnp.stack(sh_s).astype(state_shift.dtype))
```

```python
import functools
import math

import jax
import jax.numpy as jnp
from jax import lax
from jax.experimental import pallas as pl
from jax.experimental.pallas import tpu as pltpu

F32 = jnp.float32
BF16 = jnp.bfloat16
I32 = jnp.int32

D_MODEL = 1024
N_META = 16
HEAD_DIM = 64
RWKV_WIDTH = D_MODEL // 2
RWKV_HEADS = RWKV_WIDTH // HEAD_DIM
ATTN_WIDTH = D_MODEL - RWKV_WIDTH
ATTN_HEADS = ATTN_WIDTH // HEAD_DIM
KV_HEADS = ATTN_HEADS // 2
KV_W = KV_HEADS * HEAD_DIM
W_LORA = 64
A_LORA = 64
G_LORA = 128
LORA_W = W_LORA + A_LORA + G_LORA
SHIFT_W = 3 * RWKV_WIDTH + LORA_W
IDX_HEADS = 8
IDX_DIM = 64
ATTN_PROJ_W = ATTN_WIDTH + 2 * KV_W + IDX_HEADS * IDX_DIM + IDX_DIM + IDX_HEADS
TOPK_MAX = 256
ROPE_THETA = 500000.0
ROT = HEAD_DIM // 4
ROT_HALF = ROT // 2
PEER_HEADS = 8
PEER_DKEY = 128
N_KEYS = 128
PEER_TOPK = 16
PEER_SLOTS = PEER_HEADS * PEER_TOPK
DEPTH = 2
DEEPNORM_ALPHA = (2.0 * DEPTH) ** 0.25
PAGE_SIZE = 128
LN_EPS = 1e-5
GN_EPS = 64e-5

LANE = 128
SUBLANE = 8
Q_BLOCK = 128
INT_MIN = -(2 ** 31)
NEG_BIG = -1e30
VMEM_LIMIT = 56 * 1024 * 1024
AT_PAD_W = 1664
KI_OFF = ATTN_WIDTH + 2 * KV_W + IDX_HEADS * IDX_DIM
PAGES_PER_STEP = 8

_NT = (((1,), (1,)), ((), ()))


def _cparams(*sem):
    return pltpu.CompilerParams(dimension_semantics=sem, vmem_limit_bytes=VMEM_LIMIT)


def _split2(x):
    hi = x.astype(BF16)
    lo = (x - hi.astype(F32)).astype(BF16)
    return hi, lo


def _split3(x):
    hi = x.astype(BF16)
    r1 = x - hi.astype(F32)
    mid = r1.astype(BF16)
    lo = (r1 - mid.astype(F32)).astype(BF16)
    return hi, mid, lo


def _dot3(a, b, dims=None):
    ah, al = _split2(a)
    bh, bl = _split2(b)
    if dims is None:
        d = lambda p, q: jnp.dot(p, q, preferred_element_type=F32)
    else:
        d = lambda p, q: lax.dot_general(p, q, dims, preferred_element_type=F32)
    return d(ah, bh) + d(al, bh) + d(ah, bl)


def _dot_sel(x, m):
    h, mid, lo = _split3(x)
    d = lambda p: jnp.dot(p, m, preferred_element_type=F32)
    return d(h) + d(mid) + d(lo)


def _f2key(x):
    x = jnp.where(x == 0.0, 0.0, x)
    b = lax.bitcast_convert_type(x, I32)
    return b ^ ((b >> 31) & 0x7FFFFFFF)


def _layer_norm(z, g, b):
    mu = jnp.mean(z, axis=-1, keepdims=True)
    zc = z - mu
    var = jnp.mean(zc * zc, axis=-1, keepdims=True)
    return zc * lax.rsqrt(var + LN_EPS) * g + b


def _mm_kernel(x_ref, w_ref, o_ref):
    o_ref[...] = jnp.dot(x_ref[...].astype(BF16), w_ref[...], preferred_element_type=F32)


def _matmul(x, w, tm):
    m, k = x.shape
    n = w.shape[1]
    tm = min(tm, m)
    return pl.pallas_call(
        _mm_kernel,
        grid=(m // tm,),
        in_specs=[pl.BlockSpec((tm, k), lambda i: (i, 0)),
                  pl.BlockSpec((k, n), lambda i: (0, 0))],
        out_specs=pl.BlockSpec((tm, n), lambda i: (i, 0)),
        out_shape=jax.ShapeDtypeStruct((m, n), F32),
        compiler_params=_cparams("parallel"),
        name="proj_matmul",
    )(x, w)


def _rwkv_prep_kernel(t_real, tt, shift, p_ref, prev_ref, mu_ref, w0_ref, a0_ref, lwh_ref, lwl_ref,
                      kk_ref, ka_ref, rk_ref, bd_ref,
                      r_o, w_o, kt_o, kko_o, b_o, v_o, bonus_o, g_o, carry_ref):
    j = pl.program_id(1)
    pf = p_ref[0]
    if shift:
        @pl.when(j == 0)
        def _():
            carry_ref[...] = prev_ref[0]
        row = lax.broadcasted_iota(I32, pf.shape, 0)
        prev = jnp.where(row == 0, carry_ref[...], pltpu.roll(pf, 1, 0))
        carry_ref[...] = pf[tt - 1:tt, :]
    else:
        prev = prev_ref[0]
    xs = pf + mu_ref[...] * (prev - pf)
    r = xs[:, 0:RWKV_WIDTH]
    k = xs[:, RWKV_WIDTH:2 * RWKV_WIDTH]
    v = xs[:, 2 * RWKV_WIDTH:3 * RWKV_WIDTH]
    z = xs[:, 3 * RWKV_WIDTH:SHIFT_W]
    lane = lax.broadcasted_iota(I32, z.shape, 1)
    zt = jnp.where(lane < W_LORA, jnp.tanh(z),
                   jnp.where(lane < W_LORA + A_LORA, z, jax.nn.sigmoid(z)))
    zh, zl = _split2(zt)
    d = lambda p, q: jnp.dot(p, q, preferred_element_type=F32)
    lo = d(zh, lwh_ref[...]) + d(zl, lwh_ref[...]) + d(zh, lwl_ref[...])
    w_raw = w0_ref[...] + lo[:, 0:RWKV_WIDTH]
    a = jax.nn.sigmoid(a0_ref[...] + lo[:, RWKV_WIDTH:2 * RWKV_WIDTH])
    g = lo[:, 2 * RWKV_WIDTH:3 * RWKV_WIDTH]
    decay = jnp.exp(-math.exp(-0.5) * jax.nn.sigmoid(w_raw))
    bd = bd_ref[...]
    kk = k * kk_ref[...]
    kk = kk / jnp.maximum(jnp.sqrt(_dot_sel(kk * kk, bd)), 1e-12)
    kt = k * (1.0 + (a - 1.0) * ka_ref[...])
    bonus = _dot_sel(r * kt * rk_ref[...], bd) * v
    pos = j * tt + lax.broadcasted_iota(I32, r.shape, 0)
    valid = pos < t_real
    r_o[0] = r
    w_o[0] = jnp.where(valid, decay, 1.0)
    kt_o[0] = jnp.where(valid, kt, 0.0)
    kko_o[0] = jnp.where(valid, kk, 0.0)
    b_o[0] = jnp.where(valid, kk * a, 0.0)
    v_o[0] = v
    bonus_o[0] = bonus
    g_o[0] = g


def _rwkv_prep(p_rw, prev, t_real, shift, wts, tt):
    bsz, tp, _ = p_rw.shape
    tt = min(tt, tp)
    row = lambda n: pl.BlockSpec((1, n), lambda b, j: (0, 0))
    full = lambda a: pl.BlockSpec(a.shape, lambda b, j: (0, 0))
    tok = lambda n: pl.BlockSpec((1, tt, n), lambda b, j: (b, j, 0))
    prev_spec = pl.BlockSpec((1, 1, SHIFT_W), lambda b, j: (b, 0, 0)) if shift else tok(SHIFT_W)
    outs = pl.pallas_call(
        functools.partial(_rwkv_prep_kernel, t_real, tt, shift),
        grid=(bsz, tp // tt),
        in_specs=[tok(SHIFT_W), prev_spec, row(SHIFT_W), row(RWKV_WIDTH), row(RWKV_WIDTH),
                  full(wts["lora_hi"]), full(wts["lora_lo"]),
                  row(RWKV_WIDTH), row(RWKV_WIDTH), row(RWKV_WIDTH), full(wts["bd"])],
        out_specs=[tok(RWKV_WIDTH)] * 8,
        out_shape=[jax.ShapeDtypeStruct((bsz, tp, RWKV_WIDTH), F32)] * 8,
        scratch_shapes=[pltpu.VMEM((1, SHIFT_W), F32)],
        compiler_params=_cparams("parallel", "arbitrary"),
        name="rwkv_prep",
    )(p_rw, prev, wts["mu"], wts["w0"], wts["a0"], wts["lora_hi"], wts["lora_lo"],
      wts["k_k"], wts["k_a"], wts["r_k"], wts["bd"])
    return outs


def _wkv_kernel(bb, tc, r_ref, w_ref, kt_ref, kk_ref, b_ref, v_ref, s0_ref, o_ref, sf_ref, s_ref):
    c = pl.program_id(1)

    @pl.when(c == 0)
    def _():
        s_ref[...] = s0_ref[...]

    lane = lax.broadcasted_iota(I32, (HEAD_DIM, LANE), 1)
    row = lax.broadcasted_iota(I32, (HEAD_DIM, LANE), 0)
    lo = lane < HEAD_DIM
    e0 = lane == row
    e1 = lane == row + HEAD_DIM
    e01 = e0 | e1
    r128 = lax.broadcasted_iota(I32, (LANE, LANE), 0)
    l128 = lax.broadcasted_iota(I32, (LANE, LANE), 1)
    half_ones = ((r128 >> 6) == (l128 >> 6)).astype(BF16)
    npair = RWKV_HEADS // 2

    def half_sums_mxu(parts, n_split):
        res = jnp.dot(jnp.concatenate(parts, axis=0), half_ones, preferred_element_type=F32)
        out = []
        for i in range(len(parts) // n_split):
            acc = res[i * n_split * HEAD_DIM:(i * n_split + 1) * HEAD_DIM]
            for p in range(1, n_split):
                acc = acc + res[(i * n_split + p) * HEAD_DIM:(i * n_split + p + 1) * HEAD_DIM]
            out.append(acc)
        return out

    def group(gi, carry):
        t0 = pl.multiple_of(gi * SUBLANE, SUBLANE)
        rows = pl.ds(t0, SUBLANE)
        blk = lambda ref, b: [ref[b, rows, j * LANE:(j + 1) * LANE] for j in range(npair)]

        def v_pieces(b):
            out = []
            for v8 in blk(v_ref, b):
                vh = v8.astype(BF16).astype(F32)
                r1 = v8 - vh
                vm = r1.astype(BF16).astype(F32)
                out.append((vh, vm, r1 - vm))
            return out

        def v_columns(pieces, u):
            parts = []
            for j in range(npair):
                parts += [jnp.where(e01, pc[u:u + 1], 0.0).astype(BF16) for pc in pieces[j]]
            return half_sums_mxu(parts, 3)

        def out_rows(qparts):
            return [jnp.sum(jnp.where(e01, oc, 0.0), axis=0, keepdims=True) for oc in half_sums_mxu(qparts, 2)]

        pieces = v_pieces(0)
        vcols = [v_columns(pieces, u) for u in range(SUBLANE)]
        pending = None
        for b in range(bb + 1):
            if b < bb:
                kk8, w8, b8, kt8, r8 = blk(kk_ref, b), blk(w_ref, b), blk(b_ref, b), blk(kt_ref, b), blk(r_ref, b)
                st = [s_ref[b, j] for j in range(npair)]
                nxt_pieces = v_pieces(b + 1) if b + 1 < bb else None
            nxt_vcols, qsteps, orows = [], [], []
            for u in range(SUBLANE):
                if b < bb:
                    prods = [st[j] * kk8[j][u:u + 1] for j in range(npair)]
                    sums = [(jnp.sum(jnp.where(lo, p, 0.0), axis=-1, keepdims=True),
                             jnp.sum(jnp.where(lo, 0.0, p), axis=-1, keepdims=True)) for p in prods]
                if nxt_pieces is not None:
                    nxt_vcols.append(v_columns(nxt_pieces, u))
                if pending is not None:
                    orows.append(out_rows(pending[1][u]))
                if b < bb:
                    qs = []
                    for j in range(npair):
                        skk = jnp.where(lo, sums[j][0], sums[j][1])
                        s = st[j] * w8[j][u:u + 1] - skk * b8[j][u:u + 1] + vcols[u][j] * kt8[j][u:u + 1]
                        st[j] = s
                        q = s * r8[j][u:u + 1]
                        qh = q.astype(BF16)
                        qs += [qh, (q - qh.astype(F32)).astype(BF16)]
                    qsteps.append(qs)
            if pending is not None:
                pb = pending[0]
                for j in range(npair):
                    o_ref[pb, rows, j * LANE:(j + 1) * LANE] = jnp.concatenate([orows[u][j] for u in range(SUBLANE)],
                                                                               axis=0)
            if b < bb:
                for j in range(npair):
                    s_ref[b, j] = st[j]
                pending = (b, qsteps)
                vcols = nxt_vcols
                nxt_pieces = None
        return carry

    lax.fori_loop(0, tc // SUBLANE, group, 0)

    @pl.when(c == pl.num_programs(1) - 1)
    def _():
        sf_ref[...] = s_ref[...]


def _pair_state(s):
    b = s.shape[0]
    return (s.reshape(b, RWKV_HEADS // 2, 2, HEAD_DIM, HEAD_DIM)
            .transpose(0, 1, 3, 2, 4).reshape(b, RWKV_HEADS // 2, HEAD_DIM, LANE))


def _unpair_state(s):
    b = s.shape[0]
    return (s.reshape(b, RWKV_HEADS // 2, HEAD_DIM, 2, HEAD_DIM)
            .transpose(0, 1, 3, 2, 4).reshape(b, RWKV_HEADS, HEAD_DIM, HEAD_DIM))


def _wkv(r, w, kt, kk, bv, v, s0, bb, tc):
    bsz, tp, _ = r.shape
    tc = min(tc, tp)
    tok = pl.BlockSpec((bb, tc, RWKV_WIDTH), lambda i, c: (i, c, 0))
    st = pl.BlockSpec((bb, RWKV_HEADS // 2, HEAD_DIM, LANE), lambda i, c: (i, 0, 0, 0))
    o, sf = pl.pallas_call(
        functools.partial(_wkv_kernel, bb, tc),
        grid=(bsz // bb, tp // tc),
        in_specs=[tok] * 6 + [st],
        out_specs=[tok, st],
        out_shape=[jax.ShapeDtypeStruct((bsz, tp, RWKV_WIDTH), F32),
                   jax.ShapeDtypeStruct((bsz, RWKV_HEADS // 2, HEAD_DIM, LANE), F32)],
        scratch_shapes=[pltpu.VMEM((bb, RWKV_HEADS // 2, HEAD_DIM, LANE), F32)],
        compiler_params=_cparams("parallel", "arbitrary"),
        name="wkv_scan",
    )(r, w, kt, kk, bv, v, _pair_state(s0))
    return o, _unpair_state(sf)


def _rope(x, c, sa, sb):
    w = x.shape[1]
    return x * c + pltpu.roll(x, w - ROT_HALF, 1) * sa + pltpu.roll(x, ROT_HALF, 1) * sb


def _attn_prep_kernel(p_ref, c_ref, sa_ref, sb_ref, g_ref, b_ref, q_o, k_o, qi_o, ki_o):
    c1, sa1, sb1 = c_ref[...], sa_ref[...], sb_ref[...]
    rep = lambda t, n: jnp.concatenate([t] * n, axis=1)
    nq = ATTN_WIDTH // LANE
    nk = KV_W // LANE
    q_o[...] = _rope(p_ref[:, 0:ATTN_WIDTH], rep(c1, nq), rep(sa1, nq), rep(sb1, nq))
    k_o[...] = _rope(p_ref[:, ATTN_WIDTH:ATTN_WIDTH + KV_W], rep(c1, nk), rep(sa1, nk), rep(sb1, nk))
    qi0 = ATTN_WIDTH + 2 * KV_W
    qi_o[...] = _rope(p_ref[:, qi0:qi0 + IDX_HEADS * IDX_DIM], rep(c1, nq), rep(sa1, nq), rep(sb1, nq))
    x = p_ref[:, KI_OFF:KI_OFF + LANE]
    lane = lax.broadcasted_iota(I32, x.shape, 1)
    isk = lane < IDX_DIM
    mu = jnp.sum(jnp.where(isk, x, 0.0), axis=-1, keepdims=True) * (1.0 / IDX_DIM)
    xc = jnp.where(isk, x - mu, 0.0)
    var = jnp.sum(xc * xc, axis=-1, keepdims=True) * (1.0 / IDX_DIM)
    y = xc * lax.rsqrt(var + LN_EPS) * g_ref[...] + b_ref[...]
    y = _rope(y, jnp.where(isk, c1, 1.0), jnp.where(isk, sa1, 0.0), jnp.where(isk, sb1, 0.0))
    ki_o[...] = jnp.where(isk, y, x)


def _attn_prep(p_at, tabs, idx_g, idx_b, tm):
    n = p_at.shape[0]
    tm = min(tm, n)
    tpb = tabs[0].shape[0] // tm
    tok = lambda w: pl.BlockSpec((tm, w), lambda i: (i, 0))
    tab = pl.BlockSpec((tm, LANE), lambda i: (i % tpb, 0))
    row = pl.BlockSpec((1, LANE), lambda i: (0, 0))
    return pl.pallas_call(
        _attn_prep_kernel,
        grid=(n // tm,),
        in_specs=[tok(AT_PAD_W), tab, tab, tab, row, row],
        out_specs=[tok(ATTN_WIDTH), tok(KV_W), tok(IDX_HEADS * IDX_DIM), tok(LANE)],
        out_shape=[jax.ShapeDtypeStruct((n, ATTN_WIDTH), F32), jax.ShapeDtypeStruct((n, KV_W), F32),
                   jax.ShapeDtypeStruct((n, IDX_HEADS * IDX_DIM), F32), jax.ShapeDtypeStruct((n, LANE), F32)],
        compiler_params=_cparams("parallel"),
        name="attn_prep",
    )(p_at, tabs[0], tabs[1], tabs[2], idx_g, idx_b)


def _rope_tables(pos):
    inv = ROPE_THETA ** (-jnp.arange(ROT_HALF, dtype=F32) * 2.0 / ROT)
    ang = pos.astype(F32)[:, None] * inv[None, :]
    cos, sin = jnp.cos(ang), jnp.sin(ang)
    n = pos.shape[0]
    rest = HEAD_DIM - ROT
    c = jnp.concatenate([cos, cos, jnp.ones((n, rest), F32)], axis=1)
    sa = jnp.concatenate([-sin, jnp.zeros((n, rest + ROT_HALF), F32)], axis=1)
    sb = jnp.concatenate([jnp.zeros((n, ROT_HALF), F32), sin, jnp.zeros((n, rest), F32)], axis=1)
    two = lambda t: jnp.concatenate([t, t], axis=1)
    return two(c), two(sa), two(sb)


def _select_bounds(key_ref, n_tiles, rows, kt, k_sel, idx_bits):
    def count(pred):
        def body(i, acc):
            off = pl.multiple_of(i * kt, kt)
            idx = off + lax.broadcasted_iota(I32, (rows, kt), 1)
            hit = jnp.where(pred(key_ref[:, pl.ds(off, kt)], idx), 1.0, 0.0)
            for c in range(kt // LANE):
                acc = acc + hit[:, c * LANE:(c + 1) * LANE]
            return acc
        acc = lax.fori_loop(0, n_tiles, body, jnp.zeros((rows, LANE), F32))
        return jnp.sum(acc, axis=-1, keepdims=True)

    def thr_bit(i, res):
        cand = res + jnp.left_shift(jnp.int32(1), 31 - i)
        c = count(lambda key, idx: key >= cand)
        return jnp.where(c >= k_sel, cand, res)

    thr = lax.fori_loop(0, 32, thr_bit, jnp.full((rows, 1), INT_MIN, I32))
    n_ge = count(lambda key, idx: key >= thr)

    def tie_search():
        need = k_sel - count(lambda key, idx: key > thr)

        def idx_bit(i, res):
            cand = res | jnp.left_shift(jnp.int32(1), idx_bits - 1 - i)
            c = count(lambda key, idx: (key == thr) & (idx < cand))
            return jnp.where(c < need, cand, res)

        return lax.fori_loop(0, idx_bits, idx_bit, jnp.zeros((rows, 1), I32))

    jmax = lax.cond(jnp.max(n_ge) > k_sel, tie_search,
                    lambda: jnp.full((rows, 1), 2 ** idx_bits - 1, I32))
    return thr, jmax


def _dsa_prompt_kernel(kt, n_sel, idx_bits, q_ref, qi_ref, wi_ref, k_ref, v_ref, ki_ref, o_ref,
                       key_ref, m_ref, l_ref, acc_ref):
    i = pl.program_id(1)
    n_kt = (i * Q_BLOCK + Q_BLOCK + kt - 1) // kt
    qi = qi_ref[0]
    qis = jnp.concatenate([qi[:, h * IDX_DIM:(h + 1) * IDX_DIM] for h in range(IDX_HEADS)], axis=0)
    wi = wi_ref[0] * IDX_HEADS ** -0.5
    qpos = i * Q_BLOCK + lax.broadcasted_iota(I32, (Q_BLOCK, kt), 0)
    lane = lax.broadcasted_iota(I32, (Q_BLOCK, kt), 1)

    def scores(t, carry):
        off = pl.multiple_of(t * kt, kt)
        s = lax.dot_general(qis, ki_ref[0, pl.ds(off, kt), :], _NT, preferred_element_type=F32)
        acc = jnp.zeros((Q_BLOCK, kt), F32)
        for h in range(IDX_HEADS):
            acc = acc + wi[:, h:h + 1] * jnp.maximum(s[h * Q_BLOCK:(h + 1) * Q_BLOCK] * IDX_DIM ** -0.5, 0.0)
        key_ref[:, pl.ds(off, kt)] = jnp.where(off + lane <= qpos, _f2key(acc), INT_MIN)
        return carry

    lax.fori_loop(0, n_kt, scores, 0)
    thr, jmax = _select_bounds(key_ref, n_kt, Q_BLOCK, kt, n_sel, idx_bits)

    q = q_ref[0].astype(F32) * HEAD_DIM ** -0.5
    grp = lax.broadcasted_iota(I32, (Q_BLOCK, KV_W), 1) >> 6
    rep = ATTN_HEADS // KV_HEADS

    def expand(h):
        qh = q[:, h * HEAD_DIM:(h + 1) * HEAD_DIM]
        return jnp.where(grp == h // rep, jnp.concatenate([qh] * KV_HEADS, axis=1), 0.0).astype(BF16)

    qe = [expand(h) for h in range(ATTN_HEADS)]
    m_ref[...] = jnp.full(m_ref.shape, NEG_BIG, F32)
    l_ref[...] = jnp.zeros(l_ref.shape, F32)
    acc_ref[...] = jnp.zeros(acc_ref.shape, F32)

    def attend(t, carry):
        off = pl.multiple_of(t * kt, kt)
        key = key_ref[:, pl.ds(off, kt)]
        kidx = off + lane
        sel = (kidx <= qpos) & ((key > thr) | ((key == thr) & (kidx <= jmax)))
        kt_tile = k_ref[0, pl.ds(off, kt), :]
        vt_tile = v_ref[0, pl.ds(off, kt), :]
        qk = lambda h: lax.dot_general(qe[h], kt_tile, _NT, preferred_element_type=F32)

        def finish(h, p, alpha):
            acc_ref[h] = alpha * acc_ref[h] + jnp.dot(p, vt_tile, preferred_element_type=F32)

        nxt = qk(0)
        pending = None
        for h in range(ATTN_HEADS):
            lg = nxt
            if h + 1 < ATTN_HEADS:
                nxt = qk(h + 1)
            lg = jnp.where(sel, lg, NEG_BIG)
            m = m_ref[h]
            mn = jnp.maximum(m, jnp.max(lg, axis=-1, keepdims=True))
            p = jnp.where(sel, jnp.exp(lg - mn), 0.0)
            alpha = jnp.exp(m - mn)
            l_ref[h] = alpha * l_ref[h] + jnp.sum(p, axis=-1, keepdims=True)
            m_ref[h] = mn
            if pending is not None:
                finish(*pending)
            pending = (h, p.astype(BF16), alpha)
        finish(*pending)
        return carry

    lax.fori_loop(0, n_kt, attend, 0)
    pieces = []
    for h in range(ATTN_HEADS):
        g = h // rep
        pieces.append(acc_ref[h][:, g * HEAD_DIM:(g + 1) * HEAD_DIM] / l_ref[h])
    o_ref[0] = jnp.concatenate(pieces, axis=1)


def _dsa_prompt(q, qi, wi, k, v, ki, t_real):
    bsz, tp, _ = q.shape
    kt = 640 if tp % 640 == 0 else LANE
    n_sel = min(TOPK_MAX, t_real // 4)
    idx_bits = max(1, (tp - 1).bit_length())
    blk = lambda w: pl.BlockSpec((1, Q_BLOCK, w), lambda b, i: (b, i, 0))
    seq = lambda w: pl.BlockSpec((1, tp, w), lambda b, i: (b, 0, 0))
    return pl.pallas_call(
        functools.partial(_dsa_prompt_kernel, kt, n_sel, idx_bits),
        grid=(bsz, tp // Q_BLOCK),
        in_specs=[blk(ATTN_WIDTH), blk(IDX_HEADS * IDX_DIM), blk(IDX_HEADS),
                  seq(KV_W), seq(KV_W), seq(IDX_DIM)],
        out_specs=blk(ATTN_WIDTH),
        out_shape=jax.ShapeDtypeStruct((bsz, tp, ATTN_WIDTH), F32),
        scratch_shapes=[pltpu.VMEM((Q_BLOCK, tp), I32),
                        pltpu.VMEM((ATTN_HEADS, Q_BLOCK, 1), F32),
                        pltpu.VMEM((ATTN_HEADS, Q_BLOCK, 1), F32),
                        pltpu.VMEM((ATTN_HEADS, Q_BLOCK, KV_W), F32)],
        compiler_params=_cparams("parallel", "arbitrary"),
        name="dsa_prompt",
    )(q, qi, wi, k, v, ki)


def _dsa_s_scores_kernel(pps, pt_ref, qi_ref, wi_ref, *refs):
    ci_refs, o_ref = refs[:pps], refs[pps]
    qi = qi_ref[0]
    w = wi_ref[0] * IDX_HEADS ** -0.5
    for u in range(pps):
        s = _dot3(qi, ci_refs[u][0, 0], _NT)
        sc = jnp.sum(w * jnp.maximum(s * IDX_DIM ** -0.5, 0.0), axis=0, keepdims=True)
        o_ref[0, :, u * PAGE_SIZE:(u + 1) * PAGE_SIZE] = sc


def _dsa_s_scores(page_table, qi3, wi3, cache_idx, layer):
    bd, npages = page_table.shape
    pps = PAGES_PER_STEP
    page = lambda u: pl.BlockSpec((1, 1, PAGE_SIZE, IDX_DIM),
                                  lambda b, p, pt: (layer, pt[b * npages + p * pps + u], 0, 0))
    gs = pltpu.PrefetchScalarGridSpec(
        num_scalar_prefetch=1,
        grid=(bd, npages // pps),
        in_specs=[pl.BlockSpec((1, IDX_HEADS, IDX_DIM), lambda b, p, pt: (b, 0, 0)),
                  pl.BlockSpec((1, IDX_HEADS, 1), lambda b, p, pt: (b, 0, 0))] + [page(u) for u in range(pps)],
        out_specs=pl.BlockSpec((1, 1, pps * PAGE_SIZE), lambda b, p, pt: (b, 0, p)),
    )
    return pl.pallas_call(
        functools.partial(_dsa_s_scores_kernel, pps),
        grid_spec=gs,
        out_shape=jax.ShapeDtypeStruct((bd, 1, npages * PAGE_SIZE), F32),
        compiler_params=_cparams("parallel", "arbitrary"),
        name="dsa_decode_scores",
    )(page_table.reshape(-1), qi3, wi3, *([cache_idx] * pps))


def _dsa_s_bounds_kernel(past, n_sel, idx_bits, sc_ref, qi_ref, kiw_ref, hm_ref, thr_o, j_o, kn_o, key_ref):
    rows = sc_ref.shape[0]
    kiw = kiw_ref[...]
    lane = lax.broadcasted_iota(I32, kiw.shape, 1)
    rolled = pltpu.roll(kiw, IDX_DIM, 1)
    ki2 = jnp.where(lane < IDX_DIM, kiw, rolled)
    w8 = jnp.where(lane < IDX_HEADS, rolled, 0.0) * IDX_HEADS ** -0.5
    prod = qi_ref[...] * jnp.concatenate([ki2] * (IDX_HEADS // 2), axis=1)
    s = _dot3(prod, hm_ref[...])
    new = jnp.sum(w8 * jnp.maximum(s * IDX_DIM ** -0.5, 0.0), axis=-1, keepdims=True)
    knew = _f2key(new)
    key_ref[:, 0:past] = _f2key(sc_ref[...])
    key_ref[:, past:past + LANE] = jnp.where(lane == 0, knew, INT_MIN)
    thr, jmax = _select_bounds(key_ref, (past + LANE) // LANE, rows, LANE, n_sel, idx_bits)
    thr_o[...] = thr
    j_o[...] = jmax
    kn_o[...] = knew


def _dsa_s_bounds(scores, qi, kiw, hm, past):
    rows = scores.shape[0]
    n_sel = min(TOPK_MAX, (past + 1) // 4)
    idx_bits = (past + LANE - 1).bit_length()
    out = jax.ShapeDtypeStruct((rows, 1), I32)
    return pl.pallas_call(
        functools.partial(_dsa_s_bounds_kernel, past, n_sel, idx_bits),
        out_shape=[out, out, out],
        scratch_shapes=[pltpu.VMEM((rows, past + LANE), I32)],
        compiler_params=pltpu.CompilerParams(vmem_limit_bytes=VMEM_LIMIT),
        name="dsa_decode_bounds",
    )(scores, qi, kiw, hm)


def _dsa_s_attend_kernel(pps, past, pt_ref, thr_ref, j_ref, kn_ref, qe_ref, sc_ref, kn_row_ref, vn_row_ref,
                         fold_ref, *refs):
    ck, cv, o_ref = refs[:pps], refs[pps:2 * pps], refs[2 * pps]
    m_ref, l_ref, acc_ref = refs[2 * pps + 1:]
    b = pl.program_id(0)
    p = pl.program_id(1)

    @pl.when(p == 0)
    def _():
        m_ref[...] = jnp.full(m_ref.shape, NEG_BIG, F32)
        l_ref[...] = jnp.zeros(l_ref.shape, F32)
        acc_ref[...] = jnp.zeros(acc_ref.shape, F32)

    thr, jmax = thr_ref[b], j_ref[b]
    qe = qe_ref[0]
    qeb = qe.astype(BF16)
    kidx = p * pps * PAGE_SIZE + lax.broadcasted_iota(I32, (1, pps * PAGE_SIZE), 1)
    kb = jnp.concatenate([ck[u][0, 0].astype(BF16) for u in range(pps)], axis=0)
    vb = jnp.concatenate([cv[u][0, 0].astype(BF16) for u in range(pps)], axis=0)
    lg = lax.dot_general(qeb, kb, _NT, preferred_element_type=F32) * HEAD_DIM ** -0.5
    key = _f2key(sc_ref[0])
    sel = (key > thr) | ((key == thr) & (kidx <= jmax))
    lg = jnp.where(sel, lg, NEG_BIG)
    m = m_ref[...]
    mn = jnp.maximum(m, jnp.max(lg, axis=-1, keepdims=True))
    pr = jnp.where(sel, jnp.exp(lg - mn), 0.0)
    alpha = jnp.exp(m - mn)
    l_ref[...] = alpha * l_ref[...] + jnp.sum(pr, axis=-1, keepdims=True)
    acc_ref[...] = alpha * acc_ref[...] + jnp.dot(pr.astype(BF16), vb, preferred_element_type=F32)
    m_ref[...] = mn

    @pl.when(p == pl.num_programs(1) - 1)
    def _():
        knew = kn_ref[b]
        sel_new = (knew > thr) | ((knew == thr) & (past <= jmax))
        lg = jnp.sum(qe * kn_row_ref[0], axis=-1, keepdims=True) * HEAD_DIM ** -0.5
        m = m_ref[...]
        mn = jnp.where(sel_new, jnp.maximum(m, lg), m)
        pr = jnp.where(sel_new, jnp.exp(lg - mn), 0.0)
        alpha = jnp.exp(m - mn)
        l = alpha * l_ref[...] + pr
        acc = alpha * acc_ref[...] + pr * vn_row_ref[0]
        rowi = lax.broadcasted_iota(I32, acc.shape, 0)
        lanei = lax.broadcasted_iota(I32, acc.shape, 1)
        own = jnp.where((lanei >> 6) == (rowi >> 1), acc / l, 0.0)
        o_ref[0] = _dot_sel(own, fold_ref[...])


def _dsa_s_attend(page_table, thr, jmax, knew, qe, scores, k_new, v_new, fold, cache_k, cache_v, layer):
    bd, npages = page_table.shape
    past = npages * PAGE_SIZE
    pps = PAGES_PER_STEP
    page = lambda u: pl.BlockSpec((1, 1, PAGE_SIZE, KV_W),
                                  lambda b, p, pt, t, j, kn: (layer, pt[b * npages + p * pps + u], 0, 0))
    per_b = lambda s: pl.BlockSpec((1,) + s, lambda b, p, pt, t, j, kn: (b, 0, 0))
    gs = pltpu.PrefetchScalarGridSpec(
        num_scalar_prefetch=4,
        grid=(bd, npages // pps),
        in_specs=[per_b((ATTN_HEADS, KV_W)),
                  pl.BlockSpec((1, 1, pps * PAGE_SIZE), lambda b, p, pt, t, j, kn: (b, 0, p)),
                  per_b((1, KV_W)), per_b((1, KV_W)),
                  pl.BlockSpec(fold.shape, lambda b, p, pt, t, j, kn: (0, 0))]
                 + [page(u) for u in range(pps)] * 2,
        out_specs=per_b((ATTN_HEADS, HEAD_DIM)),
        scratch_shapes=[pltpu.VMEM((ATTN_HEADS, 1), F32), pltpu.VMEM((ATTN_HEADS, 1), F32),
                        pltpu.VMEM((ATTN_HEADS, KV_W), F32)],
    )
    return pl.pallas_call(
        functools.partial(_dsa_s_attend_kernel, pps, past),
        grid_spec=gs,
        out_shape=jax.ShapeDtypeStruct((bd, ATTN_HEADS, HEAD_DIM), F32),
        compiler_params=_cparams("parallel", "arbitrary"),
        name="dsa_decode_attend",
    )(page_table.reshape(-1), thr.reshape(-1), jmax.reshape(-1), knew.reshape(-1),
      qe, scores, k_new, v_new, fold, *([cache_k] * pps), *([cache_v] * pps))


def _merge_ln_kernel(o_ref, bonus_ref, g_ref, at_ref, x_ref, wt_ref, wb_ref, xg_ref, xb_ref,
                     lg_ref, lb_ref, bd_ref, out_ref):
    bd = bd_ref[...]
    o = o_ref[...]
    mean = _dot_sel(o, bd) * (1.0 / HEAD_DIM)
    oc = o - mean
    var = _dot_sel(oc * oc, bd) * (1.0 / HEAD_DIM)
    rw = (oc * lax.rsqrt(var + GN_EPS) * xg_ref[...] + xb_ref[...] + bonus_ref[...]) * g_ref[...]
    f = (jnp.dot(rw.astype(BF16), wt_ref[...], preferred_element_type=F32)
         + jnp.dot(at_ref[...].astype(BF16), wb_ref[...], preferred_element_type=F32))
    out_ref[...] = _layer_norm(DEEPNORM_ALPHA * x_ref[...] + f, lg_ref[...], lb_ref[...])


def _merge_ln(o, bonus, g, at, x, wts, tm):
    n = x.shape[0]
    tm = min(tm, n)
    tok = lambda w: pl.BlockSpec((tm, w), lambda i: (i, 0))
    full = lambda a: pl.BlockSpec(a.shape, lambda i: (0, 0))
    ws = [wts["wo_top"], wts["wo_bot"], wts["lnx_g"], wts["lnx_b"], wts["ln1_g"], wts["ln1_b"], wts["bd"]]
    return pl.pallas_call(
        _merge_ln_kernel,
        grid=(n // tm,),
        in_specs=[tok(RWKV_WIDTH)] * 3 + [tok(ATTN_WIDTH), tok(D_MODEL)] + [full(a) for a in ws],
        out_specs=tok(D_MODEL),
        out_shape=jax.ShapeDtypeStruct((n, D_MODEL), F32),
        compiler_params=_cparams("parallel"),
        name="merge_ln",
    )(o, bonus, g, at, x, *ws)


def _take_top(src_ref, n_rows, val_ref, idx_ref):
    shape = src_ref.shape
    row = lax.broadcasted_iota(I32, shape, 1)

    def body(a, carry):
        sv = src_ref[...]
        m = jnp.max(sv, axis=1, keepdims=True)
        idx = jnp.min(jnp.where(sv == m, row, n_rows), axis=1, keepdims=True)
        val_ref[:, pl.ds(a, 1), :] = m
        idx_ref[:, pl.ds(a, 1), :] = idx
        src_ref[...] = jnp.where(row == idx, -jnp.inf, sv)
        return carry

    lax.fori_loop(0, PEER_TOPK, body, 0)


def _peer_route_kernel(x_ref, wq_ref, sk0_ref, sk1_ref, e_o, g_o, s_ref, t_ref, i_ref, c_ref, ts_ref, ic_ref):
    q = jnp.dot(x_ref[...].astype(BF16), wq_ref[...], preferred_element_type=F32)
    for h in range(PEER_HEADS):
        qh = q[:, h * PEER_DKEY:(h + 1) * PEER_DKEY]
        s_ref[h] = _dot3(sk0_ref[...], qh, _NT)
        s_ref[PEER_HEADS + h] = _dot3(sk1_ref[...], qh, _NT)
    _take_top(s_ref, N_KEYS, t_ref, i_ref)
    t1, t2 = t_ref[0:PEER_HEADS], t_ref[PEER_HEADS:2 * PEER_HEADS]
    c_ref[...] = jnp.concatenate([t1[:, a:a + 1, :] + t2 for a in range(PEER_TOPK)], axis=1)
    _take_top(c_ref, PEER_TOPK * PEER_TOPK, ts_ref, ic_ref)
    ic = ic_ref[...]
    i1, i2 = i_ref[0:PEER_HEADS], i_ref[PEER_HEADS:2 * PEER_HEADS]
    ia, ib = ic >> 4, ic & (PEER_TOPK - 1)
    e = jnp.zeros(ic.shape, I32)
    for a in range(PEER_TOPK):
        e = e + jnp.where(ia == a, i1[:, a:a + 1, :] * N_KEYS, 0) + jnp.where(ib == a, i2[:, a:a + 1, :], 0)
    ts = ts_ref[...]
    ex = jnp.exp(ts - jnp.max(ts, axis=1, keepdims=True))
    e_o[0] = e
    g_o[0] = ex / jnp.sum(ex, axis=1, keepdims=True)


def _peer_route(x, wq, sk0, sk1, tt):
    n = x.shape[0]
    nt = n // tt
    full = lambda a: pl.BlockSpec(a.shape, lambda i: (0, 0))
    out = pl.BlockSpec((1, PEER_HEADS, PEER_TOPK, tt), lambda i: (i, 0, 0, 0))
    return pl.pallas_call(
        _peer_route_kernel,
        grid=(nt,),
        in_specs=[pl.BlockSpec((tt, D_MODEL), lambda i: (i, 0)), full(wq), full(sk0), full(sk1)],
        out_specs=[out, out],
        out_shape=[jax.ShapeDtypeStruct((nt, PEER_HEADS, PEER_TOPK, tt), I32),
                   jax.ShapeDtypeStruct((nt, PEER_HEADS, PEER_TOPK, tt), F32)],
        scratch_shapes=[pltpu.VMEM((2 * PEER_HEADS, N_KEYS, tt), F32),
                        pltpu.VMEM((2 * PEER_HEADS, PEER_TOPK, tt), F32),
                        pltpu.VMEM((2 * PEER_HEADS, PEER_TOPK, tt), I32),
                        pltpu.VMEM((PEER_HEADS, PEER_TOPK * PEER_TOPK, tt), F32),
                        pltpu.VMEM((PEER_HEADS, PEER_TOPK, tt), F32),
                        pltpu.VMEM((PEER_HEADS, PEER_TOPK, tt), I32)],
        compiler_params=_cparams("parallel"),
        name="peer_route",
    )(x, wq, sk0, sk1)


def _peer_gather_kernel(tt, n_valid, idx_hbm, x_ref, g_ref, uv_hbm, lg_ref, lb_ref, o_ref,
                        idx_smem, buf0, buf1, buf2, sem, isem, y_ref):
    i = pl.program_id(0)
    n_idx = PEER_SLOTS * tt
    islot = i % 2
    bufs = (buf0, buf1, buf2)

    def idx_copy(tile, slot):
        return pltpu.make_async_copy(idx_hbm.at[tile], idx_smem.at[pl.ds(slot * n_idx, n_idx)], isem.at[slot])

    @pl.when(i == 0)
    def _():
        idx_copy(0, 0).start()

    if n_valid < tt:
        y_ref[...] = jnp.zeros(y_ref.shape, F32)
    idx_copy(i, islot).wait()

    @pl.when(i + 1 < pl.num_programs(0))
    def _():
        idx_copy(i + 1, 1 - islot).start()

    base = islot * n_idx

    def issue(t, k):
        for s in range(PEER_SLOTS):
            e = idx_smem[base + s * tt + t]
            pltpu.async_copy(uv_hbm.at[pl.ds(e, 1)], bufs[k].at[pl.ds(s, 1)], sem.at[k], priority=s % 2)

    def wait(k):
        pltpu.make_async_copy(uv_hbm.at[pl.ds(0, PEER_SLOTS)], bufs[k], sem.at[k]).wait()

    lane = lax.broadcasted_iota(I32, (PEER_SLOTS, tt), 1)

    def compute(t, k):
        xrow = x_ref[pl.ds(t, 1), :]
        word = bufs[k][...]
        u = lax.bitcast_convert_type(word << 16, F32)
        v = lax.bitcast_convert_type(word & jnp.int32(-65536), F32)
        h = jnp.sum(u * xrow, axis=-1, keepdims=True)
        gate = jnp.sum(jnp.where(lane == t, g_ref[0], 0.0), axis=-1, keepdims=True)
        coef = gate * jax.nn.gelu(h)
        y_ref[pl.ds(t, 1), :] = jnp.sum(coef * v, axis=0, keepdims=True)

    depth = len(bufs)
    ahead = depth - 1
    n_main = n_valid - ahead
    assert n_main >= 0 and n_main % depth == 0
    for t in range(ahead):
        issue(t, t)

    def body(i3, carry):
        for k in range(depth):
            t = i3 * depth + k
            wait(k)
            issue(t + ahead, (k + ahead) % depth)
            compute(t, k)
        return carry

    lax.fori_loop(0, n_main // depth, body, 0)
    for t in range(n_main, n_valid):
        wait(t % depth)
        compute(t, t % depth)
    o_ref[...] = _layer_norm(DEEPNORM_ALPHA * x_ref[...] + y_ref[...], lg_ref[...], lb_ref[...])


def _peer_gather(idx, gates, x, uv, ln_g, ln_b, tt, n_valid):
    n = x.shape[0]
    nt = n // tt
    row = pl.BlockSpec((1, D_MODEL), lambda i: (0, 0))
    return pl.pallas_call(
        functools.partial(_peer_gather_kernel, tt, n_valid),
        grid=(nt,),
        in_specs=[pl.BlockSpec(memory_space=pl.ANY),
                  pl.BlockSpec((tt, D_MODEL), lambda i: (i, 0)),
                  pl.BlockSpec((1, PEER_SLOTS, tt), lambda i: (i, 0, 0)),
                  pl.BlockSpec(memory_space=pl.ANY), row, row],
        out_specs=pl.BlockSpec((tt, D_MODEL), lambda i: (i, 0)),
        out_shape=jax.ShapeDtypeStruct((n, D_MODEL), F32),
        scratch_shapes=[pltpu.SMEM((2 * PEER_SLOTS * tt,), I32)]
                       + [pltpu.VMEM((PEER_SLOTS, D_MODEL), I32)] * 3
                       + [pltpu.SemaphoreType.DMA((3,)),
                          pltpu.SemaphoreType.DMA((2,)),
                          pltpu.VMEM((tt, D_MODEL), F32)],
        compiler_params=_cparams("arbitrary"),
        name="peer_gather",
    )(idx, x, gates, uv, ln_g, ln_b)


def _peer(x, wts, tt, n_valid):
    e, gates = _peer_route(x, wts["wq"], wts["sk0"], wts["sk1"], tt)
    nt = x.shape[0] // tt
    return _peer_gather(e.reshape(nt, PEER_SLOTS * tt), gates.reshape(nt, PEER_SLOTS, tt), x,
                        wts["uv"], wts["ln2_g"], wts["ln2_b"], tt, n_valid)


def _pack_bf16_pair(lo, hi):
    bits = lambda a: lax.bitcast_convert_type(a.astype(BF16), jnp.uint16).astype(jnp.uint32)
    return lax.bitcast_convert_type(bits(lo) | (bits(hi) << 16), I32)


def _layer_weights(l, w_in, shift_mu, decay_w0, decay_up, iclr_a0, iclr_up, gate_up, k_k, k_a, r_k,
                   lnx_g, lnx_b, idx_ln_g, idx_ln_b, w_out, ln1_g, ln1_b, ln2_g, ln2_b,
                   peer_wq, peer_subkeys, peer_u, peer_v):
    row = lambda a: a.reshape(1, -1).astype(F32)
    lora = jnp.zeros((LORA_W, 3 * RWKV_WIDTH), F32)
    lora = lora.at[0:W_LORA, 0:RWKV_WIDTH].set(decay_up[l])
    lora = lora.at[W_LORA:W_LORA + A_LORA, RWKV_WIDTH:2 * RWKV_WIDTH].set(iclr_up[l])
    lora = lora.at[W_LORA + A_LORA:, 2 * RWKV_WIDTH:].set(gate_up[l])
    lora_hi = lora.astype(BF16)
    pad_lane = lambda a: jnp.pad(a.reshape(1, -1), ((0, 0), (0, LANE - a.size)))
    half = PEER_DKEY // 2
    seg = jnp.arange(RWKV_WIDTH) // HEAD_DIM
    return dict(
        w_rw=w_in[l][:, :SHIFT_W].astype(BF16),
        w_at=jnp.pad(w_in[l][:, SHIFT_W:], ((0, 0), (0, AT_PAD_W - ATTN_PROJ_W))).astype(BF16),
        mu=row(shift_mu[l]), w0=row(decay_w0[l]), a0=row(iclr_a0[l]),
        lora_hi=lora_hi, lora_lo=(lora - lora_hi.astype(F32)).astype(BF16),
        k_k=row(k_k[l]), k_a=row(k_a[l]), r_k=row(r_k[l]),
        bd=(seg[:, None] == seg[None, :]).astype(BF16),
        lnx_g=row(lnx_g[l]), lnx_b=row(lnx_b[l]),
        idx_g=pad_lane(idx_ln_g[l]), idx_b=pad_lane(idx_ln_b[l]),
        wo_top=w_out[l][:RWKV_WIDTH].astype(BF16), wo_bot=w_out[l][RWKV_WIDTH:].astype(BF16),
        ln1_g=row(ln1_g[l]), ln1_b=row(ln1_b[l]), ln2_g=row(ln2_g[l]), ln2_b=row(ln2_b[l]),
        wq=peer_wq[l].astype(BF16),
        sk0=jnp.pad(peer_subkeys[l, 0], ((0, 0), (0, half))),
        sk1=jnp.pad(peer_subkeys[l, 1], ((0, 0), (half, 0))),
        uv=_pack_bf16_pair(peer_u[l], peer_v[l]),
    )


def _tile(n, cap):
    if n <= cap:
        return n
    return max(d for d in range(8, cap + 1, 8) if n % d == 0)


def _mixer_front(x, wts, tabs):
    tm = _tile(x.shape[0], 512)
    p_rw = _matmul(x, wts["w_rw"], tm)
    p_at = _matmul(x, wts["w_at"], tm)
    q, k, qi, kiw = _attn_prep(p_at, tabs, wts["idx_g"], wts["idx_b"], _tile(tabs[0].shape[0], 640))
    v = p_at[:, ATTN_WIDTH + KV_W:ATTN_WIDTH + 2 * KV_W]
    return p_rw, q, k, v, qi, kiw


def kernel(x_prompt, x_sample, cache_k, cache_v, cache_idx_k, state_wkv, state_shift, page_table, meta_tokens, w_in, shift_mu, decay_w0, decay_up, iclr_a0, iclr_up, gate_up, k_k, k_a, r_k, lnx_g, lnx_b, idx_ln_g, idx_ln_b, w_out, ln1_g, ln1_b, ln2_g, ln2_b, peer_wq, peer_subkeys, peer_u, peer_v):
    bsz, seq, _ = x_prompt.shape
    t = seq + N_META
    tp = -(-t // LANE) * LANE
    n_p = bsz * tp
    depth = w_in.shape[0]
    bd_, ts_, _ = x_sample.shape
    assert ts_ == 1
    npages = page_table.shape[1]
    past = npages * PAGE_SIZE
    n_pool = cache_k.shape[1]
    peer_tt = LANE
    ns_pad = -(-bd_ // peer_tt) * peer_tt

    xp = jnp.concatenate([jnp.broadcast_to(meta_tokens[None], (bsz, N_META, D_MODEL)), x_prompt], axis=1)
    xp = jnp.pad(xp, ((0, 0), (0, tp - t), (0, 0))).reshape(n_p, D_MODEL)
    xs = x_sample.reshape(bd_, D_MODEL)
    tabs_p = _rope_tables(jnp.arange(tp, dtype=I32))
    tabs_s = _rope_tables(jnp.full((bd_,), past, I32))
    ck = cache_k.reshape(depth, n_pool, PAGE_SIZE, KV_W)
    cv = cache_v.reshape(depth, n_pool, PAGE_SIZE, KV_W)
    hsel = jnp.arange(IDX_HEADS * IDX_DIM)[:, None] // IDX_DIM == jnp.arange(LANE)[None, :]
    hm = hsel.astype(F32)
    fold = (jnp.arange(KV_W)[:, None] % HEAD_DIM == jnp.arange(HEAD_DIM)[None, :]).astype(BF16)
    own = (jnp.arange(KV_W)[None, :] // HEAD_DIM == jnp.arange(ATTN_HEADS)[:, None] // (ATTN_HEADS // KV_HEADS))

    k_p, v_p, ki_p, wkv_p, sh_p = [], [], [], [], []
    k_s, v_s, ki_s, wkv_s, sh_s = [], [], [], [], []
    for l in range(depth):
        wts = _layer_weights(l, w_in, shift_mu, decay_w0, decay_up, iclr_a0, iclr_up, gate_up, k_k, k_a, r_k,
                             lnx_g, lnx_b, idx_ln_g, idx_ln_b, w_out, ln1_g, ln1_b, ln2_g, ln2_b,
                             peer_wq, peer_subkeys, peer_u, peer_v)

        p_rw, q, k, v, qi, kiw = _mixer_front(xp, wts, tabs_p)
        p_rw3 = p_rw.reshape(bsz, tp, SHIFT_W)
        r_, w_, kt_, kk_, b_, vv_, bonus, g = _rwkv_prep(
            p_rw3, jnp.zeros((bsz, 1, SHIFT_W), F32), t, True, wts, LANE)
        o, s_fin = _wkv(r_, w_, kt_, kk_, b_, vv_,
                        jnp.zeros((bsz, RWKV_HEADS, HEAD_DIM, HEAD_DIM), F32), bsz, LANE)
        three = lambda a: a.reshape(bsz, tp, -1)
        at = _dsa_prompt(three(q).astype(BF16), three(qi).astype(BF16),
                         three(kiw)[:, :, IDX_DIM:IDX_DIM + IDX_HEADS],
                         three(k).astype(BF16), three(v).astype(BF16),
                         three(kiw)[:, :, :IDX_DIM].astype(BF16), t)
        flat = lambda a: a.reshape(n_p, -1)
        x1 = _merge_ln(flat(o), flat(bonus), flat(g), flat(at), xp, wts, _tile(n_p, 256))
        xp = _peer(x1, wts, peer_tt, peer_tt)
        k_p.append(three(k)[:, :t].reshape(bsz, t, KV_HEADS, HEAD_DIM))
        v_p.append(three(v)[:, :t].reshape(bsz, t, KV_HEADS, HEAD_DIM))
        ki_p.append(three(kiw)[:, :t, :IDX_DIM])
        wkv_p.append(s_fin)
        sh_p.append(p_rw3[:, t - 1])

        p_rw, q, k, v, qi, kiw = _mixer_front(xs, wts, tabs_s)
        r_, w_, kt_, kk_, b_, vv_, bonus, g = _rwkv_prep(
            p_rw[None], state_shift[l][None], bd_, False, wts, bd_)
        tc_s = 8
        padt = lambda a, c: jnp.pad(a[0][:, None, :], ((0, 0), (0, tc_s - 1), (0, 0)), constant_values=c)
        o, s_fin = _wkv(padt(r_, 0.0), padt(w_, 1.0), padt(kt_, 0.0), padt(kk_, 0.0), padt(b_, 0.0),
                        padt(vv_, 0.0), state_wkv[l].astype(F32), 4, tc_s)
        o = o[:, 0]
        scores = _dsa_s_scores(page_table, qi.reshape(bd_, IDX_HEADS, IDX_DIM),
                               kiw[:, IDX_DIM:IDX_DIM + IDX_HEADS].reshape(bd_, IDX_HEADS, 1),
                               cache_idx_k, l)
        thr, jmax, knew = _dsa_s_bounds(scores.reshape(bd_, past), qi, kiw, hm, past)
        qe = jnp.where(own[None], jnp.tile(q.reshape(bd_, ATTN_HEADS, HEAD_DIM), (1, 1, KV_HEADS)), 0.0)
        at = _dsa_s_attend(page_table, thr, jmax, knew, qe, scores, k[:, None, :], v[:, None, :], fold,
                           ck, cv, l).reshape(bd_, ATTN_WIDTH)
        x1 = _merge_ln(o, bonus[0], g[0], at, xs, wts, bd_)
        x1p = jnp.pad(x1, ((0, ns_pad - bd_), (0, 0)))
        xs = _peer(x1p, wts, peer_tt, bd_)[:bd_]
        k_s.append(k.reshape(bd_, 1, KV_HEADS, HEAD_DIM))
        v_s.append(v.reshape(bd_, 1, KV_HEADS, HEAD_DIM))
        ki_s.append(kiw[:, None, :IDX_DIM])
        wkv_s.append(s_fin)
        sh_s.append(p_rw)

    y_prompt = xp.reshape(bsz, tp, D_MODEL)[:, N_META:t]
    y_sample = xs.reshape(bd_, 1, D_MODEL)
    return (y_prompt, y_sample, jnp.stack(k_p), jnp.stack(v_p), jnp.stack(ki_p),
            jnp.stack(wkv_p).astype(state_wkv.dtype), jnp.stack(sh_p).astype(state_shift.dtype),
            jnp.stack(k_s), jnp.stack(v_s), jnp.stack(ki_s),
            jnp.stack(wkv_s).astype(state_wkv.dtype), jnp.stack(sh_s).astype(state_shift.dtype))
```

```python
import functools
import math

import jax
import jax.numpy as jnp
from jax import lax
from jax.experimental import pallas as pl
from jax.experimental.pallas import tpu as pltpu

F32 = jnp.float32
BF16 = jnp.bfloat16
I32 = jnp.int32

D_MODEL = 1024
N_META = 16
HEAD_DIM = 64
RWKV_WIDTH = D_MODEL // 2
RWKV_HEADS = RWKV_WIDTH // HEAD_DIM
ATTN_WIDTH = D_MODEL - RWKV_WIDTH
ATTN_HEADS = ATTN_WIDTH // HEAD_DIM
KV_HEADS = ATTN_HEADS // 2
KV_W = KV_HEADS * HEAD_DIM
W_LORA = 64
A_LORA = 64
G_LORA = 128
LORA_W = W_LORA + A_LORA + G_LORA
SHIFT_W = 3 * RWKV_WIDTH + LORA_W
IDX_HEADS = 8
IDX_DIM = 64
ATTN_PROJ_W = ATTN_WIDTH + 2 * KV_W + IDX_HEADS * IDX_DIM + IDX_DIM + IDX_HEADS
TOPK_MAX = 256
ROPE_THETA = 500000.0
ROT = HEAD_DIM // 4
ROT_HALF = ROT // 2
PEER_HEADS = 8
PEER_DKEY = 128
N_KEYS = 128
PEER_TOPK = 16
PEER_SLOTS = PEER_HEADS * PEER_TOPK
DEPTH = 2
DEEPNORM_ALPHA = (2.0 * DEPTH) ** 0.25
PAGE_SIZE = 128
LN_EPS = 1e-5
GN_EPS = 64e-5

LANE = 128
SUBLANE = 8
Q_BLOCK = 128
INT_MIN = -(2 ** 31)
NEG_BIG = -1e30
VMEM_LIMIT = 56 * 1024 * 1024
AT_PAD_W = 1664
KI_OFF = ATTN_WIDTH + 2 * KV_W + IDX_HEADS * IDX_DIM
PAGES_PER_STEP = 8
GATHER_BUFFERS = 6

_NT = (((1,), (1,)), ((), ()))


def _cparams(*sem):
    return pltpu.CompilerParams(dimension_semantics=sem, vmem_limit_bytes=VMEM_LIMIT)


def _split2(x):
    hi = x.astype(BF16)
    lo = (x - hi.astype(F32)).astype(BF16)
    return hi, lo


def _split3(x):
    hi = x.astype(BF16)
    r1 = x - hi.astype(F32)
    mid = r1.astype(BF16)
    lo = (r1 - mid.astype(F32)).astype(BF16)
    return hi, mid, lo


def _dot3(a, b, dims=None):
    ah, al = _split2(a)
    bh, bl = _split2(b)
    if dims is None:
        d = lambda p, q: jnp.dot(p, q, preferred_element_type=F32)
    else:
        d = lambda p, q: lax.dot_general(p, q, dims, preferred_element_type=F32)
    return d(ah, bh) + d(al, bh) + d(ah, bl)


def _dot_sel(x, m):
    h, mid, lo = _split3(x)
    d = lambda p: jnp.dot(p, m, preferred_element_type=F32)
    return d(h) + d(mid) + d(lo)


def _f2key(x):
    x = jnp.where(x == 0.0, 0.0, x)
    b = lax.bitcast_convert_type(x, I32)
    return b ^ ((b >> 31) & 0x7FFFFFFF)


def _layer_norm(z, g, b):
    mu = jnp.mean(z, axis=-1, keepdims=True)
    zc = z - mu
    var = jnp.mean(zc * zc, axis=-1, keepdims=True)
    return zc * lax.rsqrt(var + LN_EPS) * g + b


def _mm_kernel(x_ref, w_ref, o_ref):
    o_ref[...] = jnp.dot(x_ref[...].astype(BF16), w_ref[...], preferred_element_type=F32)


def _matmul(x, w, tm):
    m, k = x.shape
    n = w.shape[1]
    tm = min(tm, m)
    return pl.pallas_call(
        _mm_kernel,
        grid=(m // tm,),
        in_specs=[pl.BlockSpec((tm, k), lambda i: (i, 0)),
                  pl.BlockSpec((k, n), lambda i: (0, 0))],
        out_specs=pl.BlockSpec((tm, n), lambda i: (i, 0)),
        out_shape=jax.ShapeDtypeStruct((m, n), F32),
        compiler_params=_cparams("parallel"),
        name="proj_matmul",
    )(x, w)


def _rwkv_prep_kernel(t_real, tt, shift, p_ref, prev_ref, mu_ref, w0_ref, a0_ref, lwh_ref, lwl_ref,
                      kk_ref, ka_ref, rk_ref, bd_ref,
                      r_o, w_o, kt_o, kko_o, b_o, v_o, bonus_o, g_o, carry_ref):
    j = pl.program_id(1)
    pf = p_ref[0]
    if shift:
        @pl.when(j == 0)
        def _():
            carry_ref[...] = prev_ref[0]
        row = lax.broadcasted_iota(I32, pf.shape, 0)
        prev = jnp.where(row == 0, carry_ref[...], pltpu.roll(pf, 1, 0))
        carry_ref[...] = pf[tt - 1:tt, :]
    else:
        prev = prev_ref[0]
    xs = pf + mu_ref[...] * (prev - pf)
    r = xs[:, 0:RWKV_WIDTH]
    k = xs[:, RWKV_WIDTH:2 * RWKV_WIDTH]
    v = xs[:, 2 * RWKV_WIDTH:3 * RWKV_WIDTH]
    z = xs[:, 3 * RWKV_WIDTH:SHIFT_W]
    lane = lax.broadcasted_iota(I32, z.shape, 1)
    zt = jnp.where(lane < W_LORA, jnp.tanh(z),
                   jnp.where(lane < W_LORA + A_LORA, z, jax.nn.sigmoid(z)))
    zh, zl = _split2(zt)
    d = lambda p, q: jnp.dot(p, q, preferred_element_type=F32)
    lo = d(zh, lwh_ref[...]) + d(zl, lwh_ref[...]) + d(zh, lwl_ref[...])
    w_raw = w0_ref[...] + lo[:, 0:RWKV_WIDTH]
    a = jax.nn.sigmoid(a0_ref[...] + lo[:, RWKV_WIDTH:2 * RWKV_WIDTH])
    g = lo[:, 2 * RWKV_WIDTH:3 * RWKV_WIDTH]
    decay = jnp.exp(-math.exp(-0.5) * jax.nn.sigmoid(w_raw))
    bd = bd_ref[...]
    kk = k * kk_ref[...]
    kk = kk / jnp.maximum(jnp.sqrt(_dot_sel(kk * kk, bd)), 1e-12)
    kt = k * (1.0 + (a - 1.0) * ka_ref[...])
    bonus = _dot_sel(r * kt * rk_ref[...], bd) * v
    pos = j * tt + lax.broadcasted_iota(I32, r.shape, 0)
    valid = pos < t_real
    r_o[0] = r
    w_o[0] = jnp.where(valid, decay, 1.0)
    kt_o[0] = jnp.where(valid, kt, 0.0)
    kko_o[0] = jnp.where(valid, kk, 0.0)
    b_o[0] = jnp.where(valid, kk * a, 0.0)
    v_o[0] = v
    bonus_o[0] = bonus
    g_o[0] = g


def _rwkv_prep(p_rw, prev, t_real, shift, wts, tt):
    bsz, tp, _ = p_rw.shape
    tt = min(tt, tp)
    row = lambda n: pl.BlockSpec((1, n), lambda b, j: (0, 0))
    full = lambda a: pl.BlockSpec(a.shape, lambda b, j: (0, 0))
    tok = lambda n: pl.BlockSpec((1, tt, n), lambda b, j: (b, j, 0))
    prev_spec = pl.BlockSpec((1, 1, SHIFT_W), lambda b, j: (b, 0, 0)) if shift else tok(SHIFT_W)
    outs = pl.pallas_call(
        functools.partial(_rwkv_prep_kernel, t_real, tt, shift),
        grid=(bsz, tp // tt),
        in_specs=[tok(SHIFT_W), prev_spec, row(SHIFT_W), row(RWKV_WIDTH), row(RWKV_WIDTH),
                  full(wts["lora_hi"]), full(wts["lora_lo"]),
                  row(RWKV_WIDTH), row(RWKV_WIDTH), row(RWKV_WIDTH), full(wts["bd"])],
        out_specs=[tok(RWKV_WIDTH)] * 8,
        out_shape=[jax.ShapeDtypeStruct((bsz, tp, RWKV_WIDTH), F32)] * 8,
        scratch_shapes=[pltpu.VMEM((1, SHIFT_W), F32)],
        compiler_params=_cparams("parallel", "arbitrary"),
        name="rwkv_prep",
    )(p_rw, prev, wts["mu"], wts["w0"], wts["a0"], wts["lora_hi"], wts["lora_lo"],
      wts["k_k"], wts["k_a"], wts["r_k"], wts["bd"])
    return outs


def _wkv_kernel(bb, tc, r_ref, w_ref, kt_ref, kk_ref, b_ref, v_ref, s0_ref, o_ref, sf_ref, s_ref):
    c = pl.program_id(1)

    @pl.when(c == 0)
    def _():
        s_ref[...] = s0_ref[...]

    lane = lax.broadcasted_iota(I32, (HEAD_DIM, LANE), 1)
    row = lax.broadcasted_iota(I32, (HEAD_DIM, LANE), 0)
    lo = lane < HEAD_DIM
    e0 = lane == row
    e1 = lane == row + HEAD_DIM
    e01 = e0 | e1
    r128 = lax.broadcasted_iota(I32, (LANE, LANE), 0)
    l128 = lax.broadcasted_iota(I32, (LANE, LANE), 1)
    half_ones = ((r128 >> 6) == (l128 >> 6)).astype(BF16)
    npair = RWKV_HEADS // 2

    def half_sums_mxu(parts, n_split):
        res = jnp.dot(jnp.concatenate(parts, axis=0), half_ones, preferred_element_type=F32)
        out = []
        for i in range(len(parts) // n_split):
            acc = res[i * n_split * HEAD_DIM:(i * n_split + 1) * HEAD_DIM]
            for p in range(1, n_split):
                acc = acc + res[(i * n_split + p) * HEAD_DIM:(i * n_split + p + 1) * HEAD_DIM]
            out.append(acc)
        return out

    def group(gi, carry):
        t0 = pl.multiple_of(gi * SUBLANE, SUBLANE)
        rows = pl.ds(t0, SUBLANE)
        blk = lambda ref, b: [ref[b, rows, j * LANE:(j + 1) * LANE] for j in range(npair)]

        def v_pieces(b):
            out = []
            for v8 in blk(v_ref, b):
                vh = v8.astype(BF16).astype(F32)
                r1 = v8 - vh
                vm = r1.astype(BF16).astype(F32)
                out.append((vh, vm, r1 - vm))
            return out

        def v_columns(pieces, u):
            parts = []
            for j in range(npair):
                parts += [jnp.where(e01, pc[u:u + 1], 0.0).astype(BF16) for pc in pieces[j]]
            return half_sums_mxu(parts, 3)

        def out_rows(qparts):
            return [jnp.sum(jnp.where(e01, oc, 0.0), axis=0, keepdims=True) for oc in half_sums_mxu(qparts, 2)]

        pieces = v_pieces(0)
        vcols = [v_columns(pieces, u) for u in range(SUBLANE)]
        pending = None
        for b in range(bb + 1):
            if b < bb:
                kk8, w8, b8, kt8, r8 = blk(kk_ref, b), blk(w_ref, b), blk(b_ref, b), blk(kt_ref, b), blk(r_ref, b)
                st = [s_ref[b, j] for j in range(npair)]
                nxt_pieces = v_pieces(b + 1) if b + 1 < bb else None
            nxt_vcols, qsteps, orows = [], [], []
            for u in range(SUBLANE):
                if b < bb:
                    prods = [st[j] * kk8[j][u:u + 1] for j in range(npair)]
                    sums = [(jnp.sum(jnp.where(lo, p, 0.0), axis=-1, keepdims=True),
                             jnp.sum(jnp.where(lo, 0.0, p), axis=-1, keepdims=True)) for p in prods]
                if nxt_pieces is not None:
                    nxt_vcols.append(v_columns(nxt_pieces, u))
                if pending is not None:
                    orows.append(out_rows(pending[1][u]))
                if b < bb:
                    qs = []
                    for j in range(npair):
                        skk = jnp.where(lo, sums[j][0], sums[j][1])
                        s = st[j] * w8[j][u:u + 1] - skk * b8[j][u:u + 1] + vcols[u][j] * kt8[j][u:u + 1]
                        st[j] = s
                        q = s * r8[j][u:u + 1]
                        qh = q.astype(BF16)
                        qs += [qh, (q - qh.astype(F32)).astype(BF16)]
                    qsteps.append(qs)
            if pending is not None:
                pb = pending[0]
                for j in range(npair):
                    o_ref[pb, rows, j * LANE:(j + 1) * LANE] = jnp.concatenate([orows[u][j] for u in range(SUBLANE)],
                                                                               axis=0)
            if b < bb:
                for j in range(npair):
                    s_ref[b, j] = st[j]
                pending = (b, qsteps)
                vcols = nxt_vcols
                nxt_pieces = None
        return carry

    lax.fori_loop(0, tc // SUBLANE, group, 0)

    @pl.when(c == pl.num_programs(1) - 1)
    def _():
        sf_ref[...] = s_ref[...]


def _pair_state(s):
    b = s.shape[0]
    return (s.reshape(b, RWKV_HEADS // 2, 2, HEAD_DIM, HEAD_DIM)
            .transpose(0, 1, 3, 2, 4).reshape(b, RWKV_HEADS // 2, HEAD_DIM, LANE))


def _unpair_state(s):
    b = s.shape[0]
    return (s.reshape(b, RWKV_HEADS // 2, HEAD_DIM, 2, HEAD_DIM)
            .transpose(0, 1, 3, 2, 4).reshape(b, RWKV_HEADS, HEAD_DIM, HEAD_DIM))


def _wkv(r, w, kt, kk, bv, v, s0, bb, tc):
    bsz, tp, _ = r.shape
    tc = min(tc, tp)
    tok = pl.BlockSpec((bb, tc, RWKV_WIDTH), lambda i, c: (i, c, 0))
    st = pl.BlockSpec((bb, RWKV_HEADS // 2, HEAD_DIM, LANE), lambda i, c: (i, 0, 0, 0))
    o, sf = pl.pallas_call(
        functools.partial(_wkv_kernel, bb, tc),
        grid=(bsz // bb, tp // tc),
        in_specs=[tok] * 6 + [st],
        out_specs=[tok, st],
        out_shape=[jax.ShapeDtypeStruct((bsz, tp, RWKV_WIDTH), F32),
                   jax.ShapeDtypeStruct((bsz, RWKV_HEADS // 2, HEAD_DIM, LANE), F32)],
        scratch_shapes=[pltpu.VMEM((bb, RWKV_HEADS // 2, HEAD_DIM, LANE), F32)],
        compiler_params=_cparams("parallel", "arbitrary"),
        name="wkv_scan",
    )(r, w, kt, kk, bv, v, _pair_state(s0))
    return o, _unpair_state(sf)


def _rope(x, c, sa, sb):
    w = x.shape[1]
    return x * c + pltpu.roll(x, w - ROT_HALF, 1) * sa + pltpu.roll(x, ROT_HALF, 1) * sb


def _attn_prep_kernel(p_ref, c_ref, sa_ref, sb_ref, g_ref, b_ref, q_o, k_o, qi_o, ki_o):
    c1, sa1, sb1 = c_ref[...], sa_ref[...], sb_ref[...]
    rep = lambda t, n: jnp.concatenate([t] * n, axis=1)
    nq = ATTN_WIDTH // LANE
    nk = KV_W // LANE
    q_o[...] = _rope(p_ref[:, 0:ATTN_WIDTH], rep(c1, nq), rep(sa1, nq), rep(sb1, nq))
    k_o[...] = _rope(p_ref[:, ATTN_WIDTH:ATTN_WIDTH + KV_W], rep(c1, nk), rep(sa1, nk), rep(sb1, nk))
    qi0 = ATTN_WIDTH + 2 * KV_W
    qi_o[...] = _rope(p_ref[:, qi0:qi0 + IDX_HEADS * IDX_DIM], rep(c1, nq), rep(sa1, nq), rep(sb1, nq))
    x = p_ref[:, KI_OFF:KI_OFF + LANE]
    lane = lax.broadcasted_iota(I32, x.shape, 1)
    isk = lane < IDX_DIM
    mu = jnp.sum(jnp.where(isk, x, 0.0), axis=-1, keepdims=True) * (1.0 / IDX_DIM)
    xc = jnp.where(isk, x - mu, 0.0)
    var = jnp.sum(xc * xc, axis=-1, keepdims=True) * (1.0 / IDX_DIM)
    y = xc * lax.rsqrt(var + LN_EPS) * g_ref[...] + b_ref[...]
    y = _rope(y, jnp.where(isk, c1, 1.0), jnp.where(isk, sa1, 0.0), jnp.where(isk, sb1, 0.0))
    ki_o[...] = jnp.where(isk, y, x)


def _attn_prep(p_at, tabs, idx_g, idx_b, tm):
    n = p_at.shape[0]
    tm = min(tm, n)
    tpb = tabs[0].shape[0] // tm
    tok = lambda w: pl.BlockSpec((tm, w), lambda i: (i, 0))
    tab = pl.BlockSpec((tm, LANE), lambda i: (i % tpb, 0))
    row = pl.BlockSpec((1, LANE), lambda i: (0, 0))
    return pl.pallas_call(
        _attn_prep_kernel,
        grid=(n // tm,),
        in_specs=[tok(AT_PAD_W), tab, tab, tab, row, row],
        out_specs=[tok(ATTN_WIDTH), tok(KV_W), tok(IDX_HEADS * IDX_DIM), tok(LANE)],
        out_shape=[jax.ShapeDtypeStruct((n, ATTN_WIDTH), F32), jax.ShapeDtypeStruct((n, KV_W), F32),
                   jax.ShapeDtypeStruct((n, IDX_HEADS * IDX_DIM), F32), jax.ShapeDtypeStruct((n, LANE), F32)],
        compiler_params=_cparams("parallel"),
        name="attn_prep",
    )(p_at, tabs[0], tabs[1], tabs[2], idx_g, idx_b)


def _rope_tables(pos):
    inv = ROPE_THETA ** (-jnp.arange(ROT_HALF, dtype=F32) * 2.0 / ROT)
    ang = pos.astype(F32)[:, None] * inv[None, :]
    cos, sin = jnp.cos(ang), jnp.sin(ang)
    n = pos.shape[0]
    rest = HEAD_DIM - ROT
    c = jnp.concatenate([cos, cos, jnp.ones((n, rest), F32)], axis=1)
    sa = jnp.concatenate([-sin, jnp.zeros((n, rest + ROT_HALF), F32)], axis=1)
    sb = jnp.concatenate([jnp.zeros((n, ROT_HALF), F32), sin, jnp.zeros((n, rest), F32)], axis=1)
    two = lambda t: jnp.concatenate([t, t], axis=1)
    return two(c), two(sa), two(sb)


def _select_bounds(key_ref, n_tiles, rows, kt, k_sel, idx_bits):
    def count(pred):
        def body(i, acc):
            off = pl.multiple_of(i * kt, kt)
            idx = off + lax.broadcasted_iota(I32, (rows, kt), 1)
            hit = jnp.where(pred(key_ref[:, pl.ds(off, kt)], idx), 1.0, 0.0)
            for c in range(kt // LANE):
                acc = acc + hit[:, c * LANE:(c + 1) * LANE]
            return acc
        acc = lax.fori_loop(0, n_tiles, body, jnp.zeros((rows, LANE), F32))
        return jnp.sum(acc, axis=-1, keepdims=True)

    def thr_bit(i, res):
        cand = res + jnp.left_shift(jnp.int32(1), 31 - i)
        c = count(lambda key, idx: key >= cand)
        return jnp.where(c >= k_sel, cand, res)

    thr = lax.fori_loop(0, 32, thr_bit, jnp.full((rows, 1), INT_MIN, I32))
    n_ge = count(lambda key, idx: key >= thr)

    def tie_search():
        need = k_sel - count(lambda key, idx: key > thr)

        def idx_bit(i, res):
            cand = res | jnp.left_shift(jnp.int32(1), idx_bits - 1 - i)
            c = count(lambda key, idx: (key == thr) & (idx < cand))
            return jnp.where(c < need, cand, res)

        return lax.fori_loop(0, idx_bits, idx_bit, jnp.zeros((rows, 1), I32))

    jmax = lax.cond(jnp.max(n_ge) > k_sel, tie_search,
                    lambda: jnp.full((rows, 1), 2 ** idx_bits - 1, I32))
    return thr, jmax


def _dsa_prompt_kernel(kt, n_sel, idx_bits, q_ref, qi_ref, wi_ref, k_ref, v_ref, ki_ref, o_ref,
                       key_ref, m_ref, l_ref, acc_ref):
    i = pl.program_id(1)
    n_kt = (i * Q_BLOCK + Q_BLOCK + kt - 1) // kt
    qi = qi_ref[0]
    qis = jnp.concatenate([qi[:, h * IDX_DIM:(h + 1) * IDX_DIM] for h in range(IDX_HEADS)], axis=0)
    wi = wi_ref[0] * IDX_HEADS ** -0.5
    qpos = i * Q_BLOCK + lax.broadcasted_iota(I32, (Q_BLOCK, kt), 0)
    lane = lax.broadcasted_iota(I32, (Q_BLOCK, kt), 1)

    def scores(t, carry):
        off = pl.multiple_of(t * kt, kt)
        s = lax.dot_general(qis, ki_ref[0, pl.ds(off, kt), :], _NT, preferred_element_type=F32)
        acc = jnp.zeros((Q_BLOCK, kt), F32)
        for h in range(IDX_HEADS):
            acc = acc + wi[:, h:h + 1] * jnp.maximum(s[h * Q_BLOCK:(h + 1) * Q_BLOCK] * IDX_DIM ** -0.5, 0.0)
        key_ref[:, pl.ds(off, kt)] = jnp.where(off + lane <= qpos, _f2key(acc), INT_MIN)
        return carry

    lax.fori_loop(0, n_kt, scores, 0)
    thr, jmax = _select_bounds(key_ref, n_kt, Q_BLOCK, kt, n_sel, idx_bits)

    q = q_ref[0].astype(F32) * HEAD_DIM ** -0.5
    grp = lax.broadcasted_iota(I32, (Q_BLOCK, KV_W), 1) >> 6
    rep = ATTN_HEADS // KV_HEADS

    def expand(h):
        qh = q[:, h * HEAD_DIM:(h + 1) * HEAD_DIM]
        return jnp.where(grp == h // rep, jnp.concatenate([qh] * KV_HEADS, axis=1), 0.0).astype(BF16)

    qe = [expand(h) for h in range(ATTN_HEADS)]
    m_ref[...] = jnp.full(m_ref.shape, NEG_BIG, F32)
    l_ref[...] = jnp.zeros(l_ref.shape, F32)
    acc_ref[...] = jnp.zeros(acc_ref.shape, F32)

    def attend(t, carry):
        off = pl.multiple_of(t * kt, kt)
        key = key_ref[:, pl.ds(off, kt)]
        kidx = off + lane
        sel = (kidx <= qpos) & ((key > thr) | ((key == thr) & (kidx <= jmax)))
        kt_tile = k_ref[0, pl.ds(off, kt), :]
        vt_tile = v_ref[0, pl.ds(off, kt), :]
        qk = lambda h: lax.dot_general(qe[h], kt_tile, _NT, preferred_element_type=F32)

        def finish(h, p, alpha):
            acc_ref[h] = alpha * acc_ref[h] + jnp.dot(p, vt_tile, preferred_element_type=F32)

        nxt = qk(0)
        pending = None
        for h in range(ATTN_HEADS):
            lg = nxt
            if h + 1 < ATTN_HEADS:
                nxt = qk(h + 1)
            lg = jnp.where(sel, lg, NEG_BIG)
            m = m_ref[h]
            mn = jnp.maximum(m, jnp.max(lg, axis=-1, keepdims=True))
            p = jnp.where(sel, jnp.exp(lg - mn), 0.0)
            alpha = jnp.exp(m - mn)
            l_ref[h] = alpha * l_ref[h] + jnp.sum(p, axis=-1, keepdims=True)
            m_ref[h] = mn
            if pending is not None:
                finish(*pending)
            pending = (h, p.astype(BF16), alpha)
        finish(*pending)
        return carry

    lax.fori_loop(0, n_kt, attend, 0)
    pieces = []
    for h in range(ATTN_HEADS):
        g = h // rep
        pieces.append(acc_ref[h][:, g * HEAD_DIM:(g + 1) * HEAD_DIM] / l_ref[h])
    o_ref[0] = jnp.concatenate(pieces, axis=1)


def _dsa_prompt(q, qi, wi, k, v, ki, t_real):
    bsz, tp, _ = q.shape
    kt = 640 if tp % 640 == 0 else LANE
    n_sel = min(TOPK_MAX, t_real // 4)
    idx_bits = max(1, (tp - 1).bit_length())
    blk = lambda w: pl.BlockSpec((1, Q_BLOCK, w), lambda b, i: (b, i, 0))
    seq = lambda w: pl.BlockSpec((1, tp, w), lambda b, i: (b, 0, 0))
    return pl.pallas_call(
        functools.partial(_dsa_prompt_kernel, kt, n_sel, idx_bits),
        grid=(bsz, tp // Q_BLOCK),
        in_specs=[blk(ATTN_WIDTH), blk(IDX_HEADS * IDX_DIM), blk(IDX_HEADS),
                  seq(KV_W), seq(KV_W), seq(IDX_DIM)],
        out_specs=blk(ATTN_WIDTH),
        out_shape=jax.ShapeDtypeStruct((bsz, tp, ATTN_WIDTH), F32),
        scratch_shapes=[pltpu.VMEM((Q_BLOCK, tp), I32),
                        pltpu.VMEM((ATTN_HEADS, Q_BLOCK, 1), F32),
                        pltpu.VMEM((ATTN_HEADS, Q_BLOCK, 1), F32),
                        pltpu.VMEM((ATTN_HEADS, Q_BLOCK, KV_W), F32)],
        compiler_params=_cparams("parallel", "arbitrary"),
        name="dsa_prompt",
    )(q, qi, wi, k, v, ki)


def _dsa_s_scores_kernel(pps, pt_ref, qi_ref, wi_ref, *refs):
    ci_refs, o_ref = refs[:pps], refs[pps]
    qi = qi_ref[0]
    w = wi_ref[0] * IDX_HEADS ** -0.5
    for u in range(pps):
        s = _dot3(qi, ci_refs[u][0, 0], _NT)
        sc = jnp.sum(w * jnp.maximum(s * IDX_DIM ** -0.5, 0.0), axis=0, keepdims=True)
        o_ref[0, :, u * PAGE_SIZE:(u + 1) * PAGE_SIZE] = sc


def _dsa_s_scores(page_table, qi3, wi3, cache_idx, layer):
    bd, npages = page_table.shape
    pps = PAGES_PER_STEP
    page = lambda u: pl.BlockSpec((1, 1, PAGE_SIZE, IDX_DIM),
                                  lambda b, p, pt: (layer, pt[b * npages + p * pps + u], 0, 0))
    gs = pltpu.PrefetchScalarGridSpec(
        num_scalar_prefetch=1,
        grid=(bd, npages // pps),
        in_specs=[pl.BlockSpec((1, IDX_HEADS, IDX_DIM), lambda b, p, pt: (b, 0, 0)),
                  pl.BlockSpec((1, IDX_HEADS, 1), lambda b, p, pt: (b, 0, 0))] + [page(u) for u in range(pps)],
        out_specs=pl.BlockSpec((1, 1, pps * PAGE_SIZE), lambda b, p, pt: (b, 0, p)),
    )
    return pl.pallas_call(
        functools.partial(_dsa_s_scores_kernel, pps),
        grid_spec=gs,
        out_shape=jax.ShapeDtypeStruct((bd, 1, npages * PAGE_SIZE), F32),
        compiler_params=_cparams("parallel", "arbitrary"),
        name="dsa_decode_scores",
    )(page_table.reshape(-1), qi3, wi3, *([cache_idx] * pps))


def _dsa_s_bounds_kernel(past, n_sel, idx_bits, sc_ref, qi_ref, kiw_ref, hm_ref, thr_o, j_o, kn_o, key_ref):
    rows = sc_ref.shape[0]
    kiw = kiw_ref[...]
    lane = lax.broadcasted_iota(I32, kiw.shape, 1)
    rolled = pltpu.roll(kiw, IDX_DIM, 1)
    ki2 = jnp.where(lane < IDX_DIM, kiw, rolled)
    w8 = jnp.where(lane < IDX_HEADS, rolled, 0.0) * IDX_HEADS ** -0.5
    prod = qi_ref[...] * jnp.concatenate([ki2] * (IDX_HEADS // 2), axis=1)
    s = _dot3(prod, hm_ref[...])
    new = jnp.sum(w8 * jnp.maximum(s * IDX_DIM ** -0.5, 0.0), axis=-1, keepdims=True)
    knew = _f2key(new)
    key_ref[:, 0:past] = _f2key(sc_ref[...])
    key_ref[:, past:past + LANE] = jnp.where(lane == 0, knew, INT_MIN)
    thr, jmax = _select_bounds(key_ref, (past + LANE) // LANE, rows, LANE, n_sel, idx_bits)
    thr_o[...] = thr
    j_o[...] = jmax
    kn_o[...] = knew


def _dsa_s_bounds(scores, qi, kiw, hm, past):
    rows = scores.shape[0]
    n_sel = min(TOPK_MAX, (past + 1) // 4)
    idx_bits = (past + LANE - 1).bit_length()
    out = jax.ShapeDtypeStruct((rows, 1), I32)
    return pl.pallas_call(
        functools.partial(_dsa_s_bounds_kernel, past, n_sel, idx_bits),
        out_shape=[out, out, out],
        scratch_shapes=[pltpu.VMEM((rows, past + LANE), I32)],
        compiler_params=pltpu.CompilerParams(vmem_limit_bytes=VMEM_LIMIT),
        name="dsa_decode_bounds",
    )(scores, qi, kiw, hm)


def _dsa_s_attend_kernel(pps, past, pt_ref, thr_ref, j_ref, kn_ref, qe_ref, sc_ref, kn_row_ref, vn_row_ref,
                         fold_ref, *refs):
    ck, cv, o_ref = refs[:pps], refs[pps:2 * pps], refs[2 * pps]
    m_ref, l_ref, acc_ref = refs[2 * pps + 1:]
    b = pl.program_id(0)
    p = pl.program_id(1)

    @pl.when(p == 0)
    def _():
        m_ref[...] = jnp.full(m_ref.shape, NEG_BIG, F32)
        l_ref[...] = jnp.zeros(l_ref.shape, F32)
        acc_ref[...] = jnp.zeros(acc_ref.shape, F32)

    thr, jmax = thr_ref[b], j_ref[b]
    qe = qe_ref[0]
    qeb = qe.astype(BF16)
    kidx = p * pps * PAGE_SIZE + lax.broadcasted_iota(I32, (1, pps * PAGE_SIZE), 1)
    kb = jnp.concatenate([ck[u][0, 0].astype(BF16) for u in range(pps)], axis=0)
    vb = jnp.concatenate([cv[u][0, 0].astype(BF16) for u in range(pps)], axis=0)
    lg = lax.dot_general(qeb, kb, _NT, preferred_element_type=F32) * HEAD_DIM ** -0.5
    key = _f2key(sc_ref[0])
    sel = (key > thr) | ((key == thr) & (kidx <= jmax))
    lg = jnp.where(sel, lg, NEG_BIG)
    m = m_ref[...]
    mn = jnp.maximum(m, jnp.max(lg, axis=-1, keepdims=True))
    pr = jnp.where(sel, jnp.exp(lg - mn), 0.0)
    alpha = jnp.exp(m - mn)
    l_ref[...] = alpha * l_ref[...] + jnp.sum(pr, axis=-1, keepdims=True)
    acc_ref[...] = alpha * acc_ref[...] + jnp.dot(pr.astype(BF16), vb, preferred_element_type=F32)
    m_ref[...] = mn

    @pl.when(p == pl.num_programs(1) - 1)
    def _():
        knew = kn_ref[b]
        sel_new = (knew > thr) | ((knew == thr) & (past <= jmax))
        lg = jnp.sum(qe * kn_row_ref[0], axis=-1, keepdims=True) * HEAD_DIM ** -0.5
        m = m_ref[...]
        mn = jnp.where(sel_new, jnp.maximum(m, lg), m)
        pr = jnp.where(sel_new, jnp.exp(lg - mn), 0.0)
        alpha = jnp.exp(m - mn)
        l = alpha * l_ref[...] + pr
        acc = alpha * acc_ref[...] + pr * vn_row_ref[0]
        rowi = lax.broadcasted_iota(I32, acc.shape, 0)
        lanei = lax.broadcasted_iota(I32, acc.shape, 1)
        own = jnp.where((lanei >> 6) == (rowi >> 1), acc / l, 0.0)
        o_ref[0] = _dot_sel(own, fold_ref[...])


def _dsa_s_attend(page_table, thr, jmax, knew, qe, scores, k_new, v_new, fold, cache_k, cache_v, layer):
    bd, npages = page_table.shape
    past = npages * PAGE_SIZE
    pps = PAGES_PER_STEP
    page = lambda u: pl.BlockSpec((1, 1, PAGE_SIZE, KV_W),
                                  lambda b, p, pt, t, j, kn: (layer, pt[b * npages + p * pps + u], 0, 0))
    per_b = lambda s: pl.BlockSpec((1,) + s, lambda b, p, pt, t, j, kn: (b, 0, 0))
    gs = pltpu.PrefetchScalarGridSpec(
        num_scalar_prefetch=4,
        grid=(bd, npages // pps),
        in_specs=[per_b((ATTN_HEADS, KV_W)),
                  pl.BlockSpec((1, 1, pps * PAGE_SIZE), lambda b, p, pt, t, j, kn: (b, 0, p)),
                  per_b((1, KV_W)), per_b((1, KV_W)),
                  pl.BlockSpec(fold.shape, lambda b, p, pt, t, j, kn: (0, 0))]
                 + [page(u) for u in range(pps)] * 2,
        out_specs=per_b((ATTN_HEADS, HEAD_DIM)),
        scratch_shapes=[pltpu.VMEM((ATTN_HEADS, 1), F32), pltpu.VMEM((ATTN_HEADS, 1), F32),
                        pltpu.VMEM((ATTN_HEADS, KV_W), F32)],
    )
    return pl.pallas_call(
        functools.partial(_dsa_s_attend_kernel, pps, past),
        grid_spec=gs,
        out_shape=jax.ShapeDtypeStruct((bd, ATTN_HEADS, HEAD_DIM), F32),
        compiler_params=_cparams("parallel", "arbitrary"),
        name="dsa_decode_attend",
    )(page_table.reshape(-1), thr.reshape(-1), jmax.reshape(-1), knew.reshape(-1),
      qe, scores, k_new, v_new, fold, *([cache_k] * pps), *([cache_v] * pps))


def _merge_ln_kernel(o_ref, bonus_ref, g_ref, at_ref, x_ref, wt_ref, wb_ref, xg_ref, xb_ref,
                     lg_ref, lb_ref, bd_ref, out_ref):
    bd = bd_ref[...]
    o = o_ref[...]
    mean = _dot_sel(o, bd) * (1.0 / HEAD_DIM)
    oc = o - mean
    var = _dot_sel(oc * oc, bd) * (1.0 / HEAD_DIM)
    rw = (oc * lax.rsqrt(var + GN_EPS) * xg_ref[...] + xb_ref[...] + bonus_ref[...]) * g_ref[...]
    f = (jnp.dot(rw.astype(BF16), wt_ref[...], preferred_element_type=F32)
         + jnp.dot(at_ref[...].astype(BF16), wb_ref[...], preferred_element_type=F32))
    out_ref[...] = _layer_norm(DEEPNORM_ALPHA * x_ref[...] + f, lg_ref[...], lb_ref[...])


def _merge_ln(o, bonus, g, at, x, wts, tm):
    n = x.shape[0]
    tm = min(tm, n)
    tok = lambda w: pl.BlockSpec((tm, w), lambda i: (i, 0))
    full = lambda a: pl.BlockSpec(a.shape, lambda i: (0, 0))
    ws = [wts["wo_top"], wts["wo_bot"], wts["lnx_g"], wts["lnx_b"], wts["ln1_g"], wts["ln1_b"], wts["bd"]]
    return pl.pallas_call(
        _merge_ln_kernel,
        grid=(n // tm,),
        in_specs=[tok(RWKV_WIDTH)] * 3 + [tok(ATTN_WIDTH), tok(D_MODEL)] + [full(a) for a in ws],
        out_specs=tok(D_MODEL),
        out_shape=jax.ShapeDtypeStruct((n, D_MODEL), F32),
        compiler_params=_cparams("parallel"),
        name="merge_ln",
    )(o, bonus, g, at, x, *ws)


def _take_top(src_ref, n_rows, val_ref, idx_ref):
    shape = src_ref.shape
    row = lax.broadcasted_iota(I32, shape, 1)

    def body(a, carry):
        sv = src_ref[...]
        m = jnp.max(sv, axis=1, keepdims=True)
        idx = jnp.min(jnp.where(sv == m, row, n_rows), axis=1, keepdims=True)
        val_ref[:, pl.ds(a, 1), :] = m
        idx_ref[:, pl.ds(a, 1), :] = idx
        src_ref[...] = jnp.where(row == idx, -jnp.inf, sv)
        return carry

    lax.fori_loop(0, PEER_TOPK, body, 0)


def _peer_route_kernel(x_ref, wq_ref, sk0_ref, sk1_ref, e_o, g_o, s_ref, t_ref, i_ref, c_ref, ts_ref, ic_ref):
    q = jnp.dot(x_ref[...].astype(BF16), wq_ref[...], preferred_element_type=F32)
    for h in range(PEER_HEADS):
        qh = q[:, h * PEER_DKEY:(h + 1) * PEER_DKEY]
        s_ref[h] = _dot3(sk0_ref[...], qh, _NT)
        s_ref[PEER_HEADS + h] = _dot3(sk1_ref[...], qh, _NT)
    _take_top(s_ref, N_KEYS, t_ref, i_ref)
    t1, t2 = t_ref[0:PEER_HEADS], t_ref[PEER_HEADS:2 * PEER_HEADS]
    c_ref[...] = jnp.concatenate([t1[:, a:a + 1, :] + t2 for a in range(PEER_TOPK)], axis=1)
    _take_top(c_ref, PEER_TOPK * PEER_TOPK, ts_ref, ic_ref)
    ic = ic_ref[...]
    i1, i2 = i_ref[0:PEER_HEADS], i_ref[PEER_HEADS:2 * PEER_HEADS]
    ia, ib = ic >> 4, ic & (PEER_TOPK - 1)
    e = jnp.zeros(ic.shape, I32)
    for a in range(PEER_TOPK):
        e = e + jnp.where(ia == a, i1[:, a:a + 1, :] * N_KEYS, 0) + jnp.where(ib == a, i2[:, a:a + 1, :], 0)
    ts = ts_ref[...]
    ex = jnp.exp(ts - jnp.max(ts, axis=1, keepdims=True))
    e_o[0] = e
    g_o[0] = ex / jnp.sum(ex, axis=1, keepdims=True)


def _peer_route(x, wq, sk0, sk1, tt):
    n = x.shape[0]
    nt = n // tt
    full = lambda a: pl.BlockSpec(a.shape, lambda i: (0, 0))
    out = pl.BlockSpec((1, PEER_HEADS, PEER_TOPK, tt), lambda i: (i, 0, 0, 0))
    return pl.pallas_call(
        _peer_route_kernel,
        grid=(nt,),
        in_specs=[pl.BlockSpec((tt, D_MODEL), lambda i: (i, 0)), full(wq), full(sk0), full(sk1)],
        out_specs=[out, out],
        out_shape=[jax.ShapeDtypeStruct((nt, PEER_HEADS, PEER_TOPK, tt), I32),
                   jax.ShapeDtypeStruct((nt, PEER_HEADS, PEER_TOPK, tt), F32)],
        scratch_shapes=[pltpu.VMEM((2 * PEER_HEADS, N_KEYS, tt), F32),
                        pltpu.VMEM((2 * PEER_HEADS, PEER_TOPK, tt), F32),
                        pltpu.VMEM((2 * PEER_HEADS, PEER_TOPK, tt), I32),
                        pltpu.VMEM((PEER_HEADS, PEER_TOPK * PEER_TOPK, tt), F32),
                        pltpu.VMEM((PEER_HEADS, PEER_TOPK, tt), F32),
                        pltpu.VMEM((PEER_HEADS, PEER_TOPK, tt), I32)],
        compiler_params=_cparams("parallel"),
        name="peer_route",
    )(x, wq, sk0, sk1)


def _peer_gather_kernel(tt, n_valid, idx_hbm, x_ref, g_ref, uv_hbm, lg_ref, lb_ref, o_ref, idx_smem, *rest):
    bufs, (sem, isem, y_ref) = rest[:GATHER_BUFFERS], rest[GATHER_BUFFERS:]
    i = pl.program_id(0)
    n_idx = PEER_SLOTS * tt
    islot = i % 2

    def idx_copy(tile, slot):
        return pltpu.make_async_copy(idx_hbm.at[tile], idx_smem.at[pl.ds(slot * n_idx, n_idx)], isem.at[slot])

    @pl.when(i == 0)
    def _():
        idx_copy(0, 0).start()

    if n_valid < tt:
        y_ref[...] = jnp.zeros(y_ref.shape, F32)
    idx_copy(i, islot).wait()

    @pl.when(i + 1 < pl.num_programs(0))
    def _():
        idx_copy(i + 1, 1 - islot).start()

    base = islot * n_idx

    def issue(t, k):
        for s in range(PEER_SLOTS):
            e = idx_smem[base + s * tt + t]
            pltpu.async_copy(uv_hbm.at[pl.ds(e, 1)], bufs[k].at[pl.ds(s, 1)], sem.at[k], priority=s % 2)

    def wait(k):
        pltpu.make_async_copy(uv_hbm.at[pl.ds(0, PEER_SLOTS)], bufs[k], sem.at[k]).wait()

    lane = lax.broadcasted_iota(I32, (PEER_SLOTS, tt), 1)

    def compute(t, k):
        xrow = x_ref[pl.ds(t, 1), :]
        word = bufs[k][...]
        u = lax.bitcast_convert_type(word << 16, F32)
        v = lax.bitcast_convert_type(word & jnp.int32(-65536), F32)
        h = jnp.sum(u * xrow, axis=-1, keepdims=True)
        gate = jnp.sum(jnp.where(lane == t, g_ref[0], 0.0), axis=-1, keepdims=True)
        coef = gate * jax.nn.gelu(h)
        y_ref[pl.ds(t, 1), :] = jnp.sum(coef * v, axis=0, keepdims=True)

    depth = len(bufs)
    ahead = depth - 1
    rounds = max(n_valid - ahead, 0) // depth
    for t in range(min(ahead, n_valid)):
        issue(t, t % depth)

    def body(r, carry):
        for k in range(depth):
            t = r * depth + k
            wait(k)
            issue(t + ahead, (k + ahead) % depth)
            compute(t, k)
        return carry

    lax.fori_loop(0, rounds, body, 0)
    for t in range(rounds * depth, n_valid):
        wait(t % depth)
        if t + ahead < n_valid:
            issue(t + ahead, (t + ahead) % depth)
        compute(t, t % depth)
    o_ref[...] = _layer_norm(DEEPNORM_ALPHA * x_ref[...] + y_ref[...], lg_ref[...], lb_ref[...])


def _peer_gather(idx, gates, x, uv, ln_g, ln_b, tt, n_valid):
    n = x.shape[0]
    nt = n // tt
    row = pl.BlockSpec((1, D_MODEL), lambda i: (0, 0))
    return pl.pallas_call(
        functools.partial(_peer_gather_kernel, tt, n_valid),
        grid=(nt,),
        in_specs=[pl.BlockSpec(memory_space=pl.ANY),
                  pl.BlockSpec((tt, D_MODEL), lambda i: (i, 0)),
                  pl.BlockSpec((1, PEER_SLOTS, tt), lambda i: (i, 0, 0)),
                  pl.BlockSpec(memory_space=pl.ANY), row, row],
        out_specs=pl.BlockSpec((tt, D_MODEL), lambda i: (i, 0)),
        out_shape=jax.ShapeDtypeStruct((n, D_MODEL), F32),
        scratch_shapes=[pltpu.SMEM((2 * PEER_SLOTS * tt,), I32)]
                       + [pltpu.VMEM((PEER_SLOTS, D_MODEL), I32)] * GATHER_BUFFERS
                       + [pltpu.SemaphoreType.DMA((GATHER_BUFFERS,)),
                          pltpu.SemaphoreType.DMA((2,)),
                          pltpu.VMEM((tt, D_MODEL), F32)],
        compiler_params=_cparams("arbitrary"),
        name="peer_gather",
    )(idx, x, gates, uv, ln_g, ln_b)


def _peer(x, wts, tt, n_valid):
    e, gates = _peer_route(x, wts["wq"], wts["sk0"], wts["sk1"], tt)
    nt = x.shape[0] // tt
    return _peer_gather(e.reshape(nt, PEER_SLOTS * tt), gates.reshape(nt, PEER_SLOTS, tt), x,
                        wts["uv"], wts["ln2_g"], wts["ln2_b"], tt, n_valid)


def _pack_bf16_pair(lo, hi):
    bits = lambda a: lax.bitcast_convert_type(a.astype(BF16), jnp.uint16).astype(jnp.uint32)
    return lax.bitcast_convert_type(bits(lo) | (bits(hi) << 16), I32)


def _layer_weights(l, w_in, shift_mu, decay_w0, decay_up, iclr_a0, iclr_up, gate_up, k_k, k_a, r_k,
                   lnx_g, lnx_b, idx_ln_g, idx_ln_b, w_out, ln1_g, ln1_b, ln2_g, ln2_b,
                   peer_wq, peer_subkeys, peer_u, peer_v):
    row = lambda a: a.reshape(1, -1).astype(F32)
    lora = jnp.zeros((LORA_W, 3 * RWKV_WIDTH), F32)
    lora = lora.at[0:W_LORA, 0:RWKV_WIDTH].set(decay_up[l])
    lora = lora.at[W_LORA:W_LORA + A_LORA, RWKV_WIDTH:2 * RWKV_WIDTH].set(iclr_up[l])
    lora = lora.at[W_LORA + A_LORA:, 2 * RWKV_WIDTH:].set(gate_up[l])
    lora_hi = lora.astype(BF16)
    pad_lane = lambda a: jnp.pad(a.reshape(1, -1), ((0, 0), (0, LANE - a.size)))
    half = PEER_DKEY // 2
    seg = jnp.arange(RWKV_WIDTH) // HEAD_DIM
    return dict(
        w_rw=w_in[l][:, :SHIFT_W].astype(BF16),
        w_at=jnp.pad(w_in[l][:, SHIFT_W:], ((0, 0), (0, AT_PAD_W - ATTN_PROJ_W))).astype(BF16),
        mu=row(shift_mu[l]), w0=row(decay_w0[l]), a0=row(iclr_a0[l]),
        lora_hi=lora_hi, lora_lo=(lora - lora_hi.astype(F32)).astype(BF16),
        k_k=row(k_k[l]), k_a=row(k_a[l]), r_k=row(r_k[l]),
        bd=(seg[:, None] == seg[None, :]).astype(BF16),
        lnx_g=row(lnx_g[l]), lnx_b=row(lnx_b[l]),
        idx_g=pad_lane(idx_ln_g[l]), idx_b=pad_lane(idx_ln_b[l]),
        wo_top=w_out[l][:RWKV_WIDTH].astype(BF16), wo_bot=w_out[l][RWKV_WIDTH:].astype(BF16),
        ln1_g=row(ln1_g[l]), ln1_b=row(ln1_b[l]), ln2_g=row(ln2_g[l]), ln2_b=row(ln2_b[l]),
        wq=peer_wq[l].astype(BF16),
        sk0=jnp.pad(peer_subkeys[l, 0], ((0, 0), (0, half))),
        sk1=jnp.pad(peer_subkeys[l, 1], ((0, 0), (half, 0))),
        uv=_pack_bf16_pair(peer_u[l], peer_v[l]),
    )


def _tile(n, cap):
    if n <= cap:
        return n
    return max(d for d in range(8, cap + 1, 8) if n % d == 0)


def _mixer_front(x, wts, tabs):
    tm = _tile(x.shape[0], 512)
    p_rw = _matmul(x, wts["w_rw"], tm)
    p_at = _matmul(x, wts["w_at"], tm)
    q, k, qi, kiw = _attn_prep(p_at, tabs, wts["idx_g"], wts["idx_b"], _tile(tabs[0].shape[0], 640))
    v = p_at[:, ATTN_WIDTH + KV_W:ATTN_WIDTH + 2 * KV_W]
    return p_rw, q, k, v, qi, kiw


def kernel(x_prompt, x_sample, cache_k, cache_v, cache_idx_k, state_wkv, state_shift, page_table, meta_tokens, w_in, shift_mu, decay_w0, decay_up, iclr_a0, iclr_up, gate_up, k_k, k_a, r_k, lnx_g, lnx_b, idx_ln_g, idx_ln_b, w_out, ln1_g, ln1_b, ln2_g, ln2_b, peer_wq, peer_subkeys, peer_u, peer_v):
    bsz, seq, _ = x_prompt.shape
    t = seq + N_META
    tp = -(-t // LANE) * LANE
    n_p = bsz * tp
    depth = w_in.shape[0]
    bd_, ts_, _ = x_sample.shape
    assert ts_ == 1
    npages = page_table.shape[1]
    past = npages * PAGE_SIZE
    n_pool = cache_k.shape[1]
    peer_tt = LANE
    ns_pad = -(-bd_ // peer_tt) * peer_tt

    xp = jnp.concatenate([jnp.broadcast_to(meta_tokens[None], (bsz, N_META, D_MODEL)), x_prompt], axis=1)
    xp = jnp.pad(xp, ((0, 0), (0, tp - t), (0, 0))).reshape(n_p, D_MODEL)
    xs = x_sample.reshape(bd_, D_MODEL)
    tabs_p = _rope_tables(jnp.arange(tp, dtype=I32))
    tabs_s = _rope_tables(jnp.full((bd_,), past, I32))
    ck = cache_k.reshape(depth, n_pool, PAGE_SIZE, KV_W)
    cv = cache_v.reshape(depth, n_pool, PAGE_SIZE, KV_W)
    hsel = jnp.arange(IDX_HEADS * IDX_DIM)[:, None] // IDX_DIM == jnp.arange(LANE)[None, :]
    hm = hsel.astype(F32)
    fold = (jnp.arange(KV_W)[:, None] % HEAD_DIM == jnp.arange(HEAD_DIM)[None, :]).astype(BF16)
    own = (jnp.arange(KV_W)[None, :] // HEAD_DIM == jnp.arange(ATTN_HEADS)[:, None] // (ATTN_HEADS // KV_HEADS))

    k_p, v_p, ki_p, wkv_p, sh_p = [], [], [], [], []
    k_s, v_s, ki_s, wkv_s, sh_s = [], [], [], [], []
    for l in range(depth):
        wts = _layer_weights(l, w_in, shift_mu, decay_w0, decay_up, iclr_a0, iclr_up, gate_up, k_k, k_a, r_k,
                             lnx_g, lnx_b, idx_ln_g, idx_ln_b, w_out, ln1_g, ln1_b, ln2_g, ln2_b,
                             peer_wq, peer_subkeys, peer_u, peer_v)

        p_rw, q, k, v, qi, kiw = _mixer_front(xp, wts, tabs_p)
        p_rw3 = p_rw.reshape(bsz, tp, SHIFT_W)
        r_, w_, kt_, kk_, b_, vv_, bonus, g = _rwkv_prep(
            p_rw3, jnp.zeros((bsz, 1, SHIFT_W), F32), t, True, wts, LANE)
        o, s_fin = _wkv(r_, w_, kt_, kk_, b_, vv_,
                        jnp.zeros((bsz, RWKV_HEADS, HEAD_DIM, HEAD_DIM), F32), bsz, LANE)
        three = lambda a: a.reshape(bsz, tp, -1)
        at = _dsa_prompt(three(q).astype(BF16), three(qi).astype(BF16),
                         three(kiw)[:, :, IDX_DIM:IDX_DIM + IDX_HEADS],
                         three(k).astype(BF16), three(v).astype(BF16),
                         three(kiw)[:, :, :IDX_DIM].astype(BF16), t)
        flat = lambda a: a.reshape(n_p, -1)
        x1 = _merge_ln(flat(o), flat(bonus), flat(g), flat(at), xp, wts, _tile(n_p, 256))
        xp = _peer(x1, wts, peer_tt, peer_tt)
        k_p.append(three(k)[:, :t].reshape(bsz, t, KV_HEADS, HEAD_DIM))
        v_p.append(three(v)[:, :t].reshape(bsz, t, KV_HEADS, HEAD_DIM))
        ki_p.append(three(kiw)[:, :t, :IDX_DIM])
        wkv_p.append(s_fin)
        sh_p.append(p_rw3[:, t - 1])

        p_rw, q, k, v, qi, kiw = _mixer_front(xs, wts, tabs_s)
        r_, w_, kt_, kk_, b_, vv_, bonus, g = _rwkv_prep(
            p_rw[None], state_shift[l][None], bd_, False, wts, bd_)
        tc_s = 8
        padt = lambda a, c: jnp.pad(a[0][:, None, :], ((0, 0), (0, tc_s - 1), (0, 0)), constant_values=c)
        o, s_fin = _wkv(padt(r_, 0.0), padt(w_, 1.0), padt(kt_, 0.0), padt(kk_, 0.0), padt(b_, 0.0),
                        padt(vv_, 0.0), state_wkv[l].astype(F32), 4, tc_s)
        o = o[:, 0]
        scores = _dsa_s_scores(page_table, qi.reshape(bd_, IDX_HEADS, IDX_DIM),
                               kiw[:, IDX_DIM:IDX_DIM + IDX_HEADS].reshape(bd_, IDX_HEADS, 1),
                               cache_idx_k, l)
        thr, jmax, knew = _dsa_s_bounds(scores.reshape(bd_, past), qi, kiw, hm, past)
        qe = jnp.where(own[None], jnp.tile(q.reshape(bd_, ATTN_HEADS, HEAD_DIM), (1, 1, KV_HEADS)), 0.0)
        at = _dsa_s_attend(page_table, thr, jmax, knew, qe, scores, k[:, None, :], v[:, None, :], fold,
                           ck, cv, l).reshape(bd_, ATTN_WIDTH)
        x1 = _merge_ln(o, bonus[0], g[0], at, xs, wts, bd_)
        x1p = jnp.pad(x1, ((0, ns_pad - bd_), (0, 0)))
        xs = _peer(x1p, wts, peer_tt, bd_)[:bd_]
        k_s.append(k.reshape(bd_, 1, KV_HEADS, HEAD_DIM))
        v_s.append(v.reshape(bd_, 1, KV_HEADS, HEAD_DIM))
        ki_s.append(kiw[:, None, :IDX_DIM])
        wkv_s.append(s_fin)
        sh_s.append(p_rw)

    y_prompt = xp.reshape(bsz, tp, D_MODEL)[:, N_META:t]
    y_sample = xs.reshape(bd_, 1, D_MODEL)
    return (y_prompt, y_sample, jnp.stack(k_p), jnp.stack(v_p), jnp.stack(ki_p),
            jnp.stack(wkv_p).astype(state_wkv.dtype), jnp.stack(sh_p).astype(state_shift.dtype),
            jnp.stack(k_s), jnp.stack(v_s), jnp.stack(ki_s),
            jnp.stack(wkv_s).astype(state_wkv.dtype), jnp.stack(sh_s).astype(state_shift.dtype))
```

```python
import functools
import math

import jax
import jax.numpy as jnp
from jax import lax
from jax.experimental import pallas as pl
from jax.experimental.pallas import tpu as pltpu

F32 = jnp.float32
BF16 = jnp.bfloat16
I32 = jnp.int32

D_MODEL = 1024
N_META = 16
HEAD_DIM = 64
RWKV_WIDTH = D_MODEL // 2
RWKV_HEADS = RWKV_WIDTH // HEAD_DIM
ATTN_WIDTH = D_MODEL - RWKV_WIDTH
ATTN_HEADS = ATTN_WIDTH // HEAD_DIM
KV_HEADS = ATTN_HEADS // 2
KV_W = KV_HEADS * HEAD_DIM
W_LORA = 64
A_LORA = 64
G_LORA = 128
LORA_W = W_LORA + A_LORA + G_LORA
SHIFT_W = 3 * RWKV_WIDTH + LORA_W
IDX_HEADS = 8
IDX_DIM = 64
ATTN_PROJ_W = ATTN_WIDTH + 2 * KV_W + IDX_HEADS * IDX_DIM + IDX_DIM + IDX_HEADS
TOPK_MAX = 256
ROPE_THETA = 500000.0
ROT = HEAD_DIM // 4
ROT_HALF = ROT // 2
PEER_HEADS = 8
PEER_DKEY = 128
N_KEYS = 128
PEER_TOPK = 16
PEER_SLOTS = PEER_HEADS * PEER_TOPK
DEPTH = 2
DEEPNORM_ALPHA = (2.0 * DEPTH) ** 0.25
PAGE_SIZE = 128
LN_EPS = 1e-5
GN_EPS = 64e-5

LANE = 128
SUBLANE = 8
Q_BLOCK = 128
INT_MIN = -(2 ** 31)
NEG_BIG = -1e30
VMEM_LIMIT = 56 * 1024 * 1024
AT_PAD_W = 1664
KI_OFF = ATTN_WIDTH + 2 * KV_W + IDX_HEADS * IDX_DIM
PAGES_PER_STEP = 8
GATHER_BUFFERS = 6

_NT = (((1,), (1,)), ((), ()))


def _cparams(*sem):
    return pltpu.CompilerParams(dimension_semantics=sem, vmem_limit_bytes=VMEM_LIMIT)


def _split2(x):
    hi = x.astype(BF16)
    lo = (x - hi.astype(F32)).astype(BF16)
    return hi, lo


def _split3(x):
    hi = x.astype(BF16)
    r1 = x - hi.astype(F32)
    mid = r1.astype(BF16)
    lo = (r1 - mid.astype(F32)).astype(BF16)
    return hi, mid, lo


def _dot3(a, b, dims=None):
    ah, al = _split2(a)
    bh, bl = _split2(b)
    if dims is None:
        d = lambda p, q: jnp.dot(p, q, preferred_element_type=F32)
    else:
        d = lambda p, q: lax.dot_general(p, q, dims, preferred_element_type=F32)
    return d(ah, bh) + d(al, bh) + d(ah, bl)


def _dot_sel(x, m):
    h, mid, lo = _split3(x)
    d = lambda p: jnp.dot(p, m, preferred_element_type=F32)
    return d(h) + d(mid) + d(lo)


def _f2key(x):
    x = jnp.where(x == 0.0, 0.0, x)
    b = lax.bitcast_convert_type(x, I32)
    return b ^ ((b >> 31) & 0x7FFFFFFF)


def _layer_norm(z, g, b):
    mu = jnp.mean(z, axis=-1, keepdims=True)
    zc = z - mu
    var = jnp.mean(zc * zc, axis=-1, keepdims=True)
    return zc * lax.rsqrt(var + LN_EPS) * g + b


def _mm_kernel(x_ref, w_ref, o_ref):
    o_ref[...] = jnp.dot(x_ref[...].astype(BF16), w_ref[...], preferred_element_type=F32)


def _matmul(x, w, tm):
    m, k = x.shape
    n = w.shape[1]
    tm = min(tm, m)
    return pl.pallas_call(
        _mm_kernel,
        grid=(m // tm,),
        in_specs=[pl.BlockSpec((tm, k), lambda i: (i, 0)),
                  pl.BlockSpec((k, n), lambda i: (0, 0))],
        out_specs=pl.BlockSpec((tm, n), lambda i: (i, 0)),
        out_shape=jax.ShapeDtypeStruct((m, n), F32),
        compiler_params=_cparams("parallel"),
        name="proj_matmul",
    )(x, w)


def _rwkv_prep_kernel(t_real, tt, shift, p_ref, prev_ref, mu_ref, w0_ref, a0_ref, lwh_ref, lwl_ref,
                      kk_ref, ka_ref, rk_ref, bd_ref,
                      r_o, w_o, kt_o, kko_o, b_o, v_o, bonus_o, g_o, carry_ref):
    j = pl.program_id(1)
    pf = p_ref[0]
    if shift:
        @pl.when(j == 0)
        def _():
            carry_ref[...] = prev_ref[0]
        row = lax.broadcasted_iota(I32, pf.shape, 0)
        prev = jnp.where(row == 0, carry_ref[...], pltpu.roll(pf, 1, 0))
        carry_ref[...] = pf[tt - 1:tt, :]
    else:
        prev = prev_ref[0]
    xs = pf + mu_ref[...] * (prev - pf)
    r = xs[:, 0:RWKV_WIDTH]
    k = xs[:, RWKV_WIDTH:2 * RWKV_WIDTH]
    v = xs[:, 2 * RWKV_WIDTH:3 * RWKV_WIDTH]
    z = xs[:, 3 * RWKV_WIDTH:SHIFT_W]
    lane = lax.broadcasted_iota(I32, z.shape, 1)
    zt = jnp.where(lane < W_LORA, jnp.tanh(z),
                   jnp.where(lane < W_LORA + A_LORA, z, jax.nn.sigmoid(z)))
    zh, zl = _split2(zt)
    d = lambda p, q: jnp.dot(p, q, preferred_element_type=F32)
    lo = d(zh, lwh_ref[...]) + d(zl, lwh_ref[...]) + d(zh, lwl_ref[...])
    w_raw = w0_ref[...] + lo[:, 0:RWKV_WIDTH]
    a = jax.nn.sigmoid(a0_ref[...] + lo[:, RWKV_WIDTH:2 * RWKV_WIDTH])
    g = lo[:, 2 * RWKV_WIDTH:3 * RWKV_WIDTH]
    decay = jnp.exp(-math.exp(-0.5) * jax.nn.sigmoid(w_raw))
    bd = bd_ref[...]
    kk = k * kk_ref[...]
    kk = kk / jnp.maximum(jnp.sqrt(_dot_sel(kk * kk, bd)), 1e-12)
    kt = k * (1.0 + (a - 1.0) * ka_ref[...])
    bonus = _dot_sel(r * kt * rk_ref[...], bd) * v
    pos = j * tt + lax.broadcasted_iota(I32, r.shape, 0)
    valid = pos < t_real
    r_o[0] = r
    w_o[0] = jnp.where(valid, decay, 1.0)
    kt_o[0] = jnp.where(valid, kt, 0.0)
    kko_o[0] = jnp.where(valid, kk, 0.0)
    b_o[0] = jnp.where(valid, kk * a, 0.0)
    v_o[0] = v
    bonus_o[0] = bonus
    g_o[0] = g


def _rwkv_prep(p_rw, prev, t_real, shift, wts, tt):
    bsz, tp, _ = p_rw.shape
    tt = min(tt, tp)
    row = lambda n: pl.BlockSpec((1, n), lambda b, j: (0, 0))
    full = lambda a: pl.BlockSpec(a.shape, lambda b, j: (0, 0))
    tok = lambda n: pl.BlockSpec((1, tt, n), lambda b, j: (b, j, 0))
    prev_spec = pl.BlockSpec((1, 1, SHIFT_W), lambda b, j: (b, 0, 0)) if shift else tok(SHIFT_W)
    outs = pl.pallas_call(
        functools.partial(_rwkv_prep_kernel, t_real, tt, shift),
        grid=(bsz, tp // tt),
        in_specs=[tok(SHIFT_W), prev_spec, row(SHIFT_W), row(RWKV_WIDTH), row(RWKV_WIDTH),
                  full(wts["lora_hi"]), full(wts["lora_lo"]),
                  row(RWKV_WIDTH), row(RWKV_WIDTH), row(RWKV_WIDTH), full(wts["bd"])],
        out_specs=[tok(RWKV_WIDTH)] * 8,
        out_shape=[jax.ShapeDtypeStruct((bsz, tp, RWKV_WIDTH), F32)] * 8,
        scratch_shapes=[pltpu.VMEM((1, SHIFT_W), F32)],
        compiler_params=_cparams("parallel", "arbitrary"),
        name="rwkv_prep",
    )(p_rw, prev, wts["mu"], wts["w0"], wts["a0"], wts["lora_hi"], wts["lora_lo"],
      wts["k_k"], wts["k_a"], wts["r_k"], wts["bd"])
    return outs


def _wkv_kernel(bb, tc, r_ref, w_ref, kt_ref, kk_ref, b_ref, v_ref, s0_ref, o_ref, sf_ref, s_ref):
    c = pl.program_id(1)

    @pl.when(c == 0)
    def _():
        s_ref[...] = s0_ref[...]

    lane = lax.broadcasted_iota(I32, (HEAD_DIM, LANE), 1)
    row = lax.broadcasted_iota(I32, (HEAD_DIM, LANE), 0)
    lo = lane < HEAD_DIM
    e0 = lane == row
    e1 = lane == row + HEAD_DIM
    e01 = e0 | e1
    r128 = lax.broadcasted_iota(I32, (LANE, LANE), 0)
    l128 = lax.broadcasted_iota(I32, (LANE, LANE), 1)
    half_ones = ((r128 >> 6) == (l128 >> 6)).astype(BF16)
    npair = RWKV_HEADS // 2

    def half_sums_mxu(parts, n_split):
        res = jnp.dot(jnp.concatenate(parts, axis=0), half_ones, preferred_element_type=F32)
        out = []
        for i in range(len(parts) // n_split):
            acc = res[i * n_split * HEAD_DIM:(i * n_split + 1) * HEAD_DIM]
            for p in range(1, n_split):
                acc = acc + res[(i * n_split + p) * HEAD_DIM:(i * n_split + p + 1) * HEAD_DIM]
            out.append(acc)
        return out

    def group(gi, carry):
        t0 = pl.multiple_of(gi * SUBLANE, SUBLANE)
        rows = pl.ds(t0, SUBLANE)
        blk = lambda ref, b: [ref[b, rows, j * LANE:(j + 1) * LANE] for j in range(npair)]

        def v_pieces(b):
            out = []
            for v8 in blk(v_ref, b):
                vh = v8.astype(BF16).astype(F32)
                r1 = v8 - vh
                vm = r1.astype(BF16).astype(F32)
                out.append((vh, vm, r1 - vm))
            return out

        def v_columns(pieces, u):
            parts = []
            for j in range(npair):
                parts += [jnp.where(e01, pc[u:u + 1], 0.0).astype(BF16) for pc in pieces[j]]
            return half_sums_mxu(parts, 3)

        def out_rows(qparts):
            return [jnp.sum(jnp.where(e01, oc, 0.0), axis=0, keepdims=True) for oc in half_sums_mxu(qparts, 2)]

        pieces = v_pieces(0)
        vcols = [v_columns(pieces, u) for u in range(SUBLANE)]
        pending = None
        for b in range(bb + 1):
            if b < bb:
                kk8, w8, b8, kt8, r8 = blk(kk_ref, b), blk(w_ref, b), blk(b_ref, b), blk(kt_ref, b), blk(r_ref, b)
                st = [s_ref[b, j] for j in range(npair)]
                nxt_pieces = v_pieces(b + 1) if b + 1 < bb else None
            nxt_vcols, qsteps, orows = [], [], []
            for u in range(SUBLANE):
                if b < bb:
                    prods = [st[j] * kk8[j][u:u + 1] for j in range(npair)]
                    sums = [(jnp.sum(jnp.where(lo, p, 0.0), axis=-1, keepdims=True),
                             jnp.sum(jnp.where(lo, 0.0, p), axis=-1, keepdims=True)) for p in prods]
                if nxt_pieces is not None:
                    nxt_vcols.append(v_columns(nxt_pieces, u))
                if pending is not None:
                    orows.append(out_rows(pending[1][u]))
                if b < bb:
                    qs = []
                    for j in range(npair):
                        skk = jnp.where(lo, sums[j][0], sums[j][1])
                        s = st[j] * w8[j][u:u + 1] - skk * b8[j][u:u + 1] + vcols[u][j] * kt8[j][u:u + 1]
                        st[j] = s
                        q = s * r8[j][u:u + 1]
                        qh = q.astype(BF16)
                        qs += [qh, (q - qh.astype(F32)).astype(BF16)]
                    qsteps.append(qs)
            if pending is not None:
                pb = pending[0]
                for j in range(npair):
                    o_ref[pb, rows, j * LANE:(j + 1) * LANE] = jnp.concatenate([orows[u][j] for u in range(SUBLANE)],
                                                                               axis=0)
            if b < bb:
                for j in range(npair):
                    s_ref[b, j] = st[j]
                pending = (b, qsteps)
                vcols = nxt_vcols
                nxt_pieces = None
        return carry

    lax.fori_loop(0, tc // SUBLANE, group, 0)

    @pl.when(c == pl.num_programs(1) - 1)
    def _():
        sf_ref[...] = s_ref[...]


def _pair_state(s):
    b = s.shape[0]
    return (s.reshape(b, RWKV_HEADS // 2, 2, HEAD_DIM, HEAD_DIM)
            .transpose(0, 1, 3, 2, 4).reshape(b, RWKV_HEADS // 2, HEAD_DIM, LANE))


def _unpair_state(s):
    b = s.shape[0]
    return (s.reshape(b, RWKV_HEADS // 2, HEAD_DIM, 2, HEAD_DIM)
            .transpose(0, 1, 3, 2, 4).reshape(b, RWKV_HEADS, HEAD_DIM, HEAD_DIM))


def _wkv(r, w, kt, kk, bv, v, s0, bb, tc):
    bsz, tp, _ = r.shape
    tc = min(tc, tp)
    tok = pl.BlockSpec((bb, tc, RWKV_WIDTH), lambda i, c: (i, c, 0))
    st = pl.BlockSpec((bb, RWKV_HEADS // 2, HEAD_DIM, LANE), lambda i, c: (i, 0, 0, 0))
    o, sf = pl.pallas_call(
        functools.partial(_wkv_kernel, bb, tc),
        grid=(bsz // bb, tp // tc),
        in_specs=[tok] * 6 + [st],
        out_specs=[tok, st],
        out_shape=[jax.ShapeDtypeStruct((bsz, tp, RWKV_WIDTH), F32),
                   jax.ShapeDtypeStruct((bsz, RWKV_HEADS // 2, HEAD_DIM, LANE), F32)],
        scratch_shapes=[pltpu.VMEM((bb, RWKV_HEADS // 2, HEAD_DIM, LANE), F32)],
        compiler_params=_cparams("parallel", "arbitrary"),
        name="wkv_scan",
    )(r, w, kt, kk, bv, v, _pair_state(s0))
    return o, _unpair_state(sf)


def _rope(x, c, sa, sb):
    w = x.shape[1]
    return x * c + pltpu.roll(x, w - ROT_HALF, 1) * sa + pltpu.roll(x, ROT_HALF, 1) * sb


def _attn_prep_kernel(p_ref, c_ref, sa_ref, sb_ref, g_ref, b_ref, q_o, k_o, qi_o, ki_o):
    c1, sa1, sb1 = c_ref[...], sa_ref[...], sb_ref[...]
    rep = lambda t, n: jnp.concatenate([t] * n, axis=1)
    nq = ATTN_WIDTH // LANE
    nk = KV_W // LANE
    q_o[...] = _rope(p_ref[:, 0:ATTN_WIDTH], rep(c1, nq), rep(sa1, nq), rep(sb1, nq))
    k_o[...] = _rope(p_ref[:, ATTN_WIDTH:ATTN_WIDTH + KV_W], rep(c1, nk), rep(sa1, nk), rep(sb1, nk))
    qi0 = ATTN_WIDTH + 2 * KV_W
    qi_o[...] = _rope(p_ref[:, qi0:qi0 + IDX_HEADS * IDX_DIM], rep(c1, nq), rep(sa1, nq), rep(sb1, nq))
    x = p_ref[:, KI_OFF:KI_OFF + LANE]
    lane = lax.broadcasted_iota(I32, x.shape, 1)
    isk = lane < IDX_DIM
    mu = jnp.sum(jnp.where(isk, x, 0.0), axis=-1, keepdims=True) * (1.0 / IDX_DIM)
    xc = jnp.where(isk, x - mu, 0.0)
    var = jnp.sum(xc * xc, axis=-1, keepdims=True) * (1.0 / IDX_DIM)
    y = xc * lax.rsqrt(var + LN_EPS) * g_ref[...] + b_ref[...]
    y = _rope(y, jnp.where(isk, c1, 1.0), jnp.where(isk, sa1, 0.0), jnp.where(isk, sb1, 0.0))
    ki_o[...] = jnp.where(isk, y, x)


def _attn_prep(p_at, tabs, idx_g, idx_b, tm):
    n = p_at.shape[0]
    tm = min(tm, n)
    tpb = tabs[0].shape[0] // tm
    tok = lambda w: pl.BlockSpec((tm, w), lambda i: (i, 0))
    tab = pl.BlockSpec((tm, LANE), lambda i: (i % tpb, 0))
    row = pl.BlockSpec((1, LANE), lambda i: (0, 0))
    return pl.pallas_call(
        _attn_prep_kernel,
        grid=(n // tm,),
        in_specs=[tok(AT_PAD_W), tab, tab, tab, row, row],
        out_specs=[tok(ATTN_WIDTH), tok(KV_W), tok(IDX_HEADS * IDX_DIM), tok(LANE)],
        out_shape=[jax.ShapeDtypeStruct((n, ATTN_WIDTH), F32), jax.ShapeDtypeStruct((n, KV_W), F32),
                   jax.ShapeDtypeStruct((n, IDX_HEADS * IDX_DIM), F32), jax.ShapeDtypeStruct((n, LANE), F32)],
        compiler_params=_cparams("parallel"),
        name="attn_prep",
    )(p_at, tabs[0], tabs[1], tabs[2], idx_g, idx_b)


def _rope_tables(pos):
    inv = ROPE_THETA ** (-jnp.arange(ROT_HALF, dtype=F32) * 2.0 / ROT)
    ang = pos.astype(F32)[:, None] * inv[None, :]
    cos, sin = jnp.cos(ang), jnp.sin(ang)
    n = pos.shape[0]
    rest = HEAD_DIM - ROT
    c = jnp.concatenate([cos, cos, jnp.ones((n, rest), F32)], axis=1)
    sa = jnp.concatenate([-sin, jnp.zeros((n, rest + ROT_HALF), F32)], axis=1)
    sb = jnp.concatenate([jnp.zeros((n, ROT_HALF), F32), sin, jnp.zeros((n, rest), F32)], axis=1)
    two = lambda t: jnp.concatenate([t, t], axis=1)
    return two(c), two(sa), two(sb)


def _select_bounds(key_ref, n_tiles, rows, kt, k_sel, idx_bits):
    def count(pred):
        def body(i, acc):
            off = pl.multiple_of(i * kt, kt)
            idx = off + lax.broadcasted_iota(I32, (rows, kt), 1)
            hit = jnp.where(pred(key_ref[:, pl.ds(off, kt)], idx), 1.0, 0.0)
            for c in range(kt // LANE):
                acc = acc + hit[:, c * LANE:(c + 1) * LANE]
            return acc
        acc = lax.fori_loop(0, n_tiles, body, jnp.zeros((rows, LANE), F32))
        return jnp.sum(acc, axis=-1, keepdims=True)

    def thr_more(c):
        i, _, n_ge = c
        return (i < 32) & (jnp.max(jnp.abs(n_ge - k_sel)) > 0.0)

    def thr_bit(c):
        i, res, n_ge = c
        cand = res + jnp.left_shift(jnp.int32(1), 31 - i)
        cnt = count(lambda key, idx: key >= cand)
        take = cnt >= k_sel
        return i + 1, jnp.where(take, cand, res), jnp.where(take, cnt, n_ge)

    n_all = jnp.full((rows, 1), 1.0, F32) * (n_tiles * kt)
    _, thr, n_ge = lax.while_loop(thr_more, thr_bit, (jnp.int32(0), jnp.full((rows, 1), INT_MIN, I32), n_all))

    def tie_search():
        need = k_sel - count(lambda key, idx: key > thr)

        def idx_bit(i, res):
            cand = res | jnp.left_shift(jnp.int32(1), idx_bits - 1 - i)
            c = count(lambda key, idx: (key == thr) & (idx < cand))
            return jnp.where(c < need, cand, res)

        return lax.fori_loop(0, idx_bits, idx_bit, jnp.zeros((rows, 1), I32))

    jmax = lax.cond(jnp.max(n_ge) > k_sel, tie_search,
                    lambda: jnp.full((rows, 1), 2 ** idx_bits - 1, I32))
    return thr, jmax


def _dsa_prompt_kernel(kt, n_sel, idx_bits, q_ref, qi_ref, wi_ref, k_ref, v_ref, ki_ref, o_ref,
                       key_ref, m_ref, l_ref, acc_ref):
    i = pl.program_id(1)
    n_kt = (i * Q_BLOCK + Q_BLOCK + kt - 1) // kt
    qi = qi_ref[0]
    qis = jnp.concatenate([qi[:, h * IDX_DIM:(h + 1) * IDX_DIM] for h in range(IDX_HEADS)], axis=0)
    wi = wi_ref[0] * IDX_HEADS ** -0.5
    qpos = i * Q_BLOCK + lax.broadcasted_iota(I32, (Q_BLOCK, kt), 0)
    lane = lax.broadcasted_iota(I32, (Q_BLOCK, kt), 1)

    def scores(t, carry):
        off = pl.multiple_of(t * kt, kt)
        s = lax.dot_general(qis, ki_ref[0, pl.ds(off, kt), :], _NT, preferred_element_type=F32)
        acc = jnp.zeros((Q_BLOCK, kt), F32)
        for h in range(IDX_HEADS):
            acc = acc + wi[:, h:h + 1] * jnp.maximum(s[h * Q_BLOCK:(h + 1) * Q_BLOCK] * IDX_DIM ** -0.5, 0.0)
        key_ref[:, pl.ds(off, kt)] = jnp.where(off + lane <= qpos, _f2key(acc), INT_MIN)
        return carry

    lax.fori_loop(0, n_kt, scores, 0)
    thr, jmax = _select_bounds(key_ref, n_kt, Q_BLOCK, kt, n_sel, idx_bits)

    q = q_ref[0].astype(F32) * HEAD_DIM ** -0.5
    grp = lax.broadcasted_iota(I32, (Q_BLOCK, KV_W), 1) >> 6
    rep = ATTN_HEADS // KV_HEADS

    def expand(h):
        qh = q[:, h * HEAD_DIM:(h + 1) * HEAD_DIM]
        return jnp.where(grp == h // rep, jnp.concatenate([qh] * KV_HEADS, axis=1), 0.0).astype(BF16)

    qe = [expand(h) for h in range(ATTN_HEADS)]
    m_ref[...] = jnp.full(m_ref.shape, NEG_BIG, F32)
    l_ref[...] = jnp.zeros(l_ref.shape, F32)
    acc_ref[...] = jnp.zeros(acc_ref.shape, F32)

    def attend(t, carry):
        off = pl.multiple_of(t * kt, kt)
        key = key_ref[:, pl.ds(off, kt)]
        kidx = off + lane
        sel = (kidx <= qpos) & ((key > thr) | ((key == thr) & (kidx <= jmax)))
        kt_tile = k_ref[0, pl.ds(off, kt), :]
        vt_tile = v_ref[0, pl.ds(off, kt), :]
        qk = lambda h: lax.dot_general(qe[h], kt_tile, _NT, preferred_element_type=F32)

        def finish(h, p, alpha):
            acc_ref[h] = alpha * acc_ref[h] + jnp.dot(p, vt_tile, preferred_element_type=F32)

        nxt = qk(0)
        pending = None
        for h in range(ATTN_HEADS):
            lg = nxt
            if h + 1 < ATTN_HEADS:
                nxt = qk(h + 1)
            lg = jnp.where(sel, lg, NEG_BIG)
            m = m_ref[h]
            mn = jnp.maximum(m, jnp.max(lg, axis=-1, keepdims=True))
            p = jnp.where(sel, jnp.exp(lg - mn), 0.0)
            alpha = jnp.exp(m - mn)
            l_ref[h] = alpha * l_ref[h] + jnp.sum(p, axis=-1, keepdims=True)
            m_ref[h] = mn
            if pending is not None:
                finish(*pending)
            pending = (h, p.astype(BF16), alpha)
        finish(*pending)
        return carry

    lax.fori_loop(0, n_kt, attend, 0)
    pieces = []
    for h in range(ATTN_HEADS):
        g = h // rep
        pieces.append(acc_ref[h][:, g * HEAD_DIM:(g + 1) * HEAD_DIM] / l_ref[h])
    o_ref[0] = jnp.concatenate(pieces, axis=1)


def _dsa_prompt(q, qi, wi, k, v, ki, t_real):
    bsz, tp, _ = q.shape
    kt = 640 if tp % 640 == 0 else LANE
    n_sel = min(TOPK_MAX, t_real // 4)
    idx_bits = max(1, (tp - 1).bit_length())
    blk = lambda w: pl.BlockSpec((1, Q_BLOCK, w), lambda b, i: (b, i, 0))
    seq = lambda w: pl.BlockSpec((1, tp, w), lambda b, i: (b, 0, 0))
    return pl.pallas_call(
        functools.partial(_dsa_prompt_kernel, kt, n_sel, idx_bits),
        grid=(bsz, tp // Q_BLOCK),
        in_specs=[blk(ATTN_WIDTH), blk(IDX_HEADS * IDX_DIM), blk(IDX_HEADS),
                  seq(KV_W), seq(KV_W), seq(IDX_DIM)],
        out_specs=blk(ATTN_WIDTH),
        out_shape=jax.ShapeDtypeStruct((bsz, tp, ATTN_WIDTH), F32),
        scratch_shapes=[pltpu.VMEM((Q_BLOCK, tp), I32),
                        pltpu.VMEM((ATTN_HEADS, Q_BLOCK, 1), F32),
                        pltpu.VMEM((ATTN_HEADS, Q_BLOCK, 1), F32),
                        pltpu.VMEM((ATTN_HEADS, Q_BLOCK, KV_W), F32)],
        compiler_params=_cparams("parallel", "arbitrary"),
        name="dsa_prompt",
    )(q, qi, wi, k, v, ki)


def _dsa_s_scores_kernel(pps, pt_ref, qi_ref, wi_ref, *refs):
    ci_refs, o_ref = refs[:pps], refs[pps]
    qi = qi_ref[0]
    w = wi_ref[0] * IDX_HEADS ** -0.5
    ki = jnp.concatenate([ci_refs[u][0, 0] for u in range(pps)], axis=0)
    s = _dot3(qi, ki, _NT)
    o_ref[0] = jnp.sum(w * jnp.maximum(s * IDX_DIM ** -0.5, 0.0), axis=0, keepdims=True)


def _dsa_s_scores(page_table, qi3, wi3, cache_idx, layer):
    bd, npages = page_table.shape
    pps = PAGES_PER_STEP
    page = lambda u: pl.BlockSpec((1, 1, PAGE_SIZE, IDX_DIM),
                                  lambda b, p, pt: (layer, pt[b * npages + p * pps + u], 0, 0))
    gs = pltpu.PrefetchScalarGridSpec(
        num_scalar_prefetch=1,
        grid=(bd, npages // pps),
        in_specs=[pl.BlockSpec((1, IDX_HEADS, IDX_DIM), lambda b, p, pt: (b, 0, 0)),
                  pl.BlockSpec((1, IDX_HEADS, 1), lambda b, p, pt: (b, 0, 0))] + [page(u) for u in range(pps)],
        out_specs=pl.BlockSpec((1, 1, pps * PAGE_SIZE), lambda b, p, pt: (b, 0, p)),
    )
    return pl.pallas_call(
        functools.partial(_dsa_s_scores_kernel, pps),
        grid_spec=gs,
        out_shape=jax.ShapeDtypeStruct((bd, 1, npages * PAGE_SIZE), F32),
        compiler_params=_cparams("parallel", "arbitrary"),
        name="dsa_decode_scores",
    )(page_table.reshape(-1), qi3, wi3, *([cache_idx] * pps))


def _dsa_s_bounds_kernel(past, n_sel, idx_bits, sc_ref, qi_ref, kiw_ref, hm_ref, thr_o, j_o, kn_o, key_ref):
    rows = sc_ref.shape[0]
    kiw = kiw_ref[...]
    lane = lax.broadcasted_iota(I32, kiw.shape, 1)
    rolled = pltpu.roll(kiw, IDX_DIM, 1)
    ki2 = jnp.where(lane < IDX_DIM, kiw, rolled)
    w8 = jnp.where(lane < IDX_HEADS, rolled, 0.0) * IDX_HEADS ** -0.5
    prod = qi_ref[...] * jnp.concatenate([ki2] * (IDX_HEADS // 2), axis=1)
    s = _dot3(prod, hm_ref[...])
    new = jnp.sum(w8 * jnp.maximum(s * IDX_DIM ** -0.5, 0.0), axis=-1, keepdims=True)
    knew = _f2key(new)
    key_ref[:, 0:past] = _f2key(sc_ref[...])
    key_ref[:, past:past + LANE] = jnp.where(lane == 0, knew, INT_MIN)
    thr, jmax = _select_bounds(key_ref, (past + LANE) // LANE, rows, LANE, n_sel, idx_bits)
    thr_o[...] = thr
    j_o[...] = jmax
    kn_o[...] = knew


def _dsa_s_bounds(scores, qi, kiw, hm, past):
    rows = scores.shape[0]
    n_sel = min(TOPK_MAX, (past + 1) // 4)
    idx_bits = (past + LANE - 1).bit_length()
    out = jax.ShapeDtypeStruct((rows, 1), I32)
    return pl.pallas_call(
        functools.partial(_dsa_s_bounds_kernel, past, n_sel, idx_bits),
        out_shape=[out, out, out],
        scratch_shapes=[pltpu.VMEM((rows, past + LANE), I32)],
        compiler_params=pltpu.CompilerParams(vmem_limit_bytes=VMEM_LIMIT),
        name="dsa_decode_bounds",
    )(scores, qi, kiw, hm)


def _dsa_s_attend_kernel(pps, past, pt_ref, thr_ref, j_ref, kn_ref, qe_ref, sc_ref, kn_row_ref, vn_row_ref,
                         fold_ref, *refs):
    ck, cv, o_ref = refs[:pps], refs[pps:2 * pps], refs[2 * pps]
    m_ref, l_ref, acc_ref = refs[2 * pps + 1:]
    b = pl.program_id(0)
    p = pl.program_id(1)

    @pl.when(p == 0)
    def _():
        m_ref[...] = jnp.full(m_ref.shape, NEG_BIG, F32)
        l_ref[...] = jnp.zeros(l_ref.shape, F32)
        acc_ref[...] = jnp.zeros(acc_ref.shape, F32)

    thr, jmax = thr_ref[b], j_ref[b]
    qe = qe_ref[0]
    qeb = qe.astype(BF16)
    kidx = p * pps * PAGE_SIZE + lax.broadcasted_iota(I32, (1, pps * PAGE_SIZE), 1)
    kb = jnp.concatenate([ck[u][0, 0].astype(BF16) for u in range(pps)], axis=0)
    vb = jnp.concatenate([cv[u][0, 0].astype(BF16) for u in range(pps)], axis=0)
    lg = lax.dot_general(qeb, kb, _NT, preferred_element_type=F32) * HEAD_DIM ** -0.5
    key = _f2key(sc_ref[0])
    sel = (key > thr) | ((key == thr) & (kidx <= jmax))
    lg = jnp.where(sel, lg, NEG_BIG)
    m = m_ref[...]
    mn = jnp.maximum(m, jnp.max(lg, axis=-1, keepdims=True))
    pr = jnp.where(sel, jnp.exp(lg - mn), 0.0)
    alpha = jnp.exp(m - mn)
    l_ref[...] = alpha * l_ref[...] + jnp.sum(pr, axis=-1, keepdims=True)
    acc_ref[...] = alpha * acc_ref[...] + jnp.dot(pr.astype(BF16), vb, preferred_element_type=F32)
    m_ref[...] = mn

    @pl.when(p == pl.num_programs(1) - 1)
    def _():
        knew = kn_ref[b]
        sel_new = (knew > thr) | ((knew == thr) & (past <= jmax))
        lg = jnp.sum(qe * kn_row_ref[0], axis=-1, keepdims=True) * HEAD_DIM ** -0.5
        m = m_ref[...]
        mn = jnp.where(sel_new, jnp.maximum(m, lg), m)
        pr = jnp.where(sel_new, jnp.exp(lg - mn), 0.0)
        alpha = jnp.exp(m - mn)
        l = alpha * l_ref[...] + pr
        acc = alpha * acc_ref[...] + pr * vn_row_ref[0]
        rowi = lax.broadcasted_iota(I32, acc.shape, 0)
        lanei = lax.broadcasted_iota(I32, acc.shape, 1)
        own = jnp.where((lanei >> 6) == (rowi >> 1), acc / l, 0.0)
        o_ref[0] = _dot_sel(own, fold_ref[...])


def _dsa_s_attend(page_table, thr, jmax, knew, qe, scores, k_new, v_new, fold, cache_k, cache_v, layer):
    bd, npages = page_table.shape
    past = npages * PAGE_SIZE
    pps = PAGES_PER_STEP
    page = lambda u: pl.BlockSpec((1, 1, PAGE_SIZE, KV_W),
                                  lambda b, p, pt, t, j, kn: (layer, pt[b * npages + p * pps + u], 0, 0))
    per_b = lambda s: pl.BlockSpec((1,) + s, lambda b, p, pt, t, j, kn: (b, 0, 0))
    gs = pltpu.PrefetchScalarGridSpec(
        num_scalar_prefetch=4,
        grid=(bd, npages // pps),
        in_specs=[per_b((ATTN_HEADS, KV_W)),
                  pl.BlockSpec((1, 1, pps * PAGE_SIZE), lambda b, p, pt, t, j, kn: (b, 0, p)),
                  per_b((1, KV_W)), per_b((1, KV_W)),
                  pl.BlockSpec(fold.shape, lambda b, p, pt, t, j, kn: (0, 0))]
                 + [page(u) for u in range(pps)] * 2,
        out_specs=per_b((ATTN_HEADS, HEAD_DIM)),
        scratch_shapes=[pltpu.VMEM((ATTN_HEADS, 1), F32), pltpu.VMEM((ATTN_HEADS, 1), F32),
                        pltpu.VMEM((ATTN_HEADS, KV_W), F32)],
    )
    return pl.pallas_call(
        functools.partial(_dsa_s_attend_kernel, pps, past),
        grid_spec=gs,
        out_shape=jax.ShapeDtypeStruct((bd, ATTN_HEADS, HEAD_DIM), F32),
        compiler_params=_cparams("parallel", "arbitrary"),
        name="dsa_decode_attend",
    )(page_table.reshape(-1), thr.reshape(-1), jmax.reshape(-1), knew.reshape(-1),
      qe, scores, k_new, v_new, fold, *([cache_k] * pps), *([cache_v] * pps))


def _merge_ln_kernel(o_ref, bonus_ref, g_ref, at_ref, x_ref, wt_ref, wb_ref, xg_ref, xb_ref,
                     lg_ref, lb_ref, bd_ref, out_ref):
    bd = bd_ref[...]
    o = o_ref[...]
    mean = _dot_sel(o, bd) * (1.0 / HEAD_DIM)
    oc = o - mean
    var = _dot_sel(oc * oc, bd) * (1.0 / HEAD_DIM)
    rw = (oc * lax.rsqrt(var + GN_EPS) * xg_ref[...] + xb_ref[...] + bonus_ref[...]) * g_ref[...]
    f = (jnp.dot(rw.astype(BF16), wt_ref[...], preferred_element_type=F32)
         + jnp.dot(at_ref[...].astype(BF16), wb_ref[...], preferred_element_type=F32))
    out_ref[...] = _layer_norm(DEEPNORM_ALPHA * x_ref[...] + f, lg_ref[...], lb_ref[...])


def _merge_ln(o, bonus, g, at, x, wts, tm):
    n = x.shape[0]
    tm = min(tm, n)
    tok = lambda w: pl.BlockSpec((tm, w), lambda i: (i, 0))
    full = lambda a: pl.BlockSpec(a.shape, lambda i: (0, 0))
    ws = [wts["wo_top"], wts["wo_bot"], wts["lnx_g"], wts["lnx_b"], wts["ln1_g"], wts["ln1_b"], wts["bd"]]
    return pl.pallas_call(
        _merge_ln_kernel,
        grid=(n // tm,),
        in_specs=[tok(RWKV_WIDTH)] * 3 + [tok(ATTN_WIDTH), tok(D_MODEL)] + [full(a) for a in ws],
        out_specs=tok(D_MODEL),
        out_shape=jax.ShapeDtypeStruct((n, D_MODEL), F32),
        compiler_params=_cparams("parallel"),
        name="merge_ln",
    )(o, bonus, g, at, x, *ws)


def _take_top(src_ref, n_rows, val_ref, idx_ref):
    shape = src_ref.shape
    row = lax.broadcasted_iota(I32, shape, 1)

    def body(a, carry):
        sv = src_ref[...]
        m = jnp.max(sv, axis=1, keepdims=True)
        idx = jnp.min(jnp.where(sv == m, row, n_rows), axis=1, keepdims=True)
        val_ref[:, pl.ds(a, 1), :] = m
        idx_ref[:, pl.ds(a, 1), :] = idx
        src_ref[...] = jnp.where(row == idx, -jnp.inf, sv)
        return carry

    lax.fori_loop(0, PEER_TOPK, body, 0)


def _peer_route_kernel(x_ref, wq_ref, sk0_ref, sk1_ref, e_o, g_o, s_ref, t_ref, i_ref, c_ref, ts_ref, ic_ref):
    q = jnp.dot(x_ref[...].astype(BF16), wq_ref[...], preferred_element_type=F32)
    for h in range(PEER_HEADS):
        qh = q[:, h * PEER_DKEY:(h + 1) * PEER_DKEY]
        s_ref[h] = _dot3(sk0_ref[...], qh, _NT)
        s_ref[PEER_HEADS + h] = _dot3(sk1_ref[...], qh, _NT)
    _take_top(s_ref, N_KEYS, t_ref, i_ref)
    t1, t2 = t_ref[0:PEER_HEADS], t_ref[PEER_HEADS:2 * PEER_HEADS]
    c_ref[...] = jnp.concatenate([t1[:, a:a + 1, :] + t2 for a in range(PEER_TOPK)], axis=1)
    _take_top(c_ref, PEER_TOPK * PEER_TOPK, ts_ref, ic_ref)
    ic = ic_ref[...]
    i1, i2 = i_ref[0:PEER_HEADS], i_ref[PEER_HEADS:2 * PEER_HEADS]
    ia, ib = ic >> 4, ic & (PEER_TOPK - 1)
    e = jnp.zeros(ic.shape, I32)
    for a in range(PEER_TOPK):
        e = e + jnp.where(ia == a, i1[:, a:a + 1, :] * N_KEYS, 0) + jnp.where(ib == a, i2[:, a:a + 1, :], 0)
    ts = ts_ref[...]
    ex = jnp.exp(ts - jnp.max(ts, axis=1, keepdims=True))
    e_o[0] = e
    g_o[0] = ex / jnp.sum(ex, axis=1, keepdims=True)


def _peer_route(x, wq, sk0, sk1, tt):
    n = x.shape[0]
    nt = n // tt
    full = lambda a: pl.BlockSpec(a.shape, lambda i: (0, 0))
    out = pl.BlockSpec((1, PEER_HEADS, PEER_TOPK, tt), lambda i: (i, 0, 0, 0))
    return pl.pallas_call(
        _peer_route_kernel,
        grid=(nt,),
        in_specs=[pl.BlockSpec((tt, D_MODEL), lambda i: (i, 0)), full(wq), full(sk0), full(sk1)],
        out_specs=[out, out],
        out_shape=[jax.ShapeDtypeStruct((nt, PEER_HEADS, PEER_TOPK, tt), I32),
                   jax.ShapeDtypeStruct((nt, PEER_HEADS, PEER_TOPK, tt), F32)],
        scratch_shapes=[pltpu.VMEM((2 * PEER_HEADS, N_KEYS, tt), F32),
                        pltpu.VMEM((2 * PEER_HEADS, PEER_TOPK, tt), F32),
                        pltpu.VMEM((2 * PEER_HEADS, PEER_TOPK, tt), I32),
                        pltpu.VMEM((PEER_HEADS, PEER_TOPK * PEER_TOPK, tt), F32),
                        pltpu.VMEM((PEER_HEADS, PEER_TOPK, tt), F32),
                        pltpu.VMEM((PEER_HEADS, PEER_TOPK, tt), I32)],
        compiler_params=_cparams("parallel"),
        name="peer_route",
    )(x, wq, sk0, sk1)


def _peer_gather_kernel(tt, n_valid, idx_hbm, x_ref, g_ref, uv_hbm, lg_ref, lb_ref, o_ref, idx_smem, *rest):
    bufs, (sem, isem, y_ref) = rest[:GATHER_BUFFERS], rest[GATHER_BUFFERS:]
    i = pl.program_id(0)
    n_idx = PEER_SLOTS * tt
    islot = i % 2

    def idx_copy(tile, slot):
        return pltpu.make_async_copy(idx_hbm.at[tile], idx_smem.at[pl.ds(slot * n_idx, n_idx)], isem.at[slot])

    @pl.when(i == 0)
    def _():
        idx_copy(0, 0).start()

    if n_valid < tt:
        y_ref[...] = jnp.zeros(y_ref.shape, F32)
    idx_copy(i, islot).wait()

    @pl.when(i + 1 < pl.num_programs(0))
    def _():
        idx_copy(i + 1, 1 - islot).start()

    base = islot * n_idx

    def issue(t, k):
        tok = base + t * PEER_SLOTS
        for s in range(PEER_SLOTS):
            e = idx_smem[tok + s]
            pltpu.async_copy(uv_hbm.at[pl.ds(e, 1)], bufs[k].at[pl.ds(s, 1)], sem.at[k], priority=s % 2)

    def wait(k):
        pltpu.make_async_copy(uv_hbm.at[pl.ds(0, PEER_SLOTS)], bufs[k], sem.at[k]).wait()

    lane = lax.broadcasted_iota(I32, (PEER_SLOTS, tt), 1)

    def compute(t, k):
        xrow = x_ref[pl.ds(t, 1), :]
        word = bufs[k][...]
        u = lax.bitcast_convert_type(word << 16, F32)
        v = lax.bitcast_convert_type(word & jnp.int32(-65536), F32)
        h = jnp.sum(u * xrow, axis=-1, keepdims=True)
        gate = jnp.sum(jnp.where(lane == t, g_ref[0], 0.0), axis=-1, keepdims=True)
        coef = gate * jax.nn.gelu(h)
        y_ref[pl.ds(t, 1), :] = jnp.sum(coef * v, axis=0, keepdims=True)

    depth = len(bufs)
    ahead = depth - 1
    rounds = max(n_valid - ahead, 0) // depth
    for t in range(min(ahead, n_valid)):
        issue(t, t % depth)

    def body(r, carry):
        for k in range(depth):
            t = r * depth + k
            wait(k)
            issue(t + ahead, (k + ahead) % depth)
            compute(t, k)
        return carry

    lax.fori_loop(0, rounds, body, 0)
    for t in range(rounds * depth, n_valid):
        wait(t % depth)
        if t + ahead < n_valid:
            issue(t + ahead, (t + ahead) % depth)
        compute(t, t % depth)
    o_ref[...] = _layer_norm(DEEPNORM_ALPHA * x_ref[...] + y_ref[...], lg_ref[...], lb_ref[...])


def _peer_gather(idx, gates, x, uv, ln_g, ln_b, tt, n_valid):
    n = x.shape[0]
    nt = n // tt
    row = pl.BlockSpec((1, D_MODEL), lambda i: (0, 0))
    return pl.pallas_call(
        functools.partial(_peer_gather_kernel, tt, n_valid),
        grid=(nt,),
        in_specs=[pl.BlockSpec(memory_space=pl.ANY),
                  pl.BlockSpec((tt, D_MODEL), lambda i: (i, 0)),
                  pl.BlockSpec((1, PEER_SLOTS, tt), lambda i: (i, 0, 0)),
                  pl.BlockSpec(memory_space=pl.ANY), row, row],
        out_specs=pl.BlockSpec((tt, D_MODEL), lambda i: (i, 0)),
        out_shape=jax.ShapeDtypeStruct((n, D_MODEL), F32),
        scratch_shapes=[pltpu.SMEM((2 * PEER_SLOTS * tt,), I32)]
                       + [pltpu.VMEM((PEER_SLOTS, D_MODEL), I32)] * GATHER_BUFFERS
                       + [pltpu.SemaphoreType.DMA((GATHER_BUFFERS,)),
                          pltpu.SemaphoreType.DMA((2,)),
                          pltpu.VMEM((tt, D_MODEL), F32)],
        compiler_params=_cparams("arbitrary"),
        name="peer_gather",
    )(idx, x, gates, uv, ln_g, ln_b)


def _peer(x, wts, tt, n_valid):
    e, gates = _peer_route(x, wts["wq"], wts["sk0"], wts["sk1"], tt)
    nt = x.shape[0] // tt
    e_tok = e.reshape(nt, PEER_SLOTS, tt).transpose(0, 2, 1)
    return _peer_gather(e_tok.reshape(nt, PEER_SLOTS * tt), gates.reshape(nt, PEER_SLOTS, tt), x,
                        wts["uv"], wts["ln2_g"], wts["ln2_b"], tt, n_valid)


def _pack_bf16_pair(lo, hi):
    bits = lambda a: lax.bitcast_convert_type(a.astype(BF16), jnp.uint16).astype(jnp.uint32)
    return lax.bitcast_convert_type(bits(lo) | (bits(hi) << 16), I32)


def _layer_weights(l, w_in, shift_mu, decay_w0, decay_up, iclr_a0, iclr_up, gate_up, k_k, k_a, r_k,
                   lnx_g, lnx_b, idx_ln_g, idx_ln_b, w_out, ln1_g, ln1_b, ln2_g, ln2_b,
                   peer_wq, peer_subkeys, peer_u, peer_v):
    row = lambda a: a.reshape(1, -1).astype(F32)
    lora = jnp.zeros((LORA_W, 3 * RWKV_WIDTH), F32)
    lora = lora.at[0:W_LORA, 0:RWKV_WIDTH].set(decay_up[l])
    lora = lora.at[W_LORA:W_LORA + A_LORA, RWKV_WIDTH:2 * RWKV_WIDTH].set(iclr_up[l])
    lora = lora.at[W_LORA + A_LORA:, 2 * RWKV_WIDTH:].set(gate_up[l])
    lora_hi = lora.astype(BF16)
    pad_lane = lambda a: jnp.pad(a.reshape(1, -1), ((0, 0), (0, LANE - a.size)))
    half = PEER_DKEY // 2
    seg = jnp.arange(RWKV_WIDTH) // HEAD_DIM
    return dict(
        w_rw=w_in[l][:, :SHIFT_W].astype(BF16),
        w_at=jnp.pad(w_in[l][:, SHIFT_W:], ((0, 0), (0, AT_PAD_W - ATTN_PROJ_W))).astype(BF16),
        mu=row(shift_mu[l]), w0=row(decay_w0[l]), a0=row(iclr_a0[l]),
        lora_hi=lora_hi, lora_lo=(lora - lora_hi.astype(F32)).astype(BF16),
        k_k=row(k_k[l]), k_a=row(k_a[l]), r_k=row(r_k[l]),
        bd=(seg[:, None] == seg[None, :]).astype(BF16),
        lnx_g=row(lnx_g[l]), lnx_b=row(lnx_b[l]),
        idx_g=pad_lane(idx_ln_g[l]), idx_b=pad_lane(idx_ln_b[l]),
        wo_top=w_out[l][:RWKV_WIDTH].astype(BF16), wo_bot=w_out[l][RWKV_WIDTH:].astype(BF16),
        ln1_g=row(ln1_g[l]), ln1_b=row(ln1_b[l]), ln2_g=row(ln2_g[l]), ln2_b=row(ln2_b[l]),
        wq=peer_wq[l].astype(BF16),
        sk0=jnp.pad(peer_subkeys[l, 0], ((0, 0), (0, half))),
        sk1=jnp.pad(peer_subkeys[l, 1], ((0, 0), (half, 0))),
        uv=_pack_bf16_pair(peer_u[l], peer_v[l]),
    )


def _tile(n, cap):
    if n <= cap:
        return n
    return max(d for d in range(8, cap + 1, 8) if n % d == 0)


def _mixer_front(x, wts, tabs):
    tm = _tile(x.shape[0], 512)
    p_rw = _matmul(x, wts["w_rw"], tm)
    p_at = _matmul(x, wts["w_at"], tm)
    q, k, qi, kiw = _attn_prep(p_at, tabs, wts["idx_g"], wts["idx_b"], _tile(tabs[0].shape[0], 640))
    v = p_at[:, ATTN_WIDTH + KV_W:ATTN_WIDTH + 2 * KV_W]
    return p_rw, q, k, v, qi, kiw


def kernel(x_prompt, x_sample, cache_k, cache_v, cache_idx_k, state_wkv, state_shift, page_table, meta_tokens, w_in, shift_mu, decay_w0, decay_up, iclr_a0, iclr_up, gate_up, k_k, k_a, r_k, lnx_g, lnx_b, idx_ln_g, idx_ln_b, w_out, ln1_g, ln1_b, ln2_g, ln2_b, peer_wq, peer_subkeys, peer_u, peer_v):
    bsz, seq, _ = x_prompt.shape
    t = seq + N_META
    tp = -(-t // LANE) * LANE
    n_p = bsz * tp
    depth = w_in.shape[0]
    bd_, ts_, _ = x_sample.shape
    assert ts_ == 1
    npages = page_table.shape[1]
    past = npages * PAGE_SIZE
    n_pool = cache_k.shape[1]
    peer_tt = LANE
    ns_pad = -(-bd_ // peer_tt) * peer_tt

    xp = jnp.concatenate([jnp.broadcast_to(meta_tokens[None], (bsz, N_META, D_MODEL)), x_prompt], axis=1)
    xp = jnp.pad(xp, ((0, 0), (0, tp - t), (0, 0))).reshape(n_p, D_MODEL)
    xs = x_sample.reshape(bd_, D_MODEL)
    tabs_p = _rope_tables(jnp.arange(tp, dtype=I32))
    tabs_s = _rope_tables(jnp.full((bd_,), past, I32))
    ck = cache_k.reshape(depth, n_pool, PAGE_SIZE, KV_W)
    cv = cache_v.reshape(depth, n_pool, PAGE_SIZE, KV_W)
    hsel = jnp.arange(IDX_HEADS * IDX_DIM)[:, None] // IDX_DIM == jnp.arange(LANE)[None, :]
    hm = hsel.astype(F32)
    fold = (jnp.arange(KV_W)[:, None] % HEAD_DIM == jnp.arange(HEAD_DIM)[None, :]).astype(BF16)
    own = (jnp.arange(KV_W)[None, :] // HEAD_DIM == jnp.arange(ATTN_HEADS)[:, None] // (ATTN_HEADS // KV_HEADS))

    k_p, v_p, ki_p, wkv_p, sh_p = [], [], [], [], []
    k_s, v_s, ki_s, wkv_s, sh_s = [], [], [], [], []
    for l in range(depth):
        wts = _layer_weights(l, w_in, shift_mu, decay_w0, decay_up, iclr_a0, iclr_up, gate_up, k_k, k_a, r_k,
                             lnx_g, lnx_b, idx_ln_g, idx_ln_b, w_out, ln1_g, ln1_b, ln2_g, ln2_b,
                             peer_wq, peer_subkeys, peer_u, peer_v)

        p_rw, q, k, v, qi, kiw = _mixer_front(xp, wts, tabs_p)
        p_rw3 = p_rw.reshape(bsz, tp, SHIFT_W)
        r_, w_, kt_, kk_, b_, vv_, bonus, g = _rwkv_prep(
            p_rw3, jnp.zeros((bsz, 1, SHIFT_W), F32), t, True, wts, LANE)
        o, s_fin = _wkv(r_, w_, kt_, kk_, b_, vv_,
                        jnp.zeros((bsz, RWKV_HEADS, HEAD_DIM, HEAD_DIM), F32), bsz, LANE)
        three = lambda a: a.reshape(bsz, tp, -1)
        at = _dsa_prompt(three(q).astype(BF16), three(qi).astype(BF16),
                         three(kiw)[:, :, IDX_DIM:IDX_DIM + IDX_HEADS],
                         three(k).astype(BF16), three(v).astype(BF16),
                         three(kiw)[:, :, :IDX_DIM].astype(BF16), t)
        flat = lambda a: a.reshape(n_p, -1)
        x1 = _merge_ln(flat(o), flat(bonus), flat(g), flat(at), xp, wts, _tile(n_p, 256))
        xp = _peer(x1, wts, peer_tt, peer_tt)
        k_p.append(three(k)[:, :t].reshape(bsz, t, KV_HEADS, HEAD_DIM))
        v_p.append(three(v)[:, :t].reshape(bsz, t, KV_HEADS, HEAD_DIM))
        ki_p.append(three(kiw)[:, :t, :IDX_DIM])
        wkv_p.append(s_fin)
        sh_p.append(p_rw3[:, t - 1])

        p_rw, q, k, v, qi, kiw = _mixer_front(xs, wts, tabs_s)
        r_, w_, kt_, kk_, b_, vv_, bonus, g = _rwkv_prep(
            p_rw[None], state_shift[l][None], bd_, False, wts, bd_)
        tc_s = 8
        padt = lambda a, c: jnp.pad(a[0][:, None, :], ((0, 0), (0, tc_s - 1), (0, 0)), constant_values=c)
        o, s_fin = _wkv(padt(r_, 0.0), padt(w_, 1.0), padt(kt_, 0.0), padt(kk_, 0.0), padt(b_, 0.0),
                        padt(vv_, 0.0), state_wkv[l].astype(F32), 4, tc_s)
        o = o[:, 0]
        scores = _dsa_s_scores(page_table, qi.reshape(bd_, IDX_HEADS, IDX_DIM),
                               kiw[:, IDX_DIM:IDX_DIM + IDX_HEADS].reshape(bd_, IDX_HEADS, 1),
                               cache_idx_k, l)
        thr, jmax, knew = _dsa_s_bounds(scores.reshape(bd_, past), qi, kiw, hm, past)
        qe = jnp.where(own[None], jnp.tile(q.reshape(bd_, ATTN_HEADS, HEAD_DIM), (1, 1, KV_HEADS)), 0.0)
        at = _dsa_s_attend(page_table, thr, jmax, knew, qe, scores, k[:, None, :], v[:, None, :], fold,
                           ck, cv, l).reshape(bd_, ATTN_WIDTH)
        x1 = _merge_ln(o, bonus[0], g[0], at, xs, wts, bd_)
        x1p = jnp.pad(x1, ((0, ns_pad - bd_), (0, 0)))
        xs = _peer(x1p, wts, peer_tt, bd_)[:bd_]
        k_s.append(k.reshape(bd_, 1, KV_HEADS, HEAD_DIM))
        v_s.append(v.reshape(bd_, 1, KV_HEADS, HEAD_DIM))
        ki_s.append(kiw[:, None, :IDX_DIM])
        wkv_s.append(s_fin)
        sh_s.append(p_rw)

    y_prompt = xp.reshape(bsz, tp, D_MODEL)[:, N_META:t]
    y_sample = xs.reshape(bd_, 1, D_MODEL)
    return (y_prompt, y_sample, jnp.stack(k_p), jnp.stack(v_p), jnp.stack(ki_p),
            jnp.stack(wkv_p).astype(state_wkv.dtype), jnp.stack(sh_p).astype(state_shift.dtype),
            jnp.stack(k_s), jnp.stack(v_s), jnp.stack(ki_s),
            jnp.stack(wkv_s).astype(state_wkv.dtype), jnp.stack(sh_s).astype(state_shift.dtype))
```

```python
import functools
import math

import jax
import jax.numpy as jnp
from jax import lax
from jax.experimental import pallas as pl
from jax.experimental.pallas import tpu as pltpu

F32 = jnp.float32
BF16 = jnp.bfloat16
I32 = jnp.int32

D_MODEL = 1024
N_META = 16
HEAD_DIM = 64
RWKV_WIDTH = D_MODEL // 2
RWKV_HEADS = RWKV_WIDTH // HEAD_DIM
ATTN_WIDTH = D_MODEL - RWKV_WIDTH
ATTN_HEADS = ATTN_WIDTH // HEAD_DIM
KV_HEADS = ATTN_HEADS // 2
KV_W = KV_HEADS * HEAD_DIM
W_LORA = 64
A_LORA = 64
G_LORA = 128
LORA_W = W_LORA + A_LORA + G_LORA
SHIFT_W = 3 * RWKV_WIDTH + LORA_W
IDX_HEADS = 8
IDX_DIM = 64
ATTN_PROJ_W = ATTN_WIDTH + 2 * KV_W + IDX_HEADS * IDX_DIM + IDX_DIM + IDX_HEADS
TOPK_MAX = 256
ROPE_THETA = 500000.0
ROT = HEAD_DIM // 4
ROT_HALF = ROT // 2
PEER_HEADS = 8
PEER_DKEY = 128
N_KEYS = 128
PEER_TOPK = 16
PEER_SLOTS = PEER_HEADS * PEER_TOPK
DEPTH = 2
DEEPNORM_ALPHA = (2.0 * DEPTH) ** 0.25
PAGE_SIZE = 128
LN_EPS = 1e-5
GN_EPS = 64e-5

LANE = 128
SUBLANE = 8
Q_BLOCK = 128
INT_MIN = -(2 ** 31)
NEG_BIG = -1e30
VMEM_LIMIT = 56 * 1024 * 1024
AT_PAD_W = 1664
KI_OFF = ATTN_WIDTH + 2 * KV_W + IDX_HEADS * IDX_DIM
PAGES_PER_STEP = 8
GATHER_BUFFERS = 8

_NT = (((1,), (1,)), ((), ()))


def _cparams(*sem):
    return pltpu.CompilerParams(dimension_semantics=sem, vmem_limit_bytes=VMEM_LIMIT)


def _split2(x):
    hi = x.astype(BF16)
    lo = (x - hi.astype(F32)).astype(BF16)
    return hi, lo


def _split3(x):
    hi = x.astype(BF16)
    r1 = x - hi.astype(F32)
    mid = r1.astype(BF16)
    lo = (r1 - mid.astype(F32)).astype(BF16)
    return hi, mid, lo


def _dot3(a, b, dims=None):
    ah, al = _split2(a)
    bh, bl = _split2(b)
    if dims is None:
        d = lambda p, q: jnp.dot(p, q, preferred_element_type=F32)
    else:
        d = lambda p, q: lax.dot_general(p, q, dims, preferred_element_type=F32)
    return d(ah, bh) + d(al, bh) + d(ah, bl)


def _dot_sel(x, m):
    h, mid, lo = _split3(x)
    d = lambda p: jnp.dot(p, m, preferred_element_type=F32)
    return d(h) + d(mid) + d(lo)


def _f2key(x):
    x = jnp.where(x == 0.0, 0.0, x)
    b = lax.bitcast_convert_type(x, I32)
    return b ^ ((b >> 31) & 0x7FFFFFFF)


def _layer_norm(z, g, b):
    mu = jnp.mean(z, axis=-1, keepdims=True)
    zc = z - mu
    var = jnp.mean(zc * zc, axis=-1, keepdims=True)
    return zc * lax.rsqrt(var + LN_EPS) * g + b


def _mm_kernel(x_ref, w_ref, o_ref):
    o_ref[...] = jnp.dot(x_ref[...].astype(BF16), w_ref[...], preferred_element_type=F32)


def _matmul(x, w, tm):
    m, k = x.shape
    n = w.shape[1]
    tm = min(tm, m)
    return pl.pallas_call(
        _mm_kernel,
        grid=(m // tm,),
        in_specs=[pl.BlockSpec((tm, k), lambda i: (i, 0)),
                  pl.BlockSpec((k, n), lambda i: (0, 0))],
        out_specs=pl.BlockSpec((tm, n), lambda i: (i, 0)),
        out_shape=jax.ShapeDtypeStruct((m, n), F32),
        compiler_params=_cparams("parallel"),
        name="proj_matmul",
    )(x, w)


def _rwkv_prep_kernel(t_real, tt, shift, p_ref, prev_ref, mu_ref, w0_ref, a0_ref, lwh_ref, lwl_ref,
                      kk_ref, ka_ref, rk_ref, bd_ref,
                      r_o, w_o, kt_o, kko_o, b_o, v_o, bonus_o, g_o, carry_ref):
    j = pl.program_id(1)
    pf = p_ref[0]
    if shift:
        @pl.when(j == 0)
        def _():
            carry_ref[...] = prev_ref[0]
        row = lax.broadcasted_iota(I32, pf.shape, 0)
        prev = jnp.where(row == 0, carry_ref[...], pltpu.roll(pf, 1, 0))
        carry_ref[...] = pf[tt - 1:tt, :]
    else:
        prev = prev_ref[0]
    xs = pf + mu_ref[...] * (prev - pf)
    r = xs[:, 0:RWKV_WIDTH]
    k = xs[:, RWKV_WIDTH:2 * RWKV_WIDTH]
    v = xs[:, 2 * RWKV_WIDTH:3 * RWKV_WIDTH]
    z = xs[:, 3 * RWKV_WIDTH:SHIFT_W]
    lane = lax.broadcasted_iota(I32, z.shape, 1)
    zt = jnp.where(lane < W_LORA, jnp.tanh(z),
                   jnp.where(lane < W_LORA + A_LORA, z, jax.nn.sigmoid(z)))
    zh, zl = _split2(zt)
    d = lambda p, q: jnp.dot(p, q, preferred_element_type=F32)
    lo = d(zh, lwh_ref[...]) + d(zl, lwh_ref[...]) + d(zh, lwl_ref[...])
    w_raw = w0_ref[...] + lo[:, 0:RWKV_WIDTH]
    a = jax.nn.sigmoid(a0_ref[...] + lo[:, RWKV_WIDTH:2 * RWKV_WIDTH])
    g = lo[:, 2 * RWKV_WIDTH:3 * RWKV_WIDTH]
    decay = jnp.exp(-math.exp(-0.5) * jax.nn.sigmoid(w_raw))
    bd = bd_ref[...]
    kk = k * kk_ref[...]
    kk = kk / jnp.maximum(jnp.sqrt(_dot_sel(kk * kk, bd)), 1e-12)
    kt = k * (1.0 + (a - 1.0) * ka_ref[...])
    bonus = _dot_sel(r * kt * rk_ref[...], bd) * v
    pos = j * tt + lax.broadcasted_iota(I32, r.shape, 0)
    valid = pos < t_real
    r_o[0] = r
    w_o[0] = jnp.where(valid, decay, 1.0)
    kt_o[0] = jnp.where(valid, kt, 0.0)
    kko_o[0] = jnp.where(valid, kk, 0.0)
    b_o[0] = jnp.where(valid, kk * a, 0.0)
    v_o[0] = v
    bonus_o[0] = bonus
    g_o[0] = g


def _rwkv_prep(p_rw, prev, t_real, shift, wts, tt):
    bsz, tp, _ = p_rw.shape
    tt = min(tt, tp)
    row = lambda n: pl.BlockSpec((1, n), lambda b, j: (0, 0))
    full = lambda a: pl.BlockSpec(a.shape, lambda b, j: (0, 0))
    tok = lambda n: pl.BlockSpec((1, tt, n), lambda b, j: (b, j, 0))
    prev_spec = pl.BlockSpec((1, 1, SHIFT_W), lambda b, j: (b, 0, 0)) if shift else tok(SHIFT_W)
    outs = pl.pallas_call(
        functools.partial(_rwkv_prep_kernel, t_real, tt, shift),
        grid=(bsz, tp // tt),
        in_specs=[tok(SHIFT_W), prev_spec, row(SHIFT_W), row(RWKV_WIDTH), row(RWKV_WIDTH),
                  full(wts["lora_hi"]), full(wts["lora_lo"]),
                  row(RWKV_WIDTH), row(RWKV_WIDTH), row(RWKV_WIDTH), full(wts["bd"])],
        out_specs=[tok(RWKV_WIDTH)] * 8,
        out_shape=[jax.ShapeDtypeStruct((bsz, tp, RWKV_WIDTH), F32)] * 8,
        scratch_shapes=[pltpu.VMEM((1, SHIFT_W), F32)],
        compiler_params=_cparams("parallel", "arbitrary"),
        name="rwkv_prep",
    )(p_rw, prev, wts["mu"], wts["w0"], wts["a0"], wts["lora_hi"], wts["lora_lo"],
      wts["k_k"], wts["k_a"], wts["r_k"], wts["bd"])
    return outs


def _wkv_kernel(bb, tc, r_ref, w_ref, kt_ref, kk_ref, b_ref, v_ref, s0_ref, o_ref, sf_ref, s_ref):
    c = pl.program_id(1)

    @pl.when(c == 0)
    def _():
        s_ref[...] = s0_ref[...]

    lane = lax.broadcasted_iota(I32, (HEAD_DIM, LANE), 1)
    row = lax.broadcasted_iota(I32, (HEAD_DIM, LANE), 0)
    lo = lane < HEAD_DIM
    e0 = lane == row
    e1 = lane == row + HEAD_DIM
    e01 = e0 | e1
    r128 = lax.broadcasted_iota(I32, (LANE, LANE), 0)
    l128 = lax.broadcasted_iota(I32, (LANE, LANE), 1)
    half_ones = ((r128 >> 6) == (l128 >> 6)).astype(BF16)
    npair = RWKV_HEADS // 2

    def half_sums_mxu(parts, n_split):
        res = jnp.dot(jnp.concatenate(parts, axis=0), half_ones, preferred_element_type=F32)
        out = []
        for i in range(len(parts) // n_split):
            acc = res[i * n_split * HEAD_DIM:(i * n_split + 1) * HEAD_DIM]
            for p in range(1, n_split):
                acc = acc + res[(i * n_split + p) * HEAD_DIM:(i * n_split + p + 1) * HEAD_DIM]
            out.append(acc)
        return out

    def group(gi, carry):
        t0 = pl.multiple_of(gi * SUBLANE, SUBLANE)
        rows = pl.ds(t0, SUBLANE)
        blk = lambda ref, b: [ref[b, rows, j * LANE:(j + 1) * LANE] for j in range(npair)]

        def v_pieces(b):
            out = []
            for v8 in blk(v_ref, b):
                vh = v8.astype(BF16).astype(F32)
                out.append((vh, v8 - vh))
            return out

        def v_columns(pieces, u):
            parts = []
            for j in range(npair):
                parts += [jnp.where(e01, pc[u:u + 1], 0.0).astype(BF16) for pc in pieces[j]]
            return half_sums_mxu(parts, 2)

        def out_rows(qparts):
            return [jnp.sum(jnp.where(e01, oc, 0.0), axis=0, keepdims=True) for oc in half_sums_mxu(qparts, 2)]

        pieces = v_pieces(0)
        vcols = [v_columns(pieces, u) for u in range(SUBLANE)]
        pending = None
        for b in range(bb + 1):
            if b < bb:
                kk8, w8, b8, kt8, r8 = blk(kk_ref, b), blk(w_ref, b), blk(b_ref, b), blk(kt_ref, b), blk(r_ref, b)
                st = [s_ref[b, j] for j in range(npair)]
                nxt_pieces = v_pieces(b + 1) if b + 1 < bb else None
            nxt_vcols, qsteps, orows = [], [], []
            for u in range(SUBLANE):
                if b < bb:
                    prods = [st[j] * kk8[j][u:u + 1] for j in range(npair)]
                    sums = [(jnp.sum(jnp.where(lo, p, 0.0), axis=-1, keepdims=True),
                             jnp.sum(jnp.where(lo, 0.0, p), axis=-1, keepdims=True)) for p in prods]
                if nxt_pieces is not None:
                    nxt_vcols.append(v_columns(nxt_pieces, u))
                if pending is not None:
                    orows.append(out_rows(pending[1][u]))
                if b < bb:
                    qs = []
                    for j in range(npair):
                        skk = jnp.where(lo, sums[j][0], sums[j][1])
                        s = st[j] * w8[j][u:u + 1] - skk * b8[j][u:u + 1] + vcols[u][j] * kt8[j][u:u + 1]
                        st[j] = s
                        q = s * r8[j][u:u + 1]
                        qh = q.astype(BF16)
                        qs += [qh, (q - qh.astype(F32)).astype(BF16)]
                    qsteps.append(qs)
            if pending is not None:
                pb = pending[0]
                for j in range(npair):
                    o_ref[pb, rows, j * LANE:(j + 1) * LANE] = jnp.concatenate([orows[u][j] for u in range(SUBLANE)],
                                                                               axis=0)
            if b < bb:
                for j in range(npair):
                    s_ref[b, j] = st[j]
                pending = (b, qsteps)
                vcols = nxt_vcols
                nxt_pieces = None
        return carry

    lax.fori_loop(0, tc // SUBLANE, group, 0)

    @pl.when(c == pl.num_programs(1) - 1)
    def _():
        sf_ref[...] = s_ref[...]


def _pair_state(s):
    b = s.shape[0]
    return (s.reshape(b, RWKV_HEADS // 2, 2, HEAD_DIM, HEAD_DIM)
            .transpose(0, 1, 3, 2, 4).reshape(b, RWKV_HEADS // 2, HEAD_DIM, LANE))


def _unpair_state(s):
    b = s.shape[0]
    return (s.reshape(b, RWKV_HEADS // 2, HEAD_DIM, 2, HEAD_DIM)
            .transpose(0, 1, 3, 2, 4).reshape(b, RWKV_HEADS, HEAD_DIM, HEAD_DIM))


def _wkv(r, w, kt, kk, bv, v, s0, bb, tc):
    bsz, tp, _ = r.shape
    tc = min(tc, tp)
    tok = pl.BlockSpec((bb, tc, RWKV_WIDTH), lambda i, c: (i, c, 0))
    st = pl.BlockSpec((bb, RWKV_HEADS // 2, HEAD_DIM, LANE), lambda i, c: (i, 0, 0, 0))
    o, sf = pl.pallas_call(
        functools.partial(_wkv_kernel, bb, tc),
        grid=(bsz // bb, tp // tc),
        in_specs=[tok] * 6 + [st],
        out_specs=[tok, st],
        out_shape=[jax.ShapeDtypeStruct((bsz, tp, RWKV_WIDTH), F32),
                   jax.ShapeDtypeStruct((bsz, RWKV_HEADS // 2, HEAD_DIM, LANE), F32)],
        scratch_shapes=[pltpu.VMEM((bb, RWKV_HEADS // 2, HEAD_DIM, LANE), F32)],
        compiler_params=_cparams("parallel", "arbitrary"),
        name="wkv_scan",
    )(r, w, kt, kk, bv, v, _pair_state(s0))
    return o, _unpair_state(sf)


def _rope(x, c, sa, sb):
    w = x.shape[1]
    return x * c + pltpu.roll(x, w - ROT_HALF, 1) * sa + pltpu.roll(x, ROT_HALF, 1) * sb


def _attn_prep_kernel(p_ref, c_ref, sa_ref, sb_ref, g_ref, b_ref, q_o, k_o, qi_o, ki_o):
    c1, sa1, sb1 = c_ref[...], sa_ref[...], sb_ref[...]
    rep = lambda t, n: jnp.concatenate([t] * n, axis=1)
    nq = ATTN_WIDTH // LANE
    nk = KV_W // LANE
    q_o[...] = _rope(p_ref[:, 0:ATTN_WIDTH], rep(c1, nq), rep(sa1, nq), rep(sb1, nq))
    k_o[...] = _rope(p_ref[:, ATTN_WIDTH:ATTN_WIDTH + KV_W], rep(c1, nk), rep(sa1, nk), rep(sb1, nk))
    qi0 = ATTN_WIDTH + 2 * KV_W
    qi_o[...] = _rope(p_ref[:, qi0:qi0 + IDX_HEADS * IDX_DIM], rep(c1, nq), rep(sa1, nq), rep(sb1, nq))
    x = p_ref[:, KI_OFF:KI_OFF + LANE]
    lane = lax.broadcasted_iota(I32, x.shape, 1)
    isk = lane < IDX_DIM
    mu = jnp.sum(jnp.where(isk, x, 0.0), axis=-1, keepdims=True) * (1.0 / IDX_DIM)
    xc = jnp.where(isk, x - mu, 0.0)
    var = jnp.sum(xc * xc, axis=-1, keepdims=True) * (1.0 / IDX_DIM)
    y = xc * lax.rsqrt(var + LN_EPS) * g_ref[...] + b_ref[...]
    y = _rope(y, jnp.where(isk, c1, 1.0), jnp.where(isk, sa1, 0.0), jnp.where(isk, sb1, 0.0))
    ki_o[...] = jnp.where(isk, y, x)


def _attn_prep(p_at, tabs, idx_g, idx_b, tm):
    n = p_at.shape[0]
    tm = min(tm, n)
    tpb = tabs[0].shape[0] // tm
    tok = lambda w: pl.BlockSpec((tm, w), lambda i: (i, 0))
    tab = pl.BlockSpec((tm, LANE), lambda i: (i % tpb, 0))
    row = pl.BlockSpec((1, LANE), lambda i: (0, 0))
    return pl.pallas_call(
        _attn_prep_kernel,
        grid=(n // tm,),
        in_specs=[tok(AT_PAD_W), tab, tab, tab, row, row],
        out_specs=[tok(ATTN_WIDTH), tok(KV_W), tok(IDX_HEADS * IDX_DIM), tok(LANE)],
        out_shape=[jax.ShapeDtypeStruct((n, ATTN_WIDTH), F32), jax.ShapeDtypeStruct((n, KV_W), F32),
                   jax.ShapeDtypeStruct((n, IDX_HEADS * IDX_DIM), F32), jax.ShapeDtypeStruct((n, LANE), F32)],
        compiler_params=_cparams("parallel"),
        name="attn_prep",
    )(p_at, tabs[0], tabs[1], tabs[2], idx_g, idx_b)


def _rope_tables(pos):
    inv = ROPE_THETA ** (-jnp.arange(ROT_HALF, dtype=F32) * 2.0 / ROT)
    ang = pos.astype(F32)[:, None] * inv[None, :]
    cos, sin = jnp.cos(ang), jnp.sin(ang)
    n = pos.shape[0]
    rest = HEAD_DIM - ROT
    c = jnp.concatenate([cos, cos, jnp.ones((n, rest), F32)], axis=1)
    sa = jnp.concatenate([-sin, jnp.zeros((n, rest + ROT_HALF), F32)], axis=1)
    sb = jnp.concatenate([jnp.zeros((n, ROT_HALF), F32), sin, jnp.zeros((n, rest), F32)], axis=1)
    two = lambda t: jnp.concatenate([t, t], axis=1)
    return two(c), two(sa), two(sb)


def _select_bounds(key_ref, n_tiles, rows, kt, k_sel, idx_bits):
    def count(pred):
        def body(i, acc):
            off = pl.multiple_of(i * kt, kt)
            idx = off + lax.broadcasted_iota(I32, (rows, kt), 1)
            hit = jnp.where(pred(key_ref[:, pl.ds(off, kt)], idx), 1.0, 0.0)
            for c in range(kt // LANE):
                acc = acc + hit[:, c * LANE:(c + 1) * LANE]
            return acc
        acc = lax.fori_loop(0, n_tiles, body, jnp.zeros((rows, LANE), F32))
        return jnp.sum(acc, axis=-1, keepdims=True)

    def thr_more(c):
        i, _, n_ge = c
        return (i < 32) & (jnp.max(jnp.abs(n_ge - k_sel)) > 0.0)

    def thr_bit(c):
        i, res, n_ge = c
        cand = res + jnp.left_shift(jnp.int32(1), 31 - i)
        cnt = count(lambda key, idx: key >= cand)
        take = cnt >= k_sel
        return i + 1, jnp.where(take, cand, res), jnp.where(take, cnt, n_ge)

    n_all = jnp.full((rows, 1), 1.0, F32) * (n_tiles * kt)
    _, thr, n_ge = lax.while_loop(thr_more, thr_bit, (jnp.int32(0), jnp.full((rows, 1), INT_MIN, I32), n_all))

    def tie_search():
        need = k_sel - count(lambda key, idx: key > thr)

        def idx_bit(i, res):
            cand = res | jnp.left_shift(jnp.int32(1), idx_bits - 1 - i)
            c = count(lambda key, idx: (key == thr) & (idx < cand))
            return jnp.where(c < need, cand, res)

        return lax.fori_loop(0, idx_bits, idx_bit, jnp.zeros((rows, 1), I32))

    jmax = lax.cond(jnp.max(n_ge) > k_sel, tie_search,
                    lambda: jnp.full((rows, 1), 2 ** idx_bits - 1, I32))
    return thr, jmax


def _dsa_prompt_kernel(kt, n_sel, idx_bits, q_ref, qi_ref, wi_ref, k_ref, v_ref, ki_ref, o_ref,
                       key_ref, m_ref, l_ref, acc_ref):
    i = pl.program_id(1)
    n_kt = (i * Q_BLOCK + Q_BLOCK + kt - 1) // kt
    qi = qi_ref[0]
    qis = jnp.concatenate([qi[:, h * IDX_DIM:(h + 1) * IDX_DIM] for h in range(IDX_HEADS)], axis=0)
    wi = wi_ref[0] * IDX_HEADS ** -0.5
    qpos = i * Q_BLOCK + lax.broadcasted_iota(I32, (Q_BLOCK, kt), 0)
    lane = lax.broadcasted_iota(I32, (Q_BLOCK, kt), 1)

    def scores(t, carry):
        off = pl.multiple_of(t * kt, kt)
        s = lax.dot_general(qis, ki_ref[0, pl.ds(off, kt), :], _NT, preferred_element_type=F32)
        acc = jnp.zeros((Q_BLOCK, kt), F32)
        for h in range(IDX_HEADS):
            acc = acc + wi[:, h:h + 1] * jnp.maximum(s[h * Q_BLOCK:(h + 1) * Q_BLOCK] * IDX_DIM ** -0.5, 0.0)
        key_ref[:, pl.ds(off, kt)] = jnp.where(off + lane <= qpos, _f2key(acc), INT_MIN)
        return carry

    lax.fori_loop(0, n_kt, scores, 0)
    thr, jmax = _select_bounds(key_ref, n_kt, Q_BLOCK, kt, n_sel, idx_bits)
    thr = jnp.maximum(thr, INT_MIN + 1)

    q = q_ref[0].astype(F32) * HEAD_DIM ** -0.5
    grp = lax.broadcasted_iota(I32, (Q_BLOCK, KV_W), 1) >> 6
    rep = ATTN_HEADS // KV_HEADS

    def expand(h):
        qh = q[:, h * HEAD_DIM:(h + 1) * HEAD_DIM]
        return jnp.where(grp == h // rep, jnp.concatenate([qh] * KV_HEADS, axis=1), 0.0).astype(BF16)

    qe = [expand(h) for h in range(ATTN_HEADS)]
    m_ref[...] = jnp.full(m_ref.shape, NEG_BIG, F32)
    l_ref[...] = jnp.zeros(l_ref.shape, F32)
    acc_ref[...] = jnp.zeros(acc_ref.shape, F32)

    def attend(t, carry):
        off = pl.multiple_of(t * kt, kt)
        key = key_ref[:, pl.ds(off, kt)]
        kidx = off + lane
        sel = (key > thr) | ((key == thr) & (kidx <= jmax))
        kt_tile = k_ref[0, pl.ds(off, kt), :]
        vt_tile = v_ref[0, pl.ds(off, kt), :]
        qk = lambda h: lax.dot_general(qe[h], kt_tile, _NT, preferred_element_type=F32)

        def finish(h, p, alpha):
            acc_ref[h] = alpha * acc_ref[h] + jnp.dot(p, vt_tile, preferred_element_type=F32)

        nxt = qk(0)
        pending = None
        for h in range(ATTN_HEADS):
            lg = nxt
            if h + 1 < ATTN_HEADS:
                nxt = qk(h + 1)
            lg = jnp.where(sel, lg, -jnp.inf)
            m = m_ref[h]
            mn = jnp.maximum(m, jnp.max(lg, axis=-1, keepdims=True))
            p = jnp.exp(lg - mn)
            alpha = jnp.exp(m - mn)
            l_ref[h] = alpha * l_ref[h] + jnp.sum(p, axis=-1, keepdims=True)
            m_ref[h] = mn
            if pending is not None:
                finish(*pending)
            pending = (h, p.astype(BF16), alpha)
        finish(*pending)
        return carry

    lax.fori_loop(0, n_kt, attend, 0)
    pieces = []
    for h in range(ATTN_HEADS):
        g = h // rep
        pieces.append(acc_ref[h][:, g * HEAD_DIM:(g + 1) * HEAD_DIM] / l_ref[h])
    o_ref[0] = jnp.concatenate(pieces, axis=1)


def _dsa_prompt(q, qi, wi, k, v, ki, t_real):
    bsz, tp, _ = q.shape
    kt = 640 if tp % 640 == 0 else LANE
    n_sel = min(TOPK_MAX, t_real // 4)
    idx_bits = max(1, (tp - 1).bit_length())
    blk = lambda w: pl.BlockSpec((1, Q_BLOCK, w), lambda b, i: (b, i, 0))
    seq = lambda w: pl.BlockSpec((1, tp, w), lambda b, i: (b, 0, 0))
    return pl.pallas_call(
        functools.partial(_dsa_prompt_kernel, kt, n_sel, idx_bits),
        grid=(bsz, tp // Q_BLOCK),
        in_specs=[blk(ATTN_WIDTH), blk(IDX_HEADS * IDX_DIM), blk(IDX_HEADS),
                  seq(KV_W), seq(KV_W), seq(IDX_DIM)],
        out_specs=blk(ATTN_WIDTH),
        out_shape=jax.ShapeDtypeStruct((bsz, tp, ATTN_WIDTH), F32),
        scratch_shapes=[pltpu.VMEM((Q_BLOCK, tp), I32),
                        pltpu.VMEM((ATTN_HEADS, Q_BLOCK, 1), F32),
                        pltpu.VMEM((ATTN_HEADS, Q_BLOCK, 1), F32),
                        pltpu.VMEM((ATTN_HEADS, Q_BLOCK, KV_W), F32)],
        compiler_params=_cparams("parallel", "arbitrary"),
        name="dsa_prompt",
    )(q, qi, wi, k, v, ki)


def _dsa_s_scores_kernel(pps, pt_ref, qi_ref, wi_ref, *refs):
    ci_refs, o_ref = refs[:pps], refs[pps]
    qi = qi_ref[0]
    w = wi_ref[0] * IDX_HEADS ** -0.5
    ki = jnp.concatenate([ci_refs[u][0, 0] for u in range(pps)], axis=0)
    s = _dot3(qi, ki, _NT)
    o_ref[0] = jnp.sum(w * jnp.maximum(s * IDX_DIM ** -0.5, 0.0), axis=0, keepdims=True)


def _dsa_s_scores(page_table, qi3, wi3, cache_idx, layer):
    bd, npages = page_table.shape
    pps = PAGES_PER_STEP
    page = lambda u: pl.BlockSpec((1, 1, PAGE_SIZE, IDX_DIM),
                                  lambda b, p, pt: (layer, pt[b * npages + p * pps + u], 0, 0))
    gs = pltpu.PrefetchScalarGridSpec(
        num_scalar_prefetch=1,
        grid=(bd, npages // pps),
        in_specs=[pl.BlockSpec((1, IDX_HEADS, IDX_DIM), lambda b, p, pt: (b, 0, 0)),
                  pl.BlockSpec((1, IDX_HEADS, 1), lambda b, p, pt: (b, 0, 0))] + [page(u) for u in range(pps)],
        out_specs=pl.BlockSpec((1, 1, pps * PAGE_SIZE), lambda b, p, pt: (b, 0, p)),
    )
    return pl.pallas_call(
        functools.partial(_dsa_s_scores_kernel, pps),
        grid_spec=gs,
        out_shape=jax.ShapeDtypeStruct((bd, 1, npages * PAGE_SIZE), F32),
        compiler_params=_cparams("parallel", "arbitrary"),
        name="dsa_decode_scores",
    )(page_table.reshape(-1), qi3, wi3, *([cache_idx] * pps))


def _dsa_s_bounds_kernel(past, n_sel, idx_bits, sc_ref, qi_ref, kiw_ref, hm_ref, thr_o, j_o, kn_o, key_ref):
    rows = sc_ref.shape[0]
    kiw = kiw_ref[...]
    lane = lax.broadcasted_iota(I32, kiw.shape, 1)
    rolled = pltpu.roll(kiw, IDX_DIM, 1)
    ki2 = jnp.where(lane < IDX_DIM, kiw, rolled)
    w8 = jnp.where(lane < IDX_HEADS, rolled, 0.0) * IDX_HEADS ** -0.5
    prod = qi_ref[...] * jnp.concatenate([ki2] * (IDX_HEADS // 2), axis=1)
    s = _dot3(prod, hm_ref[...])
    new = jnp.sum(w8 * jnp.maximum(s * IDX_DIM ** -0.5, 0.0), axis=-1, keepdims=True)
    knew = _f2key(new)
    key_ref[:, 0:past] = _f2key(sc_ref[...])
    key_ref[:, past:past + LANE] = jnp.where(lane == 0, knew, INT_MIN)
    thr, jmax = _select_bounds(key_ref, (past + LANE) // LANE, rows, LANE, n_sel, idx_bits)
    thr_o[...] = thr
    j_o[...] = jmax
    kn_o[...] = knew


def _dsa_s_bounds(scores, qi, kiw, hm, past):
    rows = scores.shape[0]
    n_sel = min(TOPK_MAX, (past + 1) // 4)
    idx_bits = (past + LANE - 1).bit_length()
    out = jax.ShapeDtypeStruct((rows, 1), I32)
    return pl.pallas_call(
        functools.partial(_dsa_s_bounds_kernel, past, n_sel, idx_bits),
        out_shape=[out, out, out],
        scratch_shapes=[pltpu.VMEM((rows, past + LANE), I32)],
        compiler_params=pltpu.CompilerParams(vmem_limit_bytes=VMEM_LIMIT),
        name="dsa_decode_bounds",
    )(scores, qi, kiw, hm)


def _dsa_s_attend_kernel(pps, past, pt_ref, thr_ref, j_ref, kn_ref, qe_ref, sc_ref, kn_row_ref, vn_row_ref,
                         fold_ref, *refs):
    ck, cv, o_ref = refs[:pps], refs[pps:2 * pps], refs[2 * pps]
    m_ref, l_ref, acc_ref = refs[2 * pps + 1:]
    b = pl.program_id(0)
    p = pl.program_id(1)

    @pl.when(p == 0)
    def _():
        m_ref[...] = jnp.full(m_ref.shape, NEG_BIG, F32)
        l_ref[...] = jnp.zeros(l_ref.shape, F32)
        acc_ref[...] = jnp.zeros(acc_ref.shape, F32)

    thr, jmax = thr_ref[b], j_ref[b]
    qe = qe_ref[0]
    qeb = qe.astype(BF16)
    kidx = p * pps * PAGE_SIZE + lax.broadcasted_iota(I32, (1, pps * PAGE_SIZE), 1)
    kb = jnp.concatenate([ck[u][0, 0].astype(BF16) for u in range(pps)], axis=0)
    vb = jnp.concatenate([cv[u][0, 0].astype(BF16) for u in range(pps)], axis=0)
    lg = lax.dot_general(qeb, kb, _NT, preferred_element_type=F32) * HEAD_DIM ** -0.5
    key = _f2key(sc_ref[0])
    sel = (key > thr) | ((key == thr) & (kidx <= jmax))
    lg = jnp.where(sel, lg, NEG_BIG)
    m = m_ref[...]
    mn = jnp.maximum(m, jnp.max(lg, axis=-1, keepdims=True))
    pr = jnp.where(sel, jnp.exp(lg - mn), 0.0)
    alpha = jnp.exp(m - mn)
    l_ref[...] = alpha * l_ref[...] + jnp.sum(pr, axis=-1, keepdims=True)
    acc_ref[...] = alpha * acc_ref[...] + jnp.dot(pr.astype(BF16), vb, preferred_element_type=F32)
    m_ref[...] = mn

    @pl.when(p == pl.num_programs(1) - 1)
    def _():
        knew = kn_ref[b]
        sel_new = (knew > thr) | ((knew == thr) & (past <= jmax))
        lg = jnp.sum(qe * kn_row_ref[0], axis=-1, keepdims=True) * HEAD_DIM ** -0.5
        m = m_ref[...]
        mn = jnp.where(sel_new, jnp.maximum(m, lg), m)
        pr = jnp.where(sel_new, jnp.exp(lg - mn), 0.0)
        alpha = jnp.exp(m - mn)
        l = alpha * l_ref[...] + pr
        acc = alpha * acc_ref[...] + pr * vn_row_ref[0]
        rowi = lax.broadcasted_iota(I32, acc.shape, 0)
        lanei = lax.broadcasted_iota(I32, acc.shape, 1)
        own = jnp.where((lanei >> 6) == (rowi >> 1), acc / l, 0.0)
        o_ref[0] = _dot_sel(own, fold_ref[...])


def _dsa_s_attend(page_table, thr, jmax, knew, qe, scores, k_new, v_new, fold, cache_k, cache_v, layer):
    bd, npages = page_table.shape
    past = npages * PAGE_SIZE
    pps = PAGES_PER_STEP
    page = lambda u: pl.BlockSpec((1, 1, PAGE_SIZE, KV_W),
                                  lambda b, p, pt, t, j, kn: (layer, pt[b * npages + p * pps + u], 0, 0))
    per_b = lambda s: pl.BlockSpec((1,) + s, lambda b, p, pt, t, j, kn: (b, 0, 0))
    gs = pltpu.PrefetchScalarGridSpec(
        num_scalar_prefetch=4,
        grid=(bd, npages // pps),
        in_specs=[per_b((ATTN_HEADS, KV_W)),
                  pl.BlockSpec((1, 1, pps * PAGE_SIZE), lambda b, p, pt, t, j, kn: (b, 0, p)),
                  per_b((1, KV_W)), per_b((1, KV_W)),
                  pl.BlockSpec(fold.shape, lambda b, p, pt, t, j, kn: (0, 0))]
                 + [page(u) for u in range(pps)] * 2,
        out_specs=per_b((ATTN_HEADS, HEAD_DIM)),
        scratch_shapes=[pltpu.VMEM((ATTN_HEADS, 1), F32), pltpu.VMEM((ATTN_HEADS, 1), F32),
                        pltpu.VMEM((ATTN_HEADS, KV_W), F32)],
    )
    return pl.pallas_call(
        functools.partial(_dsa_s_attend_kernel, pps, past),
        grid_spec=gs,
        out_shape=jax.ShapeDtypeStruct((bd, ATTN_HEADS, HEAD_DIM), F32),
        compiler_params=_cparams("parallel", "arbitrary"),
        name="dsa_decode_attend",
    )(page_table.reshape(-1), thr.reshape(-1), jmax.reshape(-1), knew.reshape(-1),
      qe, scores, k_new, v_new, fold, *([cache_k] * pps), *([cache_v] * pps))


def _merge_ln_kernel(o_ref, bonus_ref, g_ref, at_ref, x_ref, wt_ref, wb_ref, xg_ref, xb_ref,
                     lg_ref, lb_ref, bd_ref, out_ref):
    bd = bd_ref[...]
    o = o_ref[...]
    mean = _dot_sel(o, bd) * (1.0 / HEAD_DIM)
    oc = o - mean
    var = _dot_sel(oc * oc, bd) * (1.0 / HEAD_DIM)
    rw = (oc * lax.rsqrt(var + GN_EPS) * xg_ref[...] + xb_ref[...] + bonus_ref[...]) * g_ref[...]
    f = (jnp.dot(rw.astype(BF16), wt_ref[...], preferred_element_type=F32)
         + jnp.dot(at_ref[...].astype(BF16), wb_ref[...], preferred_element_type=F32))
    out_ref[...] = _layer_norm(DEEPNORM_ALPHA * x_ref[...] + f, lg_ref[...], lb_ref[...])


def _merge_ln(o, bonus, g, at, x, wts, tm):
    n = x.shape[0]
    tm = min(tm, n)
    tok = lambda w: pl.BlockSpec((tm, w), lambda i: (i, 0))
    full = lambda a: pl.BlockSpec(a.shape, lambda i: (0, 0))
    ws = [wts["wo_top"], wts["wo_bot"], wts["lnx_g"], wts["lnx_b"], wts["ln1_g"], wts["ln1_b"], wts["bd"]]
    return pl.pallas_call(
        _merge_ln_kernel,
        grid=(n // tm,),
        in_specs=[tok(RWKV_WIDTH)] * 3 + [tok(ATTN_WIDTH), tok(D_MODEL)] + [full(a) for a in ws],
        out_specs=tok(D_MODEL),
        out_shape=jax.ShapeDtypeStruct((n, D_MODEL), F32),
        compiler_params=_cparams("parallel"),
        name="merge_ln",
    )(o, bonus, g, at, x, *ws)


def _take_top(src_ref, n_rows, val_ref, idx_ref):
    shape = src_ref.shape
    row = lax.broadcasted_iota(I32, shape, 1)

    def body(a, carry):
        sv = src_ref[...]
        m = jnp.max(sv, axis=1, keepdims=True)
        idx = jnp.min(jnp.where(sv == m, row, n_rows), axis=1, keepdims=True)
        val_ref[:, pl.ds(a, 1), :] = m
        idx_ref[:, pl.ds(a, 1), :] = idx
        src_ref[...] = jnp.where(row == idx, -jnp.inf, sv)
        return carry

    lax.fori_loop(0, PEER_TOPK, body, 0)


def _peer_route_kernel(x_ref, wq_ref, sk0_ref, sk1_ref, e_o, g_o, s_ref, t_ref, i_ref, c_ref, ts_ref, ic_ref):
    q = jnp.dot(x_ref[...].astype(BF16), wq_ref[...], preferred_element_type=F32)
    for h in range(PEER_HEADS):
        qh = q[:, h * PEER_DKEY:(h + 1) * PEER_DKEY]
        s_ref[h] = _dot3(sk0_ref[...], qh, _NT)
        s_ref[PEER_HEADS + h] = _dot3(sk1_ref[...], qh, _NT)
    _take_top(s_ref, N_KEYS, t_ref, i_ref)
    t1, t2 = t_ref[0:PEER_HEADS], t_ref[PEER_HEADS:2 * PEER_HEADS]
    c_ref[...] = jnp.concatenate([t1[:, a:a + 1, :] + t2 for a in range(PEER_TOPK)], axis=1)
    _take_top(c_ref, PEER_TOPK * PEER_TOPK, ts_ref, ic_ref)
    ic = ic_ref[...]
    i1, i2 = i_ref[0:PEER_HEADS], i_ref[PEER_HEADS:2 * PEER_HEADS]
    ia, ib = ic >> 4, ic & (PEER_TOPK - 1)
    e = jnp.zeros(ic.shape, I32)
    for a in range(PEER_TOPK):
        e = e + jnp.where(ia == a, i1[:, a:a + 1, :] * N_KEYS, 0) + jnp.where(ib == a, i2[:, a:a + 1, :], 0)
    ts = ts_ref[...]
    ex = jnp.exp(ts - jnp.max(ts, axis=1, keepdims=True))
    e_o[0] = e
    g_o[0] = ex / jnp.sum(ex, axis=1, keepdims=True)


def _peer_route(x, wq, sk0, sk1, tt):
    n = x.shape[0]
    nt = n // tt
    full = lambda a: pl.BlockSpec(a.shape, lambda i: (0, 0))
    out = pl.BlockSpec((1, PEER_HEADS, PEER_TOPK, tt), lambda i: (i, 0, 0, 0))
    return pl.pallas_call(
        _peer_route_kernel,
        grid=(nt,),
        in_specs=[pl.BlockSpec((tt, D_MODEL), lambda i: (i, 0)), full(wq), full(sk0), full(sk1)],
        out_specs=[out, out],
        out_shape=[jax.ShapeDtypeStruct((nt, PEER_HEADS, PEER_TOPK, tt), I32),
                   jax.ShapeDtypeStruct((nt, PEER_HEADS, PEER_TOPK, tt), F32)],
        scratch_shapes=[pltpu.VMEM((2 * PEER_HEADS, N_KEYS, tt), F32),
                        pltpu.VMEM((2 * PEER_HEADS, PEER_TOPK, tt), F32),
                        pltpu.VMEM((2 * PEER_HEADS, PEER_TOPK, tt), I32),
                        pltpu.VMEM((PEER_HEADS, PEER_TOPK * PEER_TOPK, tt), F32),
                        pltpu.VMEM((PEER_HEADS, PEER_TOPK, tt), F32),
                        pltpu.VMEM((PEER_HEADS, PEER_TOPK, tt), I32)],
        compiler_params=_cparams("parallel"),
        name="peer_route",
    )(x, wq, sk0, sk1)


def _peer_gather_kernel(tt, n_valid, idx_hbm, x_ref, g_ref, uv_hbm, lg_ref, lb_ref, o_ref, idx_smem, *rest):
    bufs, (sem, isem, y_ref) = rest[:GATHER_BUFFERS], rest[GATHER_BUFFERS:]
    i = pl.program_id(0)
    n_idx = PEER_SLOTS * tt
    islot = i % 2

    def idx_copy(tile, slot):
        return pltpu.make_async_copy(idx_hbm.at[tile], idx_smem.at[pl.ds(slot * n_idx, n_idx)], isem.at[slot])

    @pl.when(i == 0)
    def _():
        idx_copy(0, 0).start()

    if n_valid < tt:
        y_ref[...] = jnp.zeros(y_ref.shape, F32)
    idx_copy(i, islot).wait()

    @pl.when(i + 1 < pl.num_programs(0))
    def _():
        idx_copy(i + 1, 1 - islot).start()

    base = islot * n_idx

    def issue(t, k):
        tok = base + t * PEER_SLOTS
        for s in range(PEER_SLOTS):
            e = idx_smem[tok + s]
            pltpu.async_copy(uv_hbm.at[pl.ds(e, 1)], bufs[k].at[pl.ds(s, 1)], sem.at[k], priority=s % 2)

    def wait(k):
        pltpu.make_async_copy(uv_hbm.at[pl.ds(0, PEER_SLOTS)], bufs[k], sem.at[k]).wait()

    lane = lax.broadcasted_iota(I32, (PEER_SLOTS, tt), 1)

    def compute(t, k):
        xrow = x_ref[pl.ds(t, 1), :]
        word = bufs[k][...]
        u = lax.bitcast_convert_type(word << 16, F32)
        v = lax.bitcast_convert_type(word & jnp.int32(-65536), F32)
        h = jnp.sum(u * xrow, axis=-1, keepdims=True)
        gate = jnp.sum(jnp.where(lane == t, g_ref[0], 0.0), axis=-1, keepdims=True)
        coef = gate * jax.nn.gelu(h)
        y_ref[pl.ds(t, 1), :] = jnp.sum(coef * v, axis=0, keepdims=True)

    depth = len(bufs)
    ahead = depth - 1
    rounds = max(n_valid - ahead, 0) // depth
    for t in range(min(ahead, n_valid)):
        issue(t, t % depth)

    def body(r, carry):
        for k in range(depth):
            t = r * depth + k
            wait(k)
            issue(t + ahead, (k + ahead) % depth)
            compute(t, k)
        return carry

    lax.fori_loop(0, rounds, body, 0)
    for t in range(rounds * depth, n_valid):
        wait(t % depth)
        if t + ahead < n_valid:
            issue(t + ahead, (t + ahead) % depth)
        compute(t, t % depth)
    o_ref[...] = _layer_norm(DEEPNORM_ALPHA * x_ref[...] + y_ref[...], lg_ref[...], lb_ref[...])


def _peer_gather(idx, gates, x, uv, ln_g, ln_b, tt, n_valid):
    n = x.shape[0]
    nt = n // tt
    row = pl.BlockSpec((1, D_MODEL), lambda i: (0, 0))
    return pl.pallas_call(
        functools.partial(_peer_gather_kernel, tt, n_valid),
        grid=(nt,),
        in_specs=[pl.BlockSpec(memory_space=pl.ANY),
                  pl.BlockSpec((tt, D_MODEL), lambda i: (i, 0)),
                  pl.BlockSpec((1, PEER_SLOTS, tt), lambda i: (i, 0, 0)),
                  pl.BlockSpec(memory_space=pl.ANY), row, row],
        out_specs=pl.BlockSpec((tt, D_MODEL), lambda i: (i, 0)),
        out_shape=jax.ShapeDtypeStruct((n, D_MODEL), F32),
        scratch_shapes=[pltpu.SMEM((2 * PEER_SLOTS * tt,), I32)]
                       + [pltpu.VMEM((PEER_SLOTS, D_MODEL), I32)] * GATHER_BUFFERS
                       + [pltpu.SemaphoreType.DMA((GATHER_BUFFERS,)),
                          pltpu.SemaphoreType.DMA((2,)),
                          pltpu.VMEM((tt, D_MODEL), F32)],
        compiler_params=_cparams("arbitrary"),
        name="peer_gather",
    )(idx, x, gates, uv, ln_g, ln_b)


def _peer(x, wts, tt, n_valid):
    e, gates = _peer_route(x, wts["wq"], wts["sk0"], wts["sk1"], tt)
    nt = x.shape[0] // tt
    e_tok = e.reshape(nt, PEER_SLOTS, tt).transpose(0, 2, 1)
    return _peer_gather(e_tok.reshape(nt, PEER_SLOTS * tt), gates.reshape(nt, PEER_SLOTS, tt), x,
                        wts["uv"], wts["ln2_g"], wts["ln2_b"], tt, n_valid)


def _pack_bf16_pair(lo, hi):
    bits = lambda a: lax.bitcast_convert_type(a.astype(BF16), jnp.uint16).astype(jnp.uint32)
    return lax.bitcast_convert_type(bits(lo) | (bits(hi) << 16), I32)


def _layer_weights(l, w_in, shift_mu, decay_w0, decay_up, iclr_a0, iclr_up, gate_up, k_k, k_a, r_k,
                   lnx_g, lnx_b, idx_ln_g, idx_ln_b, w_out, ln1_g, ln1_b, ln2_g, ln2_b,
                   peer_wq, peer_subkeys, peer_u, peer_v):
    row = lambda a: a.reshape(1, -1).astype(F32)
    lora = jnp.zeros((LORA_W, 3 * RWKV_WIDTH), F32)
    lora = lora.at[0:W_LORA, 0:RWKV_WIDTH].set(decay_up[l])
    lora = lora.at[W_LORA:W_LORA + A_LORA, RWKV_WIDTH:2 * RWKV_WIDTH].set(iclr_up[l])
    lora = lora.at[W_LORA + A_LORA:, 2 * RWKV_WIDTH:].set(gate_up[l])
    lora_hi = lora.astype(BF16)
    pad_lane = lambda a: jnp.pad(a.reshape(1, -1), ((0, 0), (0, LANE - a.size)))
    half = PEER_DKEY // 2
    seg = jnp.arange(RWKV_WIDTH) // HEAD_DIM
    return dict(
        w_rw=w_in[l][:, :SHIFT_W].astype(BF16),
        w_at=jnp.pad(w_in[l][:, SHIFT_W:], ((0, 0), (0, AT_PAD_W - ATTN_PROJ_W))).astype(BF16),
        mu=row(shift_mu[l]), w0=row(decay_w0[l]), a0=row(iclr_a0[l]),
        lora_hi=lora_hi, lora_lo=(lora - lora_hi.astype(F32)).astype(BF16),
        k_k=row(k_k[l]), k_a=row(k_a[l]), r_k=row(r_k[l]),
        bd=(seg[:, None] == seg[None, :]).astype(BF16),
        lnx_g=row(lnx_g[l]), lnx_b=row(lnx_b[l]),
        idx_g=pad_lane(idx_ln_g[l]), idx_b=pad_lane(idx_ln_b[l]),
        wo_top=w_out[l][:RWKV_WIDTH].astype(BF16), wo_bot=w_out[l][RWKV_WIDTH:].astype(BF16),
        ln1_g=row(ln1_g[l]), ln1_b=row(ln1_b[l]), ln2_g=row(ln2_g[l]), ln2_b=row(ln2_b[l]),
        wq=peer_wq[l].astype(BF16),
        sk0=jnp.pad(peer_subkeys[l, 0], ((0, 0), (0, half))),
        sk1=jnp.pad(peer_subkeys[l, 1], ((0, 0), (half, 0))),
        uv=_pack_bf16_pair(peer_u[l], peer_v[l]),
    )


def _tile(n, cap):
    if n <= cap:
        return n
    return max(d for d in range(8, cap + 1, 8) if n % d == 0)


def _mixer_front(x, wts, tabs):
    tm = _tile(x.shape[0], 512)
    p_rw = _matmul(x, wts["w_rw"], tm)
    p_at = _matmul(x, wts["w_at"], tm)
    q, k, qi, kiw = _attn_prep(p_at, tabs, wts["idx_g"], wts["idx_b"], _tile(tabs[0].shape[0], 640))
    v = p_at[:, ATTN_WIDTH + KV_W:ATTN_WIDTH + 2 * KV_W]
    return p_rw, q, k, v, qi, kiw


def kernel(x_prompt, x_sample, cache_k, cache_v, cache_idx_k, state_wkv, state_shift, page_table, meta_tokens, w_in, shift_mu, decay_w0, decay_up, iclr_a0, iclr_up, gate_up, k_k, k_a, r_k, lnx_g, lnx_b, idx_ln_g, idx_ln_b, w_out, ln1_g, ln1_b, ln2_g, ln2_b, peer_wq, peer_subkeys, peer_u, peer_v):
    bsz, seq, _ = x_prompt.shape
    t = seq + N_META
    tp = -(-t // LANE) * LANE
    n_p = bsz * tp
    depth = w_in.shape[0]
    bd_, ts_, _ = x_sample.shape
    assert ts_ == 1
    npages = page_table.shape[1]
    past = npages * PAGE_SIZE
    n_pool = cache_k.shape[1]
    peer_tt = LANE
    ns_pad = -(-bd_ // peer_tt) * peer_tt

    xp = jnp.concatenate([jnp.broadcast_to(meta_tokens[None], (bsz, N_META, D_MODEL)), x_prompt], axis=1)
    xp = jnp.pad(xp, ((0, 0), (0, tp - t), (0, 0))).reshape(n_p, D_MODEL)
    xs = x_sample.reshape(bd_, D_MODEL)
    tabs_p = _rope_tables(jnp.arange(tp, dtype=I32))
    tabs_s = _rope_tables(jnp.full((bd_,), past, I32))
    ck = cache_k.reshape(depth, n_pool, PAGE_SIZE, KV_W)
    cv = cache_v.reshape(depth, n_pool, PAGE_SIZE, KV_W)
    hsel = jnp.arange(IDX_HEADS * IDX_DIM)[:, None] // IDX_DIM == jnp.arange(LANE)[None, :]
    hm = hsel.astype(F32)
    fold = (jnp.arange(KV_W)[:, None] % HEAD_DIM == jnp.arange(HEAD_DIM)[None, :]).astype(BF16)
    own = (jnp.arange(KV_W)[None, :] // HEAD_DIM == jnp.arange(ATTN_HEADS)[:, None] // (ATTN_HEADS // KV_HEADS))

    k_p, v_p, ki_p, wkv_p, sh_p = [], [], [], [], []
    k_s, v_s, ki_s, wkv_s, sh_s = [], [], [], [], []
    for l in range(depth):
        wts = _layer_weights(l, w_in, shift_mu, decay_w0, decay_up, iclr_a0, iclr_up, gate_up, k_k, k_a, r_k,
                             lnx_g, lnx_b, idx_ln_g, idx_ln_b, w_out, ln1_g, ln1_b, ln2_g, ln2_b,
                             peer_wq, peer_subkeys, peer_u, peer_v)

        p_rw, q, k, v, qi, kiw = _mixer_front(xp, wts, tabs_p)
        p_rw3 = p_rw.reshape(bsz, tp, SHIFT_W)
        r_, w_, kt_, kk_, b_, vv_, bonus, g = _rwkv_prep(
            p_rw3, jnp.zeros((bsz, 1, SHIFT_W), F32), t, True, wts, LANE)
        o, s_fin = _wkv(r_, w_, kt_, kk_, b_, vv_,
                        jnp.zeros((bsz, RWKV_HEADS, HEAD_DIM, HEAD_DIM), F32), bsz, LANE)
        three = lambda a: a.reshape(bsz, tp, -1)
        at = _dsa_prompt(three(q).astype(BF16), three(qi).astype(BF16),
                         three(kiw)[:, :, IDX_DIM:IDX_DIM + IDX_HEADS],
                         three(k).astype(BF16), three(v).astype(BF16),
                         three(kiw)[:, :, :IDX_DIM].astype(BF16), t)
        flat = lambda a: a.reshape(n_p, -1)
        x1 = _merge_ln(flat(o), flat(bonus), flat(g), flat(at), xp, wts, _tile(n_p, 256))
        xp = _peer(x1, wts, peer_tt, peer_tt)
        k_p.append(three(k)[:, :t].reshape(bsz, t, KV_HEADS, HEAD_DIM))
        v_p.append(three(v)[:, :t].reshape(bsz, t, KV_HEADS, HEAD_DIM))
        ki_p.append(three(kiw)[:, :t, :IDX_DIM])
        wkv_p.append(s_fin)
        sh_p.append(p_rw3[:, t - 1])

        p_rw, q, k, v, qi, kiw = _mixer_front(xs, wts, tabs_s)
        r_, w_, kt_, kk_, b_, vv_, bonus, g = _rwkv_prep(
            p_rw[None], state_shift[l][None], bd_, False, wts, bd_)
        tc_s = 8
        padt = lambda a, c: jnp.pad(a[0][:, None, :], ((0, 0), (0, tc_s - 1), (0, 0)), constant_values=c)
        o, s_fin = _wkv(padt(r_, 0.0), padt(w_, 1.0), padt(kt_, 0.0), padt(kk_, 0.0), padt(b_, 0.0),
                        padt(vv_, 0.0), state_wkv[l].astype(F32), 4, tc_s)
        o = o[:, 0]
        scores = _dsa_s_scores(page_table, qi.reshape(bd_, IDX_HEADS, IDX_DIM),
                               kiw[:, IDX_DIM:IDX_DIM + IDX_HEADS].reshape(bd_, IDX_HEADS, 1),
                               cache_idx_k, l)
        thr, jmax, knew = _dsa_s_bounds(scores.reshape(bd_, past), qi, kiw, hm, past)
        qe = jnp.where(own[None], jnp.tile(q.reshape(bd_, ATTN_HEADS, HEAD_DIM), (1, 1, KV_HEADS)), 0.0)
        at = _dsa_s_attend(page_table, thr, jmax, knew, qe, scores, k[:, None, :], v[:, None, :], fold,
                           ck, cv, l).reshape(bd_, ATTN_WIDTH)
        x1 = _merge_ln(o, bonus[0], g[0], at, xs, wts, bd_)
        x1p = jnp.pad(x1, ((0, ns_pad - bd_), (0, 0)))
        xs = _peer(x1p, wts, peer_tt, bd_)[:bd_]
        k_s.append(k.reshape(bd_, 1, KV_HEADS, HEAD_DIM))
        v_s.append(v.reshape(bd_, 1, KV_HEADS, HEAD_DIM))
        ki_s.append(kiw[:, None, :IDX_DIM])
        wkv_s.append(s_fin)
        sh_s.append(p_rw)

    y_prompt = xp.reshape(bsz, tp, D_MODEL)[:, N_META:t]
    y_sample = xs.reshape(bd_, 1, D_MODEL)
    return (y_prompt, y_sample, jnp.stack(k_p), jnp.stack(v_p), jnp.stack(ki_p),
            jnp.stack(wkv_p).astype(state_wkv.dtype), jnp.stack(sh_p).astype(state_shift.dtype),
            jnp.stack(k_s), jnp.stack(v_s), jnp.stack(ki_s),
            jnp.stack(wkv_s).astype(state_wkv.dtype), jnp.stack(sh_s).astype(state_shift.dtype))
```

```python
import functools
import math

import jax
import jax.numpy as jnp
from jax import lax
from jax.experimental import pallas as pl
from jax.experimental.pallas import tpu as pltpu

F32 = jnp.float32
BF16 = jnp.bfloat16
I32 = jnp.int32

D_MODEL = 1024
N_META = 16
HEAD_DIM = 64
RWKV_WIDTH = D_MODEL // 2
RWKV_HEADS = RWKV_WIDTH // HEAD_DIM
ATTN_WIDTH = D_MODEL - RWKV_WIDTH
ATTN_HEADS = ATTN_WIDTH // HEAD_DIM
KV_HEADS = ATTN_HEADS // 2
KV_W = KV_HEADS * HEAD_DIM
W_LORA = 64
A_LORA = 64
G_LORA = 128
LORA_W = W_LORA + A_LORA + G_LORA
SHIFT_W = 3 * RWKV_WIDTH + LORA_W
IDX_HEADS = 8
IDX_DIM = 64
ATTN_PROJ_W = ATTN_WIDTH + 2 * KV_W + IDX_HEADS * IDX_DIM + IDX_DIM + IDX_HEADS
TOPK_MAX = 256
ROPE_THETA = 500000.0
ROT = HEAD_DIM // 4
ROT_HALF = ROT // 2
PEER_HEADS = 8
PEER_DKEY = 128
N_KEYS = 128
PEER_TOPK = 16
PEER_SLOTS = PEER_HEADS * PEER_TOPK
DEPTH = 2
DEEPNORM_ALPHA = (2.0 * DEPTH) ** 0.25
PAGE_SIZE = 128
LN_EPS = 1e-5
GN_EPS = 64e-5

LANE = 128
SUBLANE = 8
Q_BLOCK = 128
INT_MIN = -(2 ** 31)
NEG_BIG = -1e30
VMEM_LIMIT = 56 * 1024 * 1024
AT_PAD_W = 1664
KI_OFF = ATTN_WIDTH + 2 * KV_W + IDX_HEADS * IDX_DIM
PAGES_PER_STEP = 8
GATHER_BUFFERS = 6

_NT = (((1,), (1,)), ((), ()))


def _cparams(*sem):
    return pltpu.CompilerParams(dimension_semantics=sem, vmem_limit_bytes=VMEM_LIMIT)


def _split2(x):
    hi = x.astype(BF16)
    lo = (x - hi.astype(F32)).astype(BF16)
    return hi, lo


def _split3(x):
    hi = x.astype(BF16)
    r1 = x - hi.astype(F32)
    mid = r1.astype(BF16)
    lo = (r1 - mid.astype(F32)).astype(BF16)
    return hi, mid, lo


def _dot3(a, b, dims=None):
    ah, al = _split2(a)
    bh, bl = _split2(b)
    if dims is None:
        d = lambda p, q: jnp.dot(p, q, preferred_element_type=F32)
    else:
        d = lambda p, q: lax.dot_general(p, q, dims, preferred_element_type=F32)
    return d(ah, bh) + d(al, bh) + d(ah, bl)


def _dot_sel(x, m):
    h, mid, lo = _split3(x)
    d = lambda p: jnp.dot(p, m, preferred_element_type=F32)
    return d(h) + d(mid) + d(lo)


def _f2key(x):
    x = jnp.where(x == 0.0, 0.0, x)
    b = lax.bitcast_convert_type(x, I32)
    return b ^ ((b >> 31) & 0x7FFFFFFF)


def _layer_norm(z, g, b):
    mu = jnp.mean(z, axis=-1, keepdims=True)
    zc = z - mu
    var = jnp.mean(zc * zc, axis=-1, keepdims=True)
    return zc * lax.rsqrt(var + LN_EPS) * g + b


def _mm_kernel(x_ref, w_ref, o_ref):
    o_ref[...] = jnp.dot(x_ref[...].astype(BF16), w_ref[...], preferred_element_type=F32)


def _matmul(x, w, tm):
    m, k = x.shape
    n = w.shape[1]
    tm = min(tm, m)
    return pl.pallas_call(
        _mm_kernel,
        grid=(m // tm,),
        in_specs=[pl.BlockSpec((tm, k), lambda i: (i, 0)),
                  pl.BlockSpec((k, n), lambda i: (0, 0))],
        out_specs=pl.BlockSpec((tm, n), lambda i: (i, 0)),
        out_shape=jax.ShapeDtypeStruct((m, n), F32),
        compiler_params=_cparams("parallel"),
        name="proj_matmul",
    )(x, w)


def _rwkv_prep_kernel(t_real, tt, shift, p_ref, prev_ref, mu_ref, w0_ref, a0_ref, lwh_ref, lwl_ref,
                      kk_ref, ka_ref, rk_ref, bd_ref,
                      r_o, w_o, kt_o, kko_o, b_o, v_o, bonus_o, g_o, carry_ref):
    j = pl.program_id(1)
    pf = p_ref[0]
    if shift:
        @pl.when(j == 0)
        def _():
            carry_ref[...] = prev_ref[0]
        row = lax.broadcasted_iota(I32, pf.shape, 0)
        prev = jnp.where(row == 0, carry_ref[...], pltpu.roll(pf, 1, 0))
        carry_ref[...] = pf[tt - 1:tt, :]
    else:
        prev = prev_ref[0]
    xs = pf + mu_ref[...] * (prev - pf)
    r = xs[:, 0:RWKV_WIDTH]
    k = xs[:, RWKV_WIDTH:2 * RWKV_WIDTH]
    v = xs[:, 2 * RWKV_WIDTH:3 * RWKV_WIDTH]
    z = xs[:, 3 * RWKV_WIDTH:SHIFT_W]
    lane = lax.broadcasted_iota(I32, z.shape, 1)
    zt = jnp.where(lane < W_LORA, jnp.tanh(z),
                   jnp.where(lane < W_LORA + A_LORA, z, jax.nn.sigmoid(z)))
    zh, zl = _split2(zt)
    d = lambda p, q: jnp.dot(p, q, preferred_element_type=F32)
    lo = d(zh, lwh_ref[...]) + d(zl, lwh_ref[...]) + d(zh, lwl_ref[...])
    w_raw = w0_ref[...] + lo[:, 0:RWKV_WIDTH]
    a = jax.nn.sigmoid(a0_ref[...] + lo[:, RWKV_WIDTH:2 * RWKV_WIDTH])
    g = lo[:, 2 * RWKV_WIDTH:3 * RWKV_WIDTH]
    decay = jnp.exp(-math.exp(-0.5) * jax.nn.sigmoid(w_raw))
    bd = bd_ref[...]
    kk = k * kk_ref[...]
    kk = kk / jnp.maximum(jnp.sqrt(_dot_sel(kk * kk, bd)), 1e-12)
    kt = k * (1.0 + (a - 1.0) * ka_ref[...])
    bonus = _dot_sel(r * kt * rk_ref[...], bd) * v
    pos = j * tt + lax.broadcasted_iota(I32, r.shape, 0)
    valid = pos < t_real
    r_o[0] = r
    w_o[0] = jnp.where(valid, decay, 1.0)
    kt_o[0] = jnp.where(valid, kt, 0.0)
    kko_o[0] = jnp.where(valid, kk, 0.0)
    b_o[0] = jnp.where(valid, kk * a, 0.0)
    v_o[0] = v
    bonus_o[0] = bonus
    g_o[0] = g


def _rwkv_prep(p_rw, prev, t_real, shift, wts, tt):
    bsz, tp, _ = p_rw.shape
    tt = min(tt, tp)
    row = lambda n: pl.BlockSpec((1, n), lambda b, j: (0, 0))
    full = lambda a: pl.BlockSpec(a.shape, lambda b, j: (0, 0))
    tok = lambda n: pl.BlockSpec((1, tt, n), lambda b, j: (b, j, 0))
    prev_spec = pl.BlockSpec((1, 1, SHIFT_W), lambda b, j: (b, 0, 0)) if shift else tok(SHIFT_W)
    outs = pl.pallas_call(
        functools.partial(_rwkv_prep_kernel, t_real, tt, shift),
        grid=(bsz, tp // tt),
        in_specs=[tok(SHIFT_W), prev_spec, row(SHIFT_W), row(RWKV_WIDTH), row(RWKV_WIDTH),
                  full(wts["lora_hi"]), full(wts["lora_lo"]),
                  row(RWKV_WIDTH), row(RWKV_WIDTH), row(RWKV_WIDTH), full(wts["bd"])],
        out_specs=[tok(RWKV_WIDTH)] * 8,
        out_shape=[jax.ShapeDtypeStruct((bsz, tp, RWKV_WIDTH), F32)] * 8,
        scratch_shapes=[pltpu.VMEM((1, SHIFT_W), F32)],
        compiler_params=_cparams("parallel", "arbitrary"),
        name="rwkv_prep",
    )(p_rw, prev, wts["mu"], wts["w0"], wts["a0"], wts["lora_hi"], wts["lora_lo"],
      wts["k_k"], wts["k_a"], wts["r_k"], wts["bd"])
    return outs


def _wkv_kernel(bb, tc, r_ref, w_ref, kt_ref, kk_ref, b_ref, v_ref, s0_ref, o_ref, sf_ref, s_ref):
    c = pl.program_id(1)

    @pl.when(c == 0)
    def _():
        s_ref[...] = s0_ref[...]

    lane = lax.broadcasted_iota(I32, (HEAD_DIM, LANE), 1)
    row = lax.broadcasted_iota(I32, (HEAD_DIM, LANE), 0)
    lo = lane < HEAD_DIM
    e0 = lane == row
    e1 = lane == row + HEAD_DIM
    e01 = e0 | e1
    r128 = lax.broadcasted_iota(I32, (LANE, LANE), 0)
    l128 = lax.broadcasted_iota(I32, (LANE, LANE), 1)
    half_ones = ((r128 >> 6) == (l128 >> 6)).astype(BF16)
    npair = RWKV_HEADS // 2

    def half_sums_mxu(parts, n_split):
        res = jnp.dot(jnp.concatenate(parts, axis=0), half_ones, preferred_element_type=F32)
        out = []
        for i in range(len(parts) // n_split):
            acc = res[i * n_split * HEAD_DIM:(i * n_split + 1) * HEAD_DIM]
            for p in range(1, n_split):
                acc = acc + res[(i * n_split + p) * HEAD_DIM:(i * n_split + p + 1) * HEAD_DIM]
            out.append(acc)
        return out

    def group(gi, carry):
        t0 = pl.multiple_of(gi * SUBLANE, SUBLANE)
        rows = pl.ds(t0, SUBLANE)
        blk = lambda ref, b: [ref[b, rows, j * LANE:(j + 1) * LANE] for j in range(npair)]

        def v_pieces(b):
            out = []
            for v8 in blk(v_ref, b):
                vh = v8.astype(BF16).astype(F32)
                out.append((vh, v8 - vh))
            return out

        def v_columns(pieces, u):
            parts = []
            for j in range(npair):
                parts += [jnp.where(e01, pc[u:u + 1], 0.0).astype(BF16) for pc in pieces[j]]
            return half_sums_mxu(parts, 2)

        def out_rows(qparts):
            return [jnp.sum(jnp.where(e01, oc, 0.0), axis=0, keepdims=True) for oc in half_sums_mxu(qparts, 2)]

        pieces = v_pieces(0)
        vcols = [v_columns(pieces, u) for u in range(SUBLANE)]
        pending = None
        for b in range(bb + 1):
            if b < bb:
                kk8, w8, b8, kt8, r8 = blk(kk_ref, b), blk(w_ref, b), blk(b_ref, b), blk(kt_ref, b), blk(r_ref, b)
                st = [s_ref[b, j] for j in range(npair)]
                nxt_pieces = v_pieces(b + 1) if b + 1 < bb else None
            nxt_vcols, qsteps, orows = [], [], []
            for u in range(SUBLANE):
                if b < bb:
                    prods = [st[j] * kk8[j][u:u + 1] for j in range(npair)]
                    sums = [(jnp.sum(jnp.where(lo, p, 0.0), axis=-1, keepdims=True),
                             jnp.sum(jnp.where(lo, 0.0, p), axis=-1, keepdims=True)) for p in prods]
                if nxt_pieces is not None:
                    nxt_vcols.append(v_columns(nxt_pieces, u))
                if pending is not None:
                    orows.append(out_rows(pending[1][u]))
                if b < bb:
                    qs = []
                    for j in range(npair):
                        skk = jnp.where(lo, sums[j][0], sums[j][1])
                        s = st[j] * w8[j][u:u + 1] - skk * b8[j][u:u + 1] + vcols[u][j] * kt8[j][u:u + 1]
                        st[j] = s
                        q = s * r8[j][u:u + 1]
                        qh = q.astype(BF16)
                        qs += [qh, (q - qh.astype(F32)).astype(BF16)]
                    qsteps.append(qs)
            if pending is not None:
                pb = pending[0]
                for j in range(npair):
                    o_ref[pb, rows, j * LANE:(j + 1) * LANE] = jnp.concatenate([orows[u][j] for u in range(SUBLANE)],
                                                                               axis=0)
            if b < bb:
                for j in range(npair):
                    s_ref[b, j] = st[j]
                pending = (b, qsteps)
                vcols = nxt_vcols
                nxt_pieces = None
        return carry

    lax.fori_loop(0, tc // SUBLANE, group, 0)

    @pl.when(c == pl.num_programs(1) - 1)
    def _():
        sf_ref[...] = s_ref[...]


def _pair_state(s):
    b = s.shape[0]
    return (s.reshape(b, RWKV_HEADS // 2, 2, HEAD_DIM, HEAD_DIM)
            .transpose(0, 1, 3, 2, 4).reshape(b, RWKV_HEADS // 2, HEAD_DIM, LANE))


def _unpair_state(s):
    b = s.shape[0]
    return (s.reshape(b, RWKV_HEADS // 2, HEAD_DIM, 2, HEAD_DIM)
            .transpose(0, 1, 3, 2, 4).reshape(b, RWKV_HEADS, HEAD_DIM, HEAD_DIM))


def _wkv(r, w, kt, kk, bv, v, s0, bb, tc):
    bsz, tp, _ = r.shape
    tc = min(tc, tp)
    tok = pl.BlockSpec((bb, tc, RWKV_WIDTH), lambda i, c: (i, c, 0))
    st = pl.BlockSpec((bb, RWKV_HEADS // 2, HEAD_DIM, LANE), lambda i, c: (i, 0, 0, 0))
    o, sf = pl.pallas_call(
        functools.partial(_wkv_kernel, bb, tc),
        grid=(bsz // bb, tp // tc),
        in_specs=[tok] * 6 + [st],
        out_specs=[tok, st],
        out_shape=[jax.ShapeDtypeStruct((bsz, tp, RWKV_WIDTH), F32),
                   jax.ShapeDtypeStruct((bsz, RWKV_HEADS // 2, HEAD_DIM, LANE), F32)],
        scratch_shapes=[pltpu.VMEM((bb, RWKV_HEADS // 2, HEAD_DIM, LANE), F32)],
        compiler_params=_cparams("parallel", "arbitrary"),
        name="wkv_scan",
    )(r, w, kt, kk, bv, v, _pair_state(s0))
    return o, _unpair_state(sf)


def _rope(x, c, sa, sb):
    w = x.shape[1]
    return x * c + pltpu.roll(x, w - ROT_HALF, 1) * sa + pltpu.roll(x, ROT_HALF, 1) * sb


def _attn_prep_kernel(p_ref, c_ref, sa_ref, sb_ref, g_ref, b_ref, q_o, k_o, qi_o, ki_o):
    c1, sa1, sb1 = c_ref[...], sa_ref[...], sb_ref[...]
    rep = lambda t, n: jnp.concatenate([t] * n, axis=1)
    nq = ATTN_WIDTH // LANE
    nk = KV_W // LANE
    q_o[...] = _rope(p_ref[:, 0:ATTN_WIDTH], rep(c1, nq), rep(sa1, nq), rep(sb1, nq))
    k_o[...] = _rope(p_ref[:, ATTN_WIDTH:ATTN_WIDTH + KV_W], rep(c1, nk), rep(sa1, nk), rep(sb1, nk))
    qi0 = ATTN_WIDTH + 2 * KV_W
    qi_o[...] = _rope(p_ref[:, qi0:qi0 + IDX_HEADS * IDX_DIM], rep(c1, nq), rep(sa1, nq), rep(sb1, nq))
    x = p_ref[:, KI_OFF:KI_OFF + LANE]
    lane = lax.broadcasted_iota(I32, x.shape, 1)
    isk = lane < IDX_DIM
    mu = jnp.sum(jnp.where(isk, x, 0.0), axis=-1, keepdims=True) * (1.0 / IDX_DIM)
    xc = jnp.where(isk, x - mu, 0.0)
    var = jnp.sum(xc * xc, axis=-1, keepdims=True) * (1.0 / IDX_DIM)
    y = xc * lax.rsqrt(var + LN_EPS) * g_ref[...] + b_ref[...]
    y = _rope(y, jnp.where(isk, c1, 1.0), jnp.where(isk, sa1, 0.0), jnp.where(isk, sb1, 0.0))
    ki_o[...] = jnp.where(isk, y, x)


def _attn_prep(p_at, tabs, idx_g, idx_b, tm):
    n = p_at.shape[0]
    tm = min(tm, n)
    tpb = tabs[0].shape[0] // tm
    tok = lambda w: pl.BlockSpec((tm, w), lambda i: (i, 0))
    tab = pl.BlockSpec((tm, LANE), lambda i: (i % tpb, 0))
    row = pl.BlockSpec((1, LANE), lambda i: (0, 0))
    return pl.pallas_call(
        _attn_prep_kernel,
        grid=(n // tm,),
        in_specs=[tok(AT_PAD_W), tab, tab, tab, row, row],
        out_specs=[tok(ATTN_WIDTH), tok(KV_W), tok(IDX_HEADS * IDX_DIM), tok(LANE)],
        out_shape=[jax.ShapeDtypeStruct((n, ATTN_WIDTH), F32), jax.ShapeDtypeStruct((n, KV_W), F32),
                   jax.ShapeDtypeStruct((n, IDX_HEADS * IDX_DIM), F32), jax.ShapeDtypeStruct((n, LANE), F32)],
        compiler_params=_cparams("parallel"),
        name="attn_prep",
    )(p_at, tabs[0], tabs[1], tabs[2], idx_g, idx_b)


def _rope_tables(pos):
    inv = ROPE_THETA ** (-jnp.arange(ROT_HALF, dtype=F32) * 2.0 / ROT)
    ang = pos.astype(F32)[:, None] * inv[None, :]
    cos, sin = jnp.cos(ang), jnp.sin(ang)
    n = pos.shape[0]
    rest = HEAD_DIM - ROT
    c = jnp.concatenate([cos, cos, jnp.ones((n, rest), F32)], axis=1)
    sa = jnp.concatenate([-sin, jnp.zeros((n, rest + ROT_HALF), F32)], axis=1)
    sb = jnp.concatenate([jnp.zeros((n, ROT_HALF), F32), sin, jnp.zeros((n, rest), F32)], axis=1)
    two = lambda t: jnp.concatenate([t, t], axis=1)
    return two(c), two(sa), two(sb)


def _select_bounds(key_ref, n_tiles, rows, kt, k_sel, idx_bits):
    def count(pred):
        def body(i, acc):
            off = pl.multiple_of(i * kt, kt)
            idx = off + lax.broadcasted_iota(I32, (rows, kt), 1)
            hit = jnp.where(pred(key_ref[:, pl.ds(off, kt)], idx), 1.0, 0.0)
            for c in range(kt // LANE):
                acc = acc + hit[:, c * LANE:(c + 1) * LANE]
            return acc
        acc = lax.fori_loop(0, n_tiles, body, jnp.zeros((rows, LANE), F32))
        return jnp.sum(acc, axis=-1, keepdims=True)

    def thr_more(c):
        i, _, n_ge = c
        return (i < 32) & (jnp.max(jnp.abs(n_ge - k_sel)) > 0.0)

    def thr_bit(c):
        i, res, n_ge = c
        cand = res + jnp.left_shift(jnp.int32(1), 31 - i)
        cnt = count(lambda key, idx: key >= cand)
        take = cnt >= k_sel
        return i + 1, jnp.where(take, cand, res), jnp.where(take, cnt, n_ge)

    n_all = jnp.full((rows, 1), 1.0, F32) * (n_tiles * kt)
    _, thr, n_ge = lax.while_loop(thr_more, thr_bit, (jnp.int32(0), jnp.full((rows, 1), INT_MIN, I32), n_all))

    def tie_search():
        need = k_sel - count(lambda key, idx: key > thr)

        def idx_bit(i, res):
            cand = res | jnp.left_shift(jnp.int32(1), idx_bits - 1 - i)
            c = count(lambda key, idx: (key == thr) & (idx < cand))
            return jnp.where(c < need, cand, res)

        return lax.fori_loop(0, idx_bits, idx_bit, jnp.zeros((rows, 1), I32))

    jmax = lax.cond(jnp.max(n_ge) > k_sel, tie_search,
                    lambda: jnp.full((rows, 1), 2 ** idx_bits - 1, I32))
    return thr, jmax


def _dsa_prompt_kernel(kt, n_sel, idx_bits, q_ref, qi_ref, wi_ref, k_ref, v_ref, ki_ref, o_ref,
                       key_ref, m_ref, l_ref, acc_ref):
    i = pl.program_id(1)
    n_kt = (i * Q_BLOCK + Q_BLOCK + kt - 1) // kt
    qi = qi_ref[0]
    qis = jnp.concatenate([qi[:, h * IDX_DIM:(h + 1) * IDX_DIM] for h in range(IDX_HEADS)], axis=0)
    wi = wi_ref[0] * IDX_HEADS ** -0.5
    qpos = i * Q_BLOCK + lax.broadcasted_iota(I32, (Q_BLOCK, kt), 0)
    lane = lax.broadcasted_iota(I32, (Q_BLOCK, kt), 1)

    def scores(t, carry):
        off = pl.multiple_of(t * kt, kt)
        s = lax.dot_general(qis, ki_ref[0, pl.ds(off, kt), :], _NT, preferred_element_type=F32)
        acc = jnp.zeros((Q_BLOCK, kt), F32)
        for h in range(IDX_HEADS):
            acc = acc + wi[:, h:h + 1] * jnp.maximum(s[h * Q_BLOCK:(h + 1) * Q_BLOCK] * IDX_DIM ** -0.5, 0.0)
        key_ref[:, pl.ds(off, kt)] = jnp.where(off + lane <= qpos, _f2key(acc), INT_MIN)
        return carry

    lax.fori_loop(0, n_kt, scores, 0)
    thr, jmax = _select_bounds(key_ref, n_kt, Q_BLOCK, kt, n_sel, idx_bits)
    thr = jnp.maximum(thr, INT_MIN + 1)

    q = q_ref[0].astype(F32) * HEAD_DIM ** -0.5
    grp = lax.broadcasted_iota(I32, (Q_BLOCK, KV_W), 1) >> 6
    rep = ATTN_HEADS // KV_HEADS

    def expand(h):
        qh = q[:, h * HEAD_DIM:(h + 1) * HEAD_DIM]
        return jnp.where(grp == h // rep, jnp.concatenate([qh] * KV_HEADS, axis=1), 0.0).astype(BF16)

    qe = [expand(h) for h in range(ATTN_HEADS)]
    m_ref[...] = jnp.full(m_ref.shape, NEG_BIG, F32)
    l_ref[...] = jnp.zeros(l_ref.shape, F32)
    acc_ref[...] = jnp.zeros(acc_ref.shape, F32)

    def attend(t, carry):
        off = pl.multiple_of(t * kt, kt)
        key = key_ref[:, pl.ds(off, kt)]
        kidx = off + lane
        sel = (key > thr) | ((key == thr) & (kidx <= jmax))
        kt_tile = k_ref[0, pl.ds(off, kt), :]
        vt_tile = v_ref[0, pl.ds(off, kt), :]
        qk = lambda h: lax.dot_general(qe[h], kt_tile, _NT, preferred_element_type=F32)

        def finish(h, p, alpha):
            acc_ref[h] = alpha * acc_ref[h] + jnp.dot(p, vt_tile, preferred_element_type=F32)

        nxt = qk(0)
        pending = None
        for h in range(ATTN_HEADS):
            lg = nxt
            if h + 1 < ATTN_HEADS:
                nxt = qk(h + 1)
            lg = jnp.where(sel, lg, -jnp.inf)
            m = m_ref[h]
            mn = jnp.maximum(m, jnp.max(lg, axis=-1, keepdims=True))
            p = jnp.exp(lg - mn)
            alpha = jnp.exp(m - mn)
            l_ref[h] = alpha * l_ref[h] + jnp.sum(p, axis=-1, keepdims=True)
            m_ref[h] = mn
            if pending is not None:
                finish(*pending)
            pending = (h, p.astype(BF16), alpha)
        finish(*pending)
        return carry

    lax.fori_loop(0, n_kt, attend, 0)
    pieces = []
    for h in range(ATTN_HEADS):
        g = h // rep
        pieces.append(acc_ref[h][:, g * HEAD_DIM:(g + 1) * HEAD_DIM] / l_ref[h])
    o_ref[0] = jnp.concatenate(pieces, axis=1)


def _dsa_prompt(q, qi, wi, k, v, ki, t_real):
    bsz, tp, _ = q.shape
    kt = 640 if tp % 640 == 0 else LANE
    n_sel = min(TOPK_MAX, t_real // 4)
    idx_bits = max(1, (tp - 1).bit_length())
    blk = lambda w: pl.BlockSpec((1, Q_BLOCK, w), lambda b, i: (b, i, 0))
    seq = lambda w: pl.BlockSpec((1, tp, w), lambda b, i: (b, 0, 0))
    return pl.pallas_call(
        functools.partial(_dsa_prompt_kernel, kt, n_sel, idx_bits),
        grid=(bsz, tp // Q_BLOCK),
        in_specs=[blk(ATTN_WIDTH), blk(IDX_HEADS * IDX_DIM), blk(IDX_HEADS),
                  seq(KV_W), seq(KV_W), seq(IDX_DIM)],
        out_specs=blk(ATTN_WIDTH),
        out_shape=jax.ShapeDtypeStruct((bsz, tp, ATTN_WIDTH), F32),
        scratch_shapes=[pltpu.VMEM((Q_BLOCK, tp), I32),
                        pltpu.VMEM((ATTN_HEADS, Q_BLOCK, 1), F32),
                        pltpu.VMEM((ATTN_HEADS, Q_BLOCK, 1), F32),
                        pltpu.VMEM((ATTN_HEADS, Q_BLOCK, KV_W), F32)],
        compiler_params=_cparams("parallel", "arbitrary"),
        name="dsa_prompt",
    )(q, qi, wi, k, v, ki)


def _dsa_s_scores_kernel(pps, pt_ref, qi_ref, wi_ref, *refs):
    ci_refs, o_ref = refs[:pps], refs[pps]
    qi = qi_ref[0]
    w = wi_ref[0] * IDX_HEADS ** -0.5
    ki = jnp.concatenate([ci_refs[u][0, 0] for u in range(pps)], axis=0)
    s = _dot3(qi, ki, _NT)
    o_ref[0] = jnp.sum(w * jnp.maximum(s * IDX_DIM ** -0.5, 0.0), axis=0, keepdims=True)


def _dsa_s_scores(page_table, qi3, wi3, cache_idx, layer):
    bd, npages = page_table.shape
    pps = PAGES_PER_STEP
    page = lambda u: pl.BlockSpec((1, 1, PAGE_SIZE, IDX_DIM),
                                  lambda b, p, pt: (layer, pt[b * npages + p * pps + u], 0, 0))
    gs = pltpu.PrefetchScalarGridSpec(
        num_scalar_prefetch=1,
        grid=(bd, npages // pps),
        in_specs=[pl.BlockSpec((1, IDX_HEADS, IDX_DIM), lambda b, p, pt: (b, 0, 0)),
                  pl.BlockSpec((1, IDX_HEADS, 1), lambda b, p, pt: (b, 0, 0))] + [page(u) for u in range(pps)],
        out_specs=pl.BlockSpec((1, 1, pps * PAGE_SIZE), lambda b, p, pt: (b, 0, p)),
    )
    return pl.pallas_call(
        functools.partial(_dsa_s_scores_kernel, pps),
        grid_spec=gs,
        out_shape=jax.ShapeDtypeStruct((bd, 1, npages * PAGE_SIZE), F32),
        compiler_params=_cparams("parallel", "arbitrary"),
        name="dsa_decode_scores",
    )(page_table.reshape(-1), qi3, wi3, *([cache_idx] * pps))


def _dsa_s_bounds_kernel(past, n_sel, idx_bits, sc_ref, qi_ref, kiw_ref, hm_ref, thr_o, j_o, kn_o, key_ref):
    rows = sc_ref.shape[0]
    kiw = kiw_ref[...]
    lane = lax.broadcasted_iota(I32, kiw.shape, 1)
    rolled = pltpu.roll(kiw, IDX_DIM, 1)
    ki2 = jnp.where(lane < IDX_DIM, kiw, rolled)
    w8 = jnp.where(lane < IDX_HEADS, rolled, 0.0) * IDX_HEADS ** -0.5
    prod = qi_ref[...] * jnp.concatenate([ki2] * (IDX_HEADS // 2), axis=1)
    s = _dot3(prod, hm_ref[...])
    new = jnp.sum(w8 * jnp.maximum(s * IDX_DIM ** -0.5, 0.0), axis=-1, keepdims=True)
    knew = _f2key(new)
    key_ref[:, 0:past] = _f2key(sc_ref[...])
    key_ref[:, past:past + LANE] = jnp.where(lane == 0, knew, INT_MIN)
    thr, jmax = _select_bounds(key_ref, (past + LANE) // LANE, rows, LANE, n_sel, idx_bits)
    thr_o[...] = thr
    j_o[...] = jmax
    kn_o[...] = knew


def _dsa_s_bounds(scores, qi, kiw, hm, past):
    rows = scores.shape[0]
    n_sel = min(TOPK_MAX, (past + 1) // 4)
    idx_bits = (past + LANE - 1).bit_length()
    out = jax.ShapeDtypeStruct((rows, 1), I32)
    return pl.pallas_call(
        functools.partial(_dsa_s_bounds_kernel, past, n_sel, idx_bits),
        out_shape=[out, out, out],
        scratch_shapes=[pltpu.VMEM((rows, past + LANE), I32)],
        compiler_params=pltpu.CompilerParams(vmem_limit_bytes=VMEM_LIMIT),
        name="dsa_decode_bounds",
    )(scores, qi, kiw, hm)


def _dsa_s_attend_kernel(pps, past, pt_ref, thr_ref, j_ref, kn_ref, qe_ref, sc_ref, kn_row_ref, vn_row_ref,
                         fold_ref, *refs):
    ck, cv, o_ref = refs[:pps], refs[pps:2 * pps], refs[2 * pps]
    m_ref, l_ref, acc_ref = refs[2 * pps + 1:]
    b = pl.program_id(0)
    p = pl.program_id(1)

    @pl.when(p == 0)
    def _():
        m_ref[...] = jnp.full(m_ref.shape, NEG_BIG, F32)
        l_ref[...] = jnp.zeros(l_ref.shape, F32)
        acc_ref[...] = jnp.zeros(acc_ref.shape, F32)

    thr, jmax = thr_ref[b], j_ref[b]
    qe = qe_ref[0]
    qeb = qe.astype(BF16)
    kidx = p * pps * PAGE_SIZE + lax.broadcasted_iota(I32, (1, pps * PAGE_SIZE), 1)
    kb = jnp.concatenate([ck[u][0, 0].astype(BF16) for u in range(pps)], axis=0)
    vb = jnp.concatenate([cv[u][0, 0].astype(BF16) for u in range(pps)], axis=0)
    lg = lax.dot_general(qeb, kb, _NT, preferred_element_type=F32) * HEAD_DIM ** -0.5
    key = _f2key(sc_ref[0])
    sel = (key > thr) | ((key == thr) & (kidx <= jmax))
    lg = jnp.where(sel, lg, NEG_BIG)
    m = m_ref[...]
    mn = jnp.maximum(m, jnp.max(lg, axis=-1, keepdims=True))
    pr = jnp.where(sel, jnp.exp(lg - mn), 0.0)
    alpha = jnp.exp(m - mn)
    l_ref[...] = alpha * l_ref[...] + jnp.sum(pr, axis=-1, keepdims=True)
    acc_ref[...] = alpha * acc_ref[...] + jnp.dot(pr.astype(BF16), vb, preferred_element_type=F32)
    m_ref[...] = mn

    @pl.when(p == pl.num_programs(1) - 1)
    def _():
        knew = kn_ref[b]
        sel_new = (knew > thr) | ((knew == thr) & (past <= jmax))
        lg = jnp.sum(qe * kn_row_ref[0], axis=-1, keepdims=True) * HEAD_DIM ** -0.5
        m = m_ref[...]
        mn = jnp.where(sel_new, jnp.maximum(m, lg), m)
        pr = jnp.where(sel_new, jnp.exp(lg - mn), 0.0)
        alpha = jnp.exp(m - mn)
        l = alpha * l_ref[...] + pr
        acc = alpha * acc_ref[...] + pr * vn_row_ref[0]
        rowi = lax.broadcasted_iota(I32, acc.shape, 0)
        lanei = lax.broadcasted_iota(I32, acc.shape, 1)
        own = jnp.where((lanei >> 6) == (rowi >> 1), acc / l, 0.0)
        o_ref[0] = _dot_sel(own, fold_ref[...])


def _dsa_s_attend(page_table, thr, jmax, knew, qe, scores, k_new, v_new, fold, cache_k, cache_v, layer):
    bd, npages = page_table.shape
    past = npages * PAGE_SIZE
    pps = PAGES_PER_STEP
    page = lambda u: pl.BlockSpec((1, 1, PAGE_SIZE, KV_W),
                                  lambda b, p, pt, t, j, kn: (layer, pt[b * npages + p * pps + u], 0, 0))
    per_b = lambda s: pl.BlockSpec((1,) + s, lambda b, p, pt, t, j, kn: (b, 0, 0))
    gs = pltpu.PrefetchScalarGridSpec(
        num_scalar_prefetch=4,
        grid=(bd, npages // pps),
        in_specs=[per_b((ATTN_HEADS, KV_W)),
                  pl.BlockSpec((1, 1, pps * PAGE_SIZE), lambda b, p, pt, t, j, kn: (b, 0, p)),
                  per_b((1, KV_W)), per_b((1, KV_W)),
                  pl.BlockSpec(fold.shape, lambda b, p, pt, t, j, kn: (0, 0))]
                 + [page(u) for u in range(pps)] * 2,
        out_specs=per_b((ATTN_HEADS, HEAD_DIM)),
        scratch_shapes=[pltpu.VMEM((ATTN_HEADS, 1), F32), pltpu.VMEM((ATTN_HEADS, 1), F32),
                        pltpu.VMEM((ATTN_HEADS, KV_W), F32)],
    )
    return pl.pallas_call(
        functools.partial(_dsa_s_attend_kernel, pps, past),
        grid_spec=gs,
        out_shape=jax.ShapeDtypeStruct((bd, ATTN_HEADS, HEAD_DIM), F32),
        compiler_params=_cparams("parallel", "arbitrary"),
        name="dsa_decode_attend",
    )(page_table.reshape(-1), thr.reshape(-1), jmax.reshape(-1), knew.reshape(-1),
      qe, scores, k_new, v_new, fold, *([cache_k] * pps), *([cache_v] * pps))


def _merge_ln_kernel(o_ref, bonus_ref, g_ref, at_ref, x_ref, wt_ref, wb_ref, xg_ref, xb_ref,
                     lg_ref, lb_ref, bd_ref, out_ref):
    bd = bd_ref[...]
    o = o_ref[...]
    mean = _dot_sel(o, bd) * (1.0 / HEAD_DIM)
    oc = o - mean
    var = _dot_sel(oc * oc, bd) * (1.0 / HEAD_DIM)
    rw = (oc * lax.rsqrt(var + GN_EPS) * xg_ref[...] + xb_ref[...] + bonus_ref[...]) * g_ref[...]
    f = (jnp.dot(rw.astype(BF16), wt_ref[...], preferred_element_type=F32)
         + jnp.dot(at_ref[...].astype(BF16), wb_ref[...], preferred_element_type=F32))
    out_ref[...] = _layer_norm(DEEPNORM_ALPHA * x_ref[...] + f, lg_ref[...], lb_ref[...])


def _merge_ln(o, bonus, g, at, x, wts, tm):
    n = x.shape[0]
    tm = min(tm, n)
    tok = lambda w: pl.BlockSpec((tm, w), lambda i: (i, 0))
    full = lambda a: pl.BlockSpec(a.shape, lambda i: (0, 0))
    ws = [wts["wo_top"], wts["wo_bot"], wts["lnx_g"], wts["lnx_b"], wts["ln1_g"], wts["ln1_b"], wts["bd"]]
    return pl.pallas_call(
        _merge_ln_kernel,
        grid=(n // tm,),
        in_specs=[tok(RWKV_WIDTH)] * 3 + [tok(ATTN_WIDTH), tok(D_MODEL)] + [full(a) for a in ws],
        out_specs=tok(D_MODEL),
        out_shape=jax.ShapeDtypeStruct((n, D_MODEL), F32),
        compiler_params=_cparams("parallel"),
        name="merge_ln",
    )(o, bonus, g, at, x, *ws)


def _take_top(src_ref, n_rows, val_ref, idx_ref):
    shape = src_ref.shape
    row = lax.broadcasted_iota(I32, shape, 1)

    def body(a, carry):
        sv = src_ref[...]
        m = jnp.max(sv, axis=1, keepdims=True)
        idx = jnp.min(jnp.where(sv == m, row, n_rows), axis=1, keepdims=True)
        val_ref[:, pl.ds(a, 1), :] = m
        idx_ref[:, pl.ds(a, 1), :] = idx
        src_ref[...] = jnp.where(row == idx, -jnp.inf, sv)
        return carry

    lax.fori_loop(0, PEER_TOPK, body, 0)


def _peer_route_kernel(x_ref, wq_ref, sk0_ref, sk1_ref, e_o, g_o, s_ref, t_ref, i_ref, c_ref, ts_ref, ic_ref):
    q = jnp.dot(x_ref[...].astype(BF16), wq_ref[...], preferred_element_type=F32)
    for h in range(PEER_HEADS):
        qh = q[:, h * PEER_DKEY:(h + 1) * PEER_DKEY]
        s_ref[h] = _dot3(sk0_ref[...], qh, _NT)
        s_ref[PEER_HEADS + h] = _dot3(sk1_ref[...], qh, _NT)
    _take_top(s_ref, N_KEYS, t_ref, i_ref)
    t1, t2 = t_ref[0:PEER_HEADS], t_ref[PEER_HEADS:2 * PEER_HEADS]
    c_ref[...] = jnp.concatenate([t1[:, a:a + 1, :] + t2 for a in range(PEER_TOPK)], axis=1)
    _take_top(c_ref, PEER_TOPK * PEER_TOPK, ts_ref, ic_ref)
    ic = ic_ref[...]
    i1, i2 = i_ref[0:PEER_HEADS], i_ref[PEER_HEADS:2 * PEER_HEADS]
    ia, ib = ic >> 4, ic & (PEER_TOPK - 1)
    e = jnp.zeros(ic.shape, I32)
    for a in range(PEER_TOPK):
        e = e + jnp.where(ia == a, i1[:, a:a + 1, :] * N_KEYS, 0) + jnp.where(ib == a, i2[:, a:a + 1, :], 0)
    ts = ts_ref[...]
    ex = jnp.exp(ts - jnp.max(ts, axis=1, keepdims=True))
    e_o[0] = e
    g_o[0] = ex / jnp.sum(ex, axis=1, keepdims=True)


def _peer_route(x, wq, sk0, sk1, tt):
    n = x.shape[0]
    nt = n // tt
    full = lambda a: pl.BlockSpec(a.shape, lambda i: (0, 0))
    out = pl.BlockSpec((1, PEER_HEADS, PEER_TOPK, tt), lambda i: (i, 0, 0, 0))
    return pl.pallas_call(
        _peer_route_kernel,
        grid=(nt,),
        in_specs=[pl.BlockSpec((tt, D_MODEL), lambda i: (i, 0)), full(wq), full(sk0), full(sk1)],
        out_specs=[out, out],
        out_shape=[jax.ShapeDtypeStruct((nt, PEER_HEADS, PEER_TOPK, tt), I32),
                   jax.ShapeDtypeStruct((nt, PEER_HEADS, PEER_TOPK, tt), F32)],
        scratch_shapes=[pltpu.VMEM((2 * PEER_HEADS, N_KEYS, tt), F32),
                        pltpu.VMEM((2 * PEER_HEADS, PEER_TOPK, tt), F32),
                        pltpu.VMEM((2 * PEER_HEADS, PEER_TOPK, tt), I32),
                        pltpu.VMEM((PEER_HEADS, PEER_TOPK * PEER_TOPK, tt), F32),
                        pltpu.VMEM((PEER_HEADS, PEER_TOPK, tt), F32),
                        pltpu.VMEM((PEER_HEADS, PEER_TOPK, tt), I32)],
        compiler_params=_cparams("parallel"),
        name="peer_route",
    )(x, wq, sk0, sk1)


def _peer_gather_kernel(tt, n_valid, idx_hbm, x_ref, g_ref, uv_hbm, lg_ref, lb_ref, o_ref, idx_smem, *rest):
    bufs, (sem, isem, y_ref) = rest[:GATHER_BUFFERS], rest[GATHER_BUFFERS:]
    i = pl.program_id(0)
    n_idx = PEER_SLOTS * tt
    islot = i % 2

    def idx_copy(tile, slot):
        return pltpu.make_async_copy(idx_hbm.at[tile], idx_smem.at[pl.ds(slot * n_idx, n_idx)], isem.at[slot])

    @pl.when(i == 0)
    def _():
        idx_copy(0, 0).start()

    if n_valid < tt:
        y_ref[...] = jnp.zeros(y_ref.shape, F32)
    idx_copy(i, islot).wait()

    @pl.when(i + 1 < pl.num_programs(0))
    def _():
        idx_copy(i + 1, 1 - islot).start()

    base = islot * n_idx

    def issue(t, k):
        tok = base + t * PEER_SLOTS
        for s in range(PEER_SLOTS):
            e = idx_smem[tok + s]
            pltpu.async_copy(uv_hbm.at[pl.ds(e, 1)], bufs[k].at[pl.ds(s, 1)], sem.at[k], priority=s % 2)

    def wait(k):
        pltpu.make_async_copy(uv_hbm.at[pl.ds(0, PEER_SLOTS)], bufs[k], sem.at[k]).wait()

    lane = lax.broadcasted_iota(I32, (PEER_SLOTS, tt), 1)

    def compute(t, k):
        xrow = x_ref[pl.ds(t, 1), :]
        word = bufs[k][...]
        u = lax.bitcast_convert_type(word << 16, F32)
        v = lax.bitcast_convert_type(word & jnp.int32(-65536), F32)
        h = jnp.sum(u * xrow, axis=-1, keepdims=True)
        gate = jnp.sum(jnp.where(lane == t, g_ref[0], 0.0), axis=-1, keepdims=True)
        coef = gate * jax.nn.gelu(h)
        y_ref[pl.ds(t, 1), :] = jnp.sum(coef * v, axis=0, keepdims=True)

    depth = len(bufs)
    ahead = depth - 1
    rounds = max(n_valid - ahead, 0) // depth
    for t in range(min(ahead, n_valid)):
        issue(t, t % depth)

    def body(r, carry):
        for k in range(depth):
            t = r * depth + k
            wait(k)
            issue(t + ahead, (k + ahead) % depth)
            compute(t, k)
        return carry

    lax.fori_loop(0, rounds, body, 0)
    for t in range(rounds * depth, n_valid):
        wait(t % depth)
        if t + ahead < n_valid:
            issue(t + ahead, (t + ahead) % depth)
        compute(t, t % depth)
    o_ref[...] = _layer_norm(DEEPNORM_ALPHA * x_ref[...] + y_ref[...], lg_ref[...], lb_ref[...])


def _peer_gather(idx, gates, x, uv, ln_g, ln_b, tt, n_valid):
    n = x.shape[0]
    nt = n // tt
    row = pl.BlockSpec((1, D_MODEL), lambda i: (0, 0))
    return pl.pallas_call(
        functools.partial(_peer_gather_kernel, tt, n_valid),
        grid=(nt,),
        in_specs=[pl.BlockSpec(memory_space=pl.ANY),
                  pl.BlockSpec((tt, D_MODEL), lambda i: (i, 0)),
                  pl.BlockSpec((1, PEER_SLOTS, tt), lambda i: (i, 0, 0)),
                  pl.BlockSpec(memory_space=pl.ANY), row, row],
        out_specs=pl.BlockSpec((tt, D_MODEL), lambda i: (i, 0)),
        out_shape=jax.ShapeDtypeStruct((n, D_MODEL), F32),
        scratch_shapes=[pltpu.SMEM((2 * PEER_SLOTS * tt,), I32)]
                       + [pltpu.VMEM((PEER_SLOTS, D_MODEL), I32)] * GATHER_BUFFERS
                       + [pltpu.SemaphoreType.DMA((GATHER_BUFFERS,)),
                          pltpu.SemaphoreType.DMA((2,)),
                          pltpu.VMEM((tt, D_MODEL), F32)],
        compiler_params=_cparams("arbitrary"),
        name="peer_gather",
    )(idx, x, gates, uv, ln_g, ln_b)


def _peer(x, wts, tt, n_valid):
    e, gates = _peer_route(x, wts["wq"], wts["sk0"], wts["sk1"], tt)
    nt = x.shape[0] // tt
    e_tok = e.reshape(nt, PEER_SLOTS, tt).transpose(0, 2, 1)
    return _peer_gather(e_tok.reshape(nt, PEER_SLOTS * tt), gates.reshape(nt, PEER_SLOTS, tt), x,
                        wts["uv"], wts["ln2_g"], wts["ln2_b"], tt, n_valid)


def _pack_bf16_pair(lo, hi):
    bits = lambda a: lax.bitcast_convert_type(a.astype(BF16), jnp.uint16).astype(jnp.uint32)
    return lax.bitcast_convert_type(bits(lo) | (bits(hi) << 16), I32)


def _layer_weights(l, w_in, shift_mu, decay_w0, decay_up, iclr_a0, iclr_up, gate_up, k_k, k_a, r_k,
                   lnx_g, lnx_b, idx_ln_g, idx_ln_b, w_out, ln1_g, ln1_b, ln2_g, ln2_b,
                   peer_wq, peer_subkeys, peer_u, peer_v):
    row = lambda a: a.reshape(1, -1).astype(F32)
    lora = jnp.zeros((LORA_W, 3 * RWKV_WIDTH), F32)
    lora = lora.at[0:W_LORA, 0:RWKV_WIDTH].set(decay_up[l])
    lora = lora.at[W_LORA:W_LORA + A_LORA, RWKV_WIDTH:2 * RWKV_WIDTH].set(iclr_up[l])
    lora = lora.at[W_LORA + A_LORA:, 2 * RWKV_WIDTH:].set(gate_up[l])
    lora_hi = lora.astype(BF16)
    pad_lane = lambda a: jnp.pad(a.reshape(1, -1), ((0, 0), (0, LANE - a.size)))
    half = PEER_DKEY // 2
    seg = jnp.arange(RWKV_WIDTH) // HEAD_DIM
    return dict(
        w_rw=w_in[l][:, :SHIFT_W].astype(BF16),
        w_at=jnp.pad(w_in[l][:, SHIFT_W:], ((0, 0), (0, AT_PAD_W - ATTN_PROJ_W))).astype(BF16),
        mu=row(shift_mu[l]), w0=row(decay_w0[l]), a0=row(iclr_a0[l]),
        lora_hi=lora_hi, lora_lo=(lora - lora_hi.astype(F32)).astype(BF16),
        k_k=row(k_k[l]), k_a=row(k_a[l]), r_k=row(r_k[l]),
        bd=(seg[:, None] == seg[None, :]).astype(BF16),
        lnx_g=row(lnx_g[l]), lnx_b=row(lnx_b[l]),
        idx_g=pad_lane(idx_ln_g[l]), idx_b=pad_lane(idx_ln_b[l]),
        wo_top=w_out[l][:RWKV_WIDTH].astype(BF16), wo_bot=w_out[l][RWKV_WIDTH:].astype(BF16),
        ln1_g=row(ln1_g[l]), ln1_b=row(ln1_b[l]), ln2_g=row(ln2_g[l]), ln2_b=row(ln2_b[l]),
        wq=peer_wq[l].astype(BF16),
        sk0=jnp.pad(peer_subkeys[l, 0], ((0, 0), (0, half))),
        sk1=jnp.pad(peer_subkeys[l, 1], ((0, 0), (half, 0))),
        uv=_pack_bf16_pair(peer_u[l], peer_v[l]),
    )


def _tile(n, cap):
    if n <= cap:
        return n
    return max(d for d in range(8, cap + 1, 8) if n % d == 0)


def _mixer_front(x, wts, tabs):
    tm = _tile(x.shape[0], 512)
    p_rw = _matmul(x, wts["w_rw"], tm)
    p_at = _matmul(x, wts["w_at"], tm)
    q, k, qi, kiw = _attn_prep(p_at, tabs, wts["idx_g"], wts["idx_b"], _tile(tabs[0].shape[0], 640))
    v = p_at[:, ATTN_WIDTH + KV_W:ATTN_WIDTH + 2 * KV_W]
    return p_rw, q, k, v, qi, kiw


def kernel(x_prompt, x_sample, cache_k, cache_v, cache_idx_k, state_wkv, state_shift, page_table, meta_tokens, w_in, shift_mu, decay_w0, decay_up, iclr_a0, iclr_up, gate_up, k_k, k_a, r_k, lnx_g, lnx_b, idx_ln_g, idx_ln_b, w_out, ln1_g, ln1_b, ln2_g, ln2_b, peer_wq, peer_subkeys, peer_u, peer_v):
    bsz, seq, _ = x_prompt.shape
    t = seq + N_META
    tp = -(-t // LANE) * LANE
    n_p = bsz * tp
    depth = w_in.shape[0]
    bd_, ts_, _ = x_sample.shape
    assert ts_ == 1
    npages = page_table.shape[1]
    past = npages * PAGE_SIZE
    n_pool = cache_k.shape[1]
    peer_tt = LANE
    ns_pad = -(-bd_ // peer_tt) * peer_tt

    xp = jnp.concatenate([jnp.broadcast_to(meta_tokens[None], (bsz, N_META, D_MODEL)), x_prompt], axis=1)
    xp = jnp.pad(xp, ((0, 0), (0, tp - t), (0, 0))).reshape(n_p, D_MODEL)
    xs = x_sample.reshape(bd_, D_MODEL)
    tabs_p = _rope_tables(jnp.arange(tp, dtype=I32))
    tabs_s = _rope_tables(jnp.full((bd_,), past, I32))
    ck = cache_k.reshape(depth, n_pool, PAGE_SIZE, KV_W)
    cv = cache_v.reshape(depth, n_pool, PAGE_SIZE, KV_W)
    hsel = jnp.arange(IDX_HEADS * IDX_DIM)[:, None] // IDX_DIM == jnp.arange(LANE)[None, :]
    hm = hsel.astype(F32)
    fold = (jnp.arange(KV_W)[:, None] % HEAD_DIM == jnp.arange(HEAD_DIM)[None, :]).astype(BF16)
    own = (jnp.arange(KV_W)[None, :] // HEAD_DIM == jnp.arange(ATTN_HEADS)[:, None] // (ATTN_HEADS // KV_HEADS))

    k_p, v_p, ki_p, wkv_p, sh_p = [], [], [], [], []
    k_s, v_s, ki_s, wkv_s, sh_s = [], [], [], [], []
    for l in range(depth):
        wts = _layer_weights(l, w_in, shift_mu, decay_w0, decay_up, iclr_a0, iclr_up, gate_up, k_k, k_a, r_k,
                             lnx_g, lnx_b, idx_ln_g, idx_ln_b, w_out, ln1_g, ln1_b, ln2_g, ln2_b,
                             peer_wq, peer_subkeys, peer_u, peer_v)

        p_rw, q, k, v, qi, kiw = _mixer_front(xp, wts, tabs_p)
        p_rw3 = p_rw.reshape(bsz, tp, SHIFT_W)
        r_, w_, kt_, kk_, b_, vv_, bonus, g = _rwkv_prep(
            p_rw3, jnp.zeros((bsz, 1, SHIFT_W), F32), t, True, wts, LANE)
        o, s_fin = _wkv(r_, w_, kt_, kk_, b_, vv_,
                        jnp.zeros((bsz, RWKV_HEADS, HEAD_DIM, HEAD_DIM), F32), bsz, LANE)
        three = lambda a: a.reshape(bsz, tp, -1)
        at = _dsa_prompt(three(q).astype(BF16), three(qi).astype(BF16),
                         three(kiw)[:, :, IDX_DIM:IDX_DIM + IDX_HEADS],
                         three(k).astype(BF16), three(v).astype(BF16),
                         three(kiw)[:, :, :IDX_DIM].astype(BF16), t)
        flat = lambda a: a.reshape(n_p, -1)
        x1 = _merge_ln(flat(o), flat(bonus), flat(g), flat(at), xp, wts, _tile(n_p, 256))
        xp = _peer(x1, wts, peer_tt, peer_tt)
        k_p.append(three(k)[:, :t].reshape(bsz, t, KV_HEADS, HEAD_DIM))
        v_p.append(three(v)[:, :t].reshape(bsz, t, KV_HEADS, HEAD_DIM))
        ki_p.append(three(kiw)[:, :t, :IDX_DIM])
        wkv_p.append(s_fin)
        sh_p.append(p_rw3[:, t - 1])

        p_rw, q, k, v, qi, kiw = _mixer_front(xs, wts, tabs_s)
        r_, w_, kt_, kk_, b_, vv_, bonus, g = _rwkv_prep(
            p_rw[None], state_shift[l][None], bd_, False, wts, bd_)
        tc_s = 8
        padt = lambda a, c: jnp.pad(a[0][:, None, :], ((0, 0), (0, tc_s - 1), (0, 0)), constant_values=c)
        o, s_fin = _wkv(padt(r_, 0.0), padt(w_, 1.0), padt(kt_, 0.0), padt(kk_, 0.0), padt(b_, 0.0),
                        padt(vv_, 0.0), state_wkv[l].astype(F32), 4, tc_s)
        o = o[:, 0]
        scores = _dsa_s_scores(page_table, qi.reshape(bd_, IDX_HEADS, IDX_DIM),
                               kiw[:, IDX_DIM:IDX_DIM + IDX_HEADS].reshape(bd_, IDX_HEADS, 1),
                               cache_idx_k, l)
        thr, jmax, knew = _dsa_s_bounds(scores.reshape(bd_, past), qi, kiw, hm, past)
        qe = jnp.where(own[None], jnp.tile(q.reshape(bd_, ATTN_HEADS, HEAD_DIM), (1, 1, KV_HEADS)), 0.0)
        at = _dsa_s_attend(page_table, thr, jmax, knew, qe, scores, k[:, None, :], v[:, None, :], fold,
                           ck, cv, l).reshape(bd_, ATTN_WIDTH)
        x1 = _merge_ln(o, bonus[0], g[0], at, xs, wts, bd_)
        x1p = jnp.pad(x1, ((0, ns_pad - bd_), (0, 0)))
        xs = _peer(x1p, wts, peer_tt, bd_)[:bd_]
        k_s.append(k.reshape(bd_, 1, KV_HEADS, HEAD_DIM))
        v_s.append(v.reshape(bd_, 1, KV_HEADS, HEAD_DIM))
        ki_s.append(kiw[:, None, :IDX_DIM])
        wkv_s.append(s_fin)
        sh_s.append(p_rw)

    y_prompt = xp.reshape(bsz, tp, D_MODEL)[:, N_META:t]
    y_sample = xs.reshape(bd_, 1, D_MODEL)
    return (y_prompt, y_sample, jnp.stack(k_p), jnp.stack(v_p), jnp.stack(ki_p),
            jnp.stack(wkv_p).astype(state_wkv.dtype), jnp.stack(sh_p).astype(state_shift.dtype),
            jnp.stack(k_s), jnp.stack(v_s), jnp.stack(ki_s),
            jnp.stack(wkv_s).astype(state_wkv.dtype), jnp.stack(sh_s).astype(state_shift.dtype))
```

```python
import functools
import math

import jax
import jax.numpy as jnp
from jax import lax
from jax.experimental import pallas as pl
from jax.experimental.pallas import tpu as pltpu

F32 = jnp.float32
BF16 = jnp.bfloat16
I32 = jnp.int32

D_MODEL = 1024
N_META = 16
HEAD_DIM = 64
RWKV_WIDTH = D_MODEL // 2
RWKV_HEADS = RWKV_WIDTH // HEAD_DIM
ATTN_WIDTH = D_MODEL - RWKV_WIDTH
ATTN_HEADS = ATTN_WIDTH // HEAD_DIM
KV_HEADS = ATTN_HEADS // 2
KV_W = KV_HEADS * HEAD_DIM
W_LORA = 64
A_LORA = 64
G_LORA = 128
LORA_W = W_LORA + A_LORA + G_LORA
SHIFT_W = 3 * RWKV_WIDTH + LORA_W
IDX_HEADS = 8
IDX_DIM = 64
ATTN_PROJ_W = ATTN_WIDTH + 2 * KV_W + IDX_HEADS * IDX_DIM + IDX_DIM + IDX_HEADS
TOPK_MAX = 256
ROPE_THETA = 500000.0
ROT = HEAD_DIM // 4
ROT_HALF = ROT // 2
PEER_HEADS = 8
PEER_DKEY = 128
N_KEYS = 128
PEER_TOPK = 16
PEER_SLOTS = PEER_HEADS * PEER_TOPK
DEPTH = 2
DEEPNORM_ALPHA = (2.0 * DEPTH) ** 0.25
PAGE_SIZE = 128
LN_EPS = 1e-5
GN_EPS = 64e-5

LANE = 128
SUBLANE = 8
Q_BLOCK = 128
INT_MIN = -(2 ** 31)
NEG_BIG = -1e30
VMEM_LIMIT = 56 * 1024 * 1024
AT_PAD_W = 1664
KI_OFF = ATTN_WIDTH + 2 * KV_W + IDX_HEADS * IDX_DIM
PAGES_PER_STEP = 16
GATHER_BUFFERS = 6

_NT = (((1,), (1,)), ((), ()))


def _cparams(*sem):
    return pltpu.CompilerParams(dimension_semantics=sem, vmem_limit_bytes=VMEM_LIMIT)


def _split2(x):
    hi = x.astype(BF16)
    lo = (x - hi.astype(F32)).astype(BF16)
    return hi, lo


def _split3(x):
    hi = x.astype(BF16)
    r1 = x - hi.astype(F32)
    mid = r1.astype(BF16)
    lo = (r1 - mid.astype(F32)).astype(BF16)
    return hi, mid, lo


def _dot3(a, b, dims=None):
    ah, al = _split2(a)
    bh, bl = _split2(b)
    if dims is None:
        d = lambda p, q: jnp.dot(p, q, preferred_element_type=F32)
    else:
        d = lambda p, q: lax.dot_general(p, q, dims, preferred_element_type=F32)
    return d(ah, bh) + d(al, bh) + d(ah, bl)


def _dot_sel(x, m):
    h, mid, lo = _split3(x)
    d = lambda p: jnp.dot(p, m, preferred_element_type=F32)
    return d(h) + d(mid) + d(lo)


def _f2key(x):
    x = jnp.where(x == 0.0, 0.0, x)
    b = lax.bitcast_convert_type(x, I32)
    return b ^ ((b >> 31) & 0x7FFFFFFF)


def _layer_norm(z, g, b):
    mu = jnp.mean(z, axis=-1, keepdims=True)
    zc = z - mu
    var = jnp.mean(zc * zc, axis=-1, keepdims=True)
    return zc * lax.rsqrt(var + LN_EPS) * g + b


def _mm_kernel(x_ref, w_ref, o_ref):
    o_ref[...] = jnp.dot(x_ref[...].astype(BF16), w_ref[...], preferred_element_type=F32)


def _matmul(x, w, tm):
    m, k = x.shape
    n = w.shape[1]
    tm = min(tm, m)
    return pl.pallas_call(
        _mm_kernel,
        grid=(m // tm,),
        in_specs=[pl.BlockSpec((tm, k), lambda i: (i, 0)),
                  pl.BlockSpec((k, n), lambda i: (0, 0))],
        out_specs=pl.BlockSpec((tm, n), lambda i: (i, 0)),
        out_shape=jax.ShapeDtypeStruct((m, n), F32),
        compiler_params=_cparams("parallel"),
        name="proj_matmul",
    )(x, w)


def _rwkv_prep_kernel(t_real, tt, shift, p_ref, prev_ref, mu_ref, w0_ref, a0_ref, lwh_ref, lwl_ref,
                      kk_ref, ka_ref, rk_ref, bd_ref,
                      r_o, w_o, kt_o, kko_o, b_o, v_o, bonus_o, g_o, carry_ref):
    j = pl.program_id(1)
    pf = p_ref[0]
    if shift:
        @pl.when(j == 0)
        def _():
            carry_ref[...] = prev_ref[0]
        row = lax.broadcasted_iota(I32, pf.shape, 0)
        prev = jnp.where(row == 0, carry_ref[...], pltpu.roll(pf, 1, 0))
        carry_ref[...] = pf[tt - 1:tt, :]
    else:
        prev = prev_ref[0]
    xs = pf + mu_ref[...] * (prev - pf)
    r = xs[:, 0:RWKV_WIDTH]
    k = xs[:, RWKV_WIDTH:2 * RWKV_WIDTH]
    v = xs[:, 2 * RWKV_WIDTH:3 * RWKV_WIDTH]
    z = xs[:, 3 * RWKV_WIDTH:SHIFT_W]
    lane = lax.broadcasted_iota(I32, z.shape, 1)
    zt = jnp.where(lane < W_LORA, jnp.tanh(z),
                   jnp.where(lane < W_LORA + A_LORA, z, jax.nn.sigmoid(z)))
    zh, zl = _split2(zt)
    d = lambda p, q: jnp.dot(p, q, preferred_element_type=F32)
    lo = d(zh, lwh_ref[...]) + d(zl, lwh_ref[...]) + d(zh, lwl_ref[...])
    w_raw = w0_ref[...] + lo[:, 0:RWKV_WIDTH]
    a = jax.nn.sigmoid(a0_ref[...] + lo[:, RWKV_WIDTH:2 * RWKV_WIDTH])
    g = lo[:, 2 * RWKV_WIDTH:3 * RWKV_WIDTH]
    decay = jnp.exp(-math.exp(-0.5) * jax.nn.sigmoid(w_raw))
    bd = bd_ref[...]
    kk = k * kk_ref[...]
    kk = kk / jnp.maximum(jnp.sqrt(_dot_sel(kk * kk, bd)), 1e-12)
    kt = k * (1.0 + (a - 1.0) * ka_ref[...])
    bonus = _dot_sel(r * kt * rk_ref[...], bd) * v
    pos = j * tt + lax.broadcasted_iota(I32, r.shape, 0)
    valid = pos < t_real
    r_o[0] = r
    w_o[0] = jnp.where(valid, decay, 1.0)
    kt_o[0] = jnp.where(valid, kt, 0.0)
    kko_o[0] = jnp.where(valid, kk, 0.0)
    b_o[0] = jnp.where(valid, kk * a, 0.0)
    v_o[0] = v
    bonus_o[0] = bonus
    g_o[0] = g


def _rwkv_prep(p_rw, prev, t_real, shift, wts, tt):
    bsz, tp, _ = p_rw.shape
    tt = min(tt, tp)
    row = lambda n: pl.BlockSpec((1, n), lambda b, j: (0, 0))
    full = lambda a: pl.BlockSpec(a.shape, lambda b, j: (0, 0))
    tok = lambda n: pl.BlockSpec((1, tt, n), lambda b, j: (b, j, 0))
    prev_spec = pl.BlockSpec((1, 1, SHIFT_W), lambda b, j: (b, 0, 0)) if shift else tok(SHIFT_W)
    outs = pl.pallas_call(
        functools.partial(_rwkv_prep_kernel, t_real, tt, shift),
        grid=(bsz, tp // tt),
        in_specs=[tok(SHIFT_W), prev_spec, row(SHIFT_W), row(RWKV_WIDTH), row(RWKV_WIDTH),
                  full(wts["lora_hi"]), full(wts["lora_lo"]),
                  row(RWKV_WIDTH), row(RWKV_WIDTH), row(RWKV_WIDTH), full(wts["bd"])],
        out_specs=[tok(RWKV_WIDTH)] * 8,
        out_shape=[jax.ShapeDtypeStruct((bsz, tp, RWKV_WIDTH), F32)] * 8,
        scratch_shapes=[pltpu.VMEM((1, SHIFT_W), F32)],
        compiler_params=_cparams("parallel", "arbitrary"),
        name="rwkv_prep",
    )(p_rw, prev, wts["mu"], wts["w0"], wts["a0"], wts["lora_hi"], wts["lora_lo"],
      wts["k_k"], wts["k_a"], wts["r_k"], wts["bd"])
    return outs


def _wkv_kernel(bb, tc, r_ref, w_ref, kt_ref, kk_ref, b_ref, v_ref, s0_ref, o_ref, sf_ref, s_ref):
    c = pl.program_id(1)

    @pl.when(c == 0)
    def _():
        s_ref[...] = s0_ref[...]

    lane = lax.broadcasted_iota(I32, (HEAD_DIM, LANE), 1)
    row = lax.broadcasted_iota(I32, (HEAD_DIM, LANE), 0)
    lo = lane < HEAD_DIM
    e0 = lane == row
    e1 = lane == row + HEAD_DIM
    e01 = e0 | e1
    r128 = lax.broadcasted_iota(I32, (LANE, LANE), 0)
    l128 = lax.broadcasted_iota(I32, (LANE, LANE), 1)
    half_ones = ((r128 >> 6) == (l128 >> 6)).astype(BF16)
    npair = RWKV_HEADS // 2

    def half_sums_mxu(parts, n_split):
        res = jnp.dot(jnp.concatenate(parts, axis=0), half_ones, preferred_element_type=F32)
        out = []
        for i in range(len(parts) // n_split):
            acc = res[i * n_split * HEAD_DIM:(i * n_split + 1) * HEAD_DIM]
            for p in range(1, n_split):
                acc = acc + res[(i * n_split + p) * HEAD_DIM:(i * n_split + p + 1) * HEAD_DIM]
            out.append(acc)
        return out

    def group(gi, carry):
        t0 = pl.multiple_of(gi * SUBLANE, SUBLANE)
        rows = pl.ds(t0, SUBLANE)
        blk = lambda ref, b: [ref[b, rows, j * LANE:(j + 1) * LANE] for j in range(npair)]

        def v_pieces(b):
            out = []
            for v8 in blk(v_ref, b):
                vh = v8.astype(BF16).astype(F32)
                out.append((vh, v8 - vh))
            return out

        def v_columns(pieces, u):
            parts = []
            for j in range(npair):
                parts += [jnp.where(e01, pc[u:u + 1], 0.0).astype(BF16) for pc in pieces[j]]
            return half_sums_mxu(parts, 2)

        def out_rows(qparts):
            return [jnp.sum(jnp.where(e01, oc, 0.0), axis=0, keepdims=True) for oc in half_sums_mxu(qparts, 2)]

        pieces = v_pieces(0)
        vcols = [v_columns(pieces, u) for u in range(SUBLANE)]
        pending = None
        for b in range(bb + 1):
            if b < bb:
                kk8, w8, b8, kt8, r8 = blk(kk_ref, b), blk(w_ref, b), blk(b_ref, b), blk(kt_ref, b), blk(r_ref, b)
                st = [s_ref[b, j] for j in range(npair)]
                nxt_pieces = v_pieces(b + 1) if b + 1 < bb else None
            nxt_vcols, qsteps, orows = [], [], []
            for u in range(SUBLANE):
                if b < bb:
                    prods = [st[j] * kk8[j][u:u + 1] for j in range(npair)]
                    sums = [(jnp.sum(jnp.where(lo, p, 0.0), axis=-1, keepdims=True),
                             jnp.sum(jnp.where(lo, 0.0, p), axis=-1, keepdims=True)) for p in prods]
                if nxt_pieces is not None:
                    nxt_vcols.append(v_columns(nxt_pieces, u))
                if pending is not None:
                    orows.append(out_rows(pending[1][u]))
                if b < bb:
                    qs = []
                    for j in range(npair):
                        skk = jnp.where(lo, sums[j][0], sums[j][1])
                        s = st[j] * w8[j][u:u + 1] - skk * b8[j][u:u + 1] + vcols[u][j] * kt8[j][u:u + 1]
                        st[j] = s
                        q = s * r8[j][u:u + 1]
                        qh = q.astype(BF16)
                        qs += [qh, (q - qh.astype(F32)).astype(BF16)]
                    qsteps.append(qs)
            if pending is not None:
                pb = pending[0]
                for j in range(npair):
                    o_ref[pb, rows, j * LANE:(j + 1) * LANE] = jnp.concatenate([orows[u][j] for u in range(SUBLANE)],
                                                                               axis=0)
            if b < bb:
                for j in range(npair):
                    s_ref[b, j] = st[j]
                pending = (b, qsteps)
                vcols = nxt_vcols
                nxt_pieces = None
        return carry

    lax.fori_loop(0, tc // SUBLANE, group, 0)

    @pl.when(c == pl.num_programs(1) - 1)
    def _():
        sf_ref[...] = s_ref[...]


def _pair_state(s):
    b = s.shape[0]
    return (s.reshape(b, RWKV_HEADS // 2, 2, HEAD_DIM, HEAD_DIM)
            .transpose(0, 1, 3, 2, 4).reshape(b, RWKV_HEADS // 2, HEAD_DIM, LANE))


def _unpair_state(s):
    b = s.shape[0]
    return (s.reshape(b, RWKV_HEADS // 2, HEAD_DIM, 2, HEAD_DIM)
            .transpose(0, 1, 3, 2, 4).reshape(b, RWKV_HEADS, HEAD_DIM, HEAD_DIM))


def _wkv(r, w, kt, kk, bv, v, s0, bb, tc):
    bsz, tp, _ = r.shape
    tc = min(tc, tp)
    tok = pl.BlockSpec((bb, tc, RWKV_WIDTH), lambda i, c: (i, c, 0))
    st = pl.BlockSpec((bb, RWKV_HEADS // 2, HEAD_DIM, LANE), lambda i, c: (i, 0, 0, 0))
    o, sf = pl.pallas_call(
        functools.partial(_wkv_kernel, bb, tc),
        grid=(bsz // bb, tp // tc),
        in_specs=[tok] * 6 + [st],
        out_specs=[tok, st],
        out_shape=[jax.ShapeDtypeStruct((bsz, tp, RWKV_WIDTH), F32),
                   jax.ShapeDtypeStruct((bsz, RWKV_HEADS // 2, HEAD_DIM, LANE), F32)],
        scratch_shapes=[pltpu.VMEM((bb, RWKV_HEADS // 2, HEAD_DIM, LANE), F32)],
        compiler_params=_cparams("parallel", "arbitrary"),
        name="wkv_scan",
    )(r, w, kt, kk, bv, v, _pair_state(s0))
    return o, _unpair_state(sf)


def _rope(x, c, sa, sb):
    w = x.shape[1]
    return x * c + pltpu.roll(x, w - ROT_HALF, 1) * sa + pltpu.roll(x, ROT_HALF, 1) * sb


def _attn_prep_kernel(p_ref, c_ref, sa_ref, sb_ref, g_ref, b_ref, q_o, k_o, qi_o, ki_o):
    c1, sa1, sb1 = c_ref[...], sa_ref[...], sb_ref[...]
    rep = lambda t, n: jnp.concatenate([t] * n, axis=1)
    nq = ATTN_WIDTH // LANE
    nk = KV_W // LANE
    q_o[...] = _rope(p_ref[:, 0:ATTN_WIDTH], rep(c1, nq), rep(sa1, nq), rep(sb1, nq))
    k_o[...] = _rope(p_ref[:, ATTN_WIDTH:ATTN_WIDTH + KV_W], rep(c1, nk), rep(sa1, nk), rep(sb1, nk))
    qi0 = ATTN_WIDTH + 2 * KV_W
    qi_o[...] = _rope(p_ref[:, qi0:qi0 + IDX_HEADS * IDX_DIM], rep(c1, nq), rep(sa1, nq), rep(sb1, nq))
    x = p_ref[:, KI_OFF:KI_OFF + LANE]
    lane = lax.broadcasted_iota(I32, x.shape, 1)
    isk = lane < IDX_DIM
    mu = jnp.sum(jnp.where(isk, x, 0.0), axis=-1, keepdims=True) * (1.0 / IDX_DIM)
    xc = jnp.where(isk, x - mu, 0.0)
    var = jnp.sum(xc * xc, axis=-1, keepdims=True) * (1.0 / IDX_DIM)
    y = xc * lax.rsqrt(var + LN_EPS) * g_ref[...] + b_ref[...]
    y = _rope(y, jnp.where(isk, c1, 1.0), jnp.where(isk, sa1, 0.0), jnp.where(isk, sb1, 0.0))
    ki_o[...] = jnp.where(isk, y, x)


def _attn_prep(p_at, tabs, idx_g, idx_b, tm):
    n = p_at.shape[0]
    tm = min(tm, n)
    tpb = tabs[0].shape[0] // tm
    tok = lambda w: pl.BlockSpec((tm, w), lambda i: (i, 0))
    tab = pl.BlockSpec((tm, LANE), lambda i: (i % tpb, 0))
    row = pl.BlockSpec((1, LANE), lambda i: (0, 0))
    return pl.pallas_call(
        _attn_prep_kernel,
        grid=(n // tm,),
        in_specs=[tok(AT_PAD_W), tab, tab, tab, row, row],
        out_specs=[tok(ATTN_WIDTH), tok(KV_W), tok(IDX_HEADS * IDX_DIM), tok(LANE)],
        out_shape=[jax.ShapeDtypeStruct((n, ATTN_WIDTH), F32), jax.ShapeDtypeStruct((n, KV_W), F32),
                   jax.ShapeDtypeStruct((n, IDX_HEADS * IDX_DIM), F32), jax.ShapeDtypeStruct((n, LANE), F32)],
        compiler_params=_cparams("parallel"),
        name="attn_prep",
    )(p_at, tabs[0], tabs[1], tabs[2], idx_g, idx_b)


def _rope_tables(pos):
    inv = ROPE_THETA ** (-jnp.arange(ROT_HALF, dtype=F32) * 2.0 / ROT)
    ang = pos.astype(F32)[:, None] * inv[None, :]
    cos, sin = jnp.cos(ang), jnp.sin(ang)
    n = pos.shape[0]
    rest = HEAD_DIM - ROT
    c = jnp.concatenate([cos, cos, jnp.ones((n, rest), F32)], axis=1)
    sa = jnp.concatenate([-sin, jnp.zeros((n, rest + ROT_HALF), F32)], axis=1)
    sb = jnp.concatenate([jnp.zeros((n, ROT_HALF), F32), sin, jnp.zeros((n, rest), F32)], axis=1)
    two = lambda t: jnp.concatenate([t, t], axis=1)
    return two(c), two(sa), two(sb)


def _select_bounds(key_ref, n_tiles, rows, kt, k_sel, idx_bits):
    def count(pred):
        def body(i, acc):
            off = pl.multiple_of(i * kt, kt)
            idx = off + lax.broadcasted_iota(I32, (rows, kt), 1)
            hit = jnp.where(pred(key_ref[:, pl.ds(off, kt)], idx), 1.0, 0.0)
            for c in range(kt // LANE):
                acc = acc + hit[:, c * LANE:(c + 1) * LANE]
            return acc
        acc = lax.fori_loop(0, n_tiles, body, jnp.zeros((rows, LANE), F32))
        return jnp.sum(acc, axis=-1, keepdims=True)

    def thr_more(c):
        i, _, n_ge = c
        return (i < 32) & (jnp.max(jnp.abs(n_ge - k_sel)) > 0.0)

    def thr_bit(c):
        i, res, n_ge = c
        cand = res + jnp.left_shift(jnp.int32(1), 31 - i)
        cnt = count(lambda key, idx: key >= cand)
        take = cnt >= k_sel
        return i + 1, jnp.where(take, cand, res), jnp.where(take, cnt, n_ge)

    n_all = jnp.full((rows, 1), 1.0, F32) * (n_tiles * kt)
    _, thr, n_ge = lax.while_loop(thr_more, thr_bit, (jnp.int32(0), jnp.full((rows, 1), INT_MIN, I32), n_all))

    def tie_search():
        need = k_sel - count(lambda key, idx: key > thr)

        def idx_bit(i, res):
            cand = res | jnp.left_shift(jnp.int32(1), idx_bits - 1 - i)
            c = count(lambda key, idx: (key == thr) & (idx < cand))
            return jnp.where(c < need, cand, res)

        return lax.fori_loop(0, idx_bits, idx_bit, jnp.zeros((rows, 1), I32))

    jmax = lax.cond(jnp.max(n_ge) > k_sel, tie_search,
                    lambda: jnp.full((rows, 1), 2 ** idx_bits - 1, I32))
    return thr, jmax


def _dsa_prompt_kernel(kt, n_sel, idx_bits, q_ref, qi_ref, wi_ref, k_ref, v_ref, ki_ref, o_ref,
                       key_ref, m_ref, l_ref, acc_ref):
    i = pl.program_id(1)
    n_kt = (i * Q_BLOCK + Q_BLOCK + kt - 1) // kt
    qi = qi_ref[0]
    qis = jnp.concatenate([qi[:, h * IDX_DIM:(h + 1) * IDX_DIM] for h in range(IDX_HEADS)], axis=0)
    wi = wi_ref[0] * IDX_HEADS ** -0.5
    qpos = i * Q_BLOCK + lax.broadcasted_iota(I32, (Q_BLOCK, kt), 0)
    lane = lax.broadcasted_iota(I32, (Q_BLOCK, kt), 1)

    def scores(t, carry):
        off = pl.multiple_of(t * kt, kt)
        s = lax.dot_general(qis, ki_ref[0, pl.ds(off, kt), :], _NT, preferred_element_type=F32)
        acc = jnp.zeros((Q_BLOCK, kt), F32)
        for h in range(IDX_HEADS):
            acc = acc + wi[:, h:h + 1] * jnp.maximum(s[h * Q_BLOCK:(h + 1) * Q_BLOCK] * IDX_DIM ** -0.5, 0.0)
        key_ref[:, pl.ds(off, kt)] = jnp.where(off + lane <= qpos, _f2key(acc), INT_MIN)
        return carry

    lax.fori_loop(0, n_kt, scores, 0)
    thr, jmax = _select_bounds(key_ref, n_kt, Q_BLOCK, kt, n_sel, idx_bits)
    thr = jnp.maximum(thr, INT_MIN + 1)

    q = q_ref[0].astype(F32) * HEAD_DIM ** -0.5
    grp = lax.broadcasted_iota(I32, (Q_BLOCK, KV_W), 1) >> 6
    rep = ATTN_HEADS // KV_HEADS

    def expand(h):
        qh = q[:, h * HEAD_DIM:(h + 1) * HEAD_DIM]
        return jnp.where(grp == h // rep, jnp.concatenate([qh] * KV_HEADS, axis=1), 0.0).astype(BF16)

    qe = [expand(h) for h in range(ATTN_HEADS)]
    m_ref[...] = jnp.full(m_ref.shape, NEG_BIG, F32)
    l_ref[...] = jnp.zeros(l_ref.shape, F32)
    acc_ref[...] = jnp.zeros(acc_ref.shape, F32)

    def attend(t, carry):
        off = pl.multiple_of(t * kt, kt)
        key = key_ref[:, pl.ds(off, kt)]
        kidx = off + lane
        sel = (key > thr) | ((key == thr) & (kidx <= jmax))
        kt_tile = k_ref[0, pl.ds(off, kt), :]
        vt_tile = v_ref[0, pl.ds(off, kt), :]
        qk = lambda h: lax.dot_general(qe[h], kt_tile, _NT, preferred_element_type=F32)

        def finish(h, p, alpha):
            acc_ref[h] = alpha * acc_ref[h] + jnp.dot(p, vt_tile, preferred_element_type=F32)

        nxt = qk(0)
        pending = None
        for h in range(ATTN_HEADS):
            lg = nxt
            if h + 1 < ATTN_HEADS:
                nxt = qk(h + 1)
            lg = jnp.where(sel, lg, -jnp.inf)
            m = m_ref[h]
            mn = jnp.maximum(m, jnp.max(lg, axis=-1, keepdims=True))
            p = jnp.exp(lg - mn)
            alpha = jnp.exp(m - mn)
            l_ref[h] = alpha * l_ref[h] + jnp.sum(p, axis=-1, keepdims=True)
            m_ref[h] = mn
            if pending is not None:
                finish(*pending)
            pending = (h, p.astype(BF16), alpha)
        finish(*pending)
        return carry

    lax.fori_loop(0, n_kt, attend, 0)
    pieces = []
    for h in range(ATTN_HEADS):
        g = h // rep
        pieces.append(acc_ref[h][:, g * HEAD_DIM:(g + 1) * HEAD_DIM] / l_ref[h])
    o_ref[0] = jnp.concatenate(pieces, axis=1)


def _dsa_prompt(q, qi, wi, k, v, ki, t_real):
    bsz, tp, _ = q.shape
    kt = 640 if tp % 640 == 0 else LANE
    n_sel = min(TOPK_MAX, t_real // 4)
    idx_bits = max(1, (tp - 1).bit_length())
    blk = lambda w: pl.BlockSpec((1, Q_BLOCK, w), lambda b, i: (b, i, 0))
    seq = lambda w: pl.BlockSpec((1, tp, w), lambda b, i: (b, 0, 0))
    return pl.pallas_call(
        functools.partial(_dsa_prompt_kernel, kt, n_sel, idx_bits),
        grid=(bsz, tp // Q_BLOCK),
        in_specs=[blk(ATTN_WIDTH), blk(IDX_HEADS * IDX_DIM), blk(IDX_HEADS),
                  seq(KV_W), seq(KV_W), seq(IDX_DIM)],
        out_specs=blk(ATTN_WIDTH),
        out_shape=jax.ShapeDtypeStruct((bsz, tp, ATTN_WIDTH), F32),
        scratch_shapes=[pltpu.VMEM((Q_BLOCK, tp), I32),
                        pltpu.VMEM((ATTN_HEADS, Q_BLOCK, 1), F32),
                        pltpu.VMEM((ATTN_HEADS, Q_BLOCK, 1), F32),
                        pltpu.VMEM((ATTN_HEADS, Q_BLOCK, KV_W), F32)],
        compiler_params=_cparams("parallel", "arbitrary"),
        name="dsa_prompt",
    )(q, qi, wi, k, v, ki)


def _dsa_s_scores_kernel(pps, pt_ref, qi_ref, wi_ref, *refs):
    ci_refs, o_ref = refs[:pps], refs[pps]
    qi = qi_ref[0]
    w = wi_ref[0] * IDX_HEADS ** -0.5
    ki = jnp.concatenate([ci_refs[u][0, 0] for u in range(pps)], axis=0)
    s = _dot3(qi, ki, _NT)
    o_ref[0] = jnp.sum(w * jnp.maximum(s * IDX_DIM ** -0.5, 0.0), axis=0, keepdims=True)


def _dsa_s_scores(page_table, qi3, wi3, cache_idx, layer):
    bd, npages = page_table.shape
    pps = PAGES_PER_STEP
    page = lambda u: pl.BlockSpec((1, 1, PAGE_SIZE, IDX_DIM),
                                  lambda b, p, pt: (layer, pt[b * npages + p * pps + u], 0, 0))
    gs = pltpu.PrefetchScalarGridSpec(
        num_scalar_prefetch=1,
        grid=(bd, npages // pps),
        in_specs=[pl.BlockSpec((1, IDX_HEADS, IDX_DIM), lambda b, p, pt: (b, 0, 0)),
                  pl.BlockSpec((1, IDX_HEADS, 1), lambda b, p, pt: (b, 0, 0))] + [page(u) for u in range(pps)],
        out_specs=pl.BlockSpec((1, 1, pps * PAGE_SIZE), lambda b, p, pt: (b, 0, p)),
    )
    return pl.pallas_call(
        functools.partial(_dsa_s_scores_kernel, pps),
        grid_spec=gs,
        out_shape=jax.ShapeDtypeStruct((bd, 1, npages * PAGE_SIZE), F32),
        compiler_params=_cparams("parallel", "arbitrary"),
        name="dsa_decode_scores",
    )(page_table.reshape(-1), qi3, wi3, *([cache_idx] * pps))


def _dsa_s_bounds_kernel(past, n_sel, idx_bits, sc_ref, qi_ref, kiw_ref, hm_ref, thr_o, j_o, kn_o, key_ref):
    rows = sc_ref.shape[0]
    kiw = kiw_ref[...]
    lane = lax.broadcasted_iota(I32, kiw.shape, 1)
    rolled = pltpu.roll(kiw, IDX_DIM, 1)
    ki2 = jnp.where(lane < IDX_DIM, kiw, rolled)
    w8 = jnp.where(lane < IDX_HEADS, rolled, 0.0) * IDX_HEADS ** -0.5
    prod = qi_ref[...] * jnp.concatenate([ki2] * (IDX_HEADS // 2), axis=1)
    s = _dot3(prod, hm_ref[...])
    new = jnp.sum(w8 * jnp.maximum(s * IDX_DIM ** -0.5, 0.0), axis=-1, keepdims=True)
    knew = _f2key(new)
    key_ref[:, 0:past] = _f2key(sc_ref[...])
    key_ref[:, past:past + LANE] = jnp.where(lane == 0, knew, INT_MIN)
    thr, jmax = _select_bounds(key_ref, (past + LANE) // LANE, rows, LANE, n_sel, idx_bits)
    thr_o[...] = thr
    j_o[...] = jmax
    kn_o[...] = knew


def _dsa_s_bounds(scores, qi, kiw, hm, past):
    rows = scores.shape[0]
    n_sel = min(TOPK_MAX, (past + 1) // 4)
    idx_bits = (past + LANE - 1).bit_length()
    out = jax.ShapeDtypeStruct((rows, 1), I32)
    return pl.pallas_call(
        functools.partial(_dsa_s_bounds_kernel, past, n_sel, idx_bits),
        out_shape=[out, out, out],
        scratch_shapes=[pltpu.VMEM((rows, past + LANE), I32)],
        compiler_params=pltpu.CompilerParams(vmem_limit_bytes=VMEM_LIMIT),
        name="dsa_decode_bounds",
    )(scores, qi, kiw, hm)


def _dsa_s_attend_kernel(pps, past, pt_ref, thr_ref, j_ref, kn_ref, qe_ref, sc_ref, kn_row_ref, vn_row_ref,
                         fold_ref, *refs):
    ck, cv, o_ref = refs[:pps], refs[pps:2 * pps], refs[2 * pps]
    m_ref, l_ref, acc_ref = refs[2 * pps + 1:]
    b = pl.program_id(0)
    p = pl.program_id(1)

    @pl.when(p == 0)
    def _():
        m_ref[...] = jnp.full(m_ref.shape, NEG_BIG, F32)
        l_ref[...] = jnp.zeros(l_ref.shape, F32)
        acc_ref[...] = jnp.zeros(acc_ref.shape, F32)

    thr, jmax = thr_ref[b], j_ref[b]
    qe = qe_ref[0]
    qeb = qe.astype(BF16)
    kidx = p * pps * PAGE_SIZE + lax.broadcasted_iota(I32, (1, pps * PAGE_SIZE), 1)
    kb = jnp.concatenate([ck[u][0, 0].astype(BF16) for u in range(pps)], axis=0)
    vb = jnp.concatenate([cv[u][0, 0].astype(BF16) for u in range(pps)], axis=0)
    lg = lax.dot_general(qeb, kb, _NT, preferred_element_type=F32) * HEAD_DIM ** -0.5
    key = _f2key(sc_ref[0])
    sel = (key > thr) | ((key == thr) & (kidx <= jmax))
    lg = jnp.where(sel, lg, NEG_BIG)
    m = m_ref[...]
    mn = jnp.maximum(m, jnp.max(lg, axis=-1, keepdims=True))
    pr = jnp.where(sel, jnp.exp(lg - mn), 0.0)
    alpha = jnp.exp(m - mn)
    l_ref[...] = alpha * l_ref[...] + jnp.sum(pr, axis=-1, keepdims=True)
    acc_ref[...] = alpha * acc_ref[...] + jnp.dot(pr.astype(BF16), vb, preferred_element_type=F32)
    m_ref[...] = mn

    @pl.when(p == pl.num_programs(1) - 1)
    def _():
        knew = kn_ref[b]
        sel_new = (knew > thr) | ((knew == thr) & (past <= jmax))
        lg = jnp.sum(qe * kn_row_ref[0], axis=-1, keepdims=True) * HEAD_DIM ** -0.5
        m = m_ref[...]
        mn = jnp.where(sel_new, jnp.maximum(m, lg), m)
        pr = jnp.where(sel_new, jnp.exp(lg - mn), 0.0)
        alpha = jnp.exp(m - mn)
        l = alpha * l_ref[...] + pr
        acc = alpha * acc_ref[...] + pr * vn_row_ref[0]
        rowi = lax.broadcasted_iota(I32, acc.shape, 0)
        lanei = lax.broadcasted_iota(I32, acc.shape, 1)
        own = jnp.where((lanei >> 6) == (rowi >> 1), acc / l, 0.0)
        o_ref[0] = _dot_sel(own, fold_ref[...])


def _dsa_s_attend(page_table, thr, jmax, knew, qe, scores, k_new, v_new, fold, cache_k, cache_v, layer):
    bd, npages = page_table.shape
    past = npages * PAGE_SIZE
    pps = PAGES_PER_STEP
    page = lambda u: pl.BlockSpec((1, 1, PAGE_SIZE, KV_W),
                                  lambda b, p, pt, t, j, kn: (layer, pt[b * npages + p * pps + u], 0, 0))
    per_b = lambda s: pl.BlockSpec((1,) + s, lambda b, p, pt, t, j, kn: (b, 0, 0))
    gs = pltpu.PrefetchScalarGridSpec(
        num_scalar_prefetch=4,
        grid=(bd, npages // pps),
        in_specs=[per_b((ATTN_HEADS, KV_W)),
                  pl.BlockSpec((1, 1, pps * PAGE_SIZE), lambda b, p, pt, t, j, kn: (b, 0, p)),
                  per_b((1, KV_W)), per_b((1, KV_W)),
                  pl.BlockSpec(fold.shape, lambda b, p, pt, t, j, kn: (0, 0))]
                 + [page(u) for u in range(pps)] * 2,
        out_specs=per_b((ATTN_HEADS, HEAD_DIM)),
        scratch_shapes=[pltpu.VMEM((ATTN_HEADS, 1), F32), pltpu.VMEM((ATTN_HEADS, 1), F32),
                        pltpu.VMEM((ATTN_HEADS, KV_W), F32)],
    )
    return pl.pallas_call(
        functools.partial(_dsa_s_attend_kernel, pps, past),
        grid_spec=gs,
        out_shape=jax.ShapeDtypeStruct((bd, ATTN_HEADS, HEAD_DIM), F32),
        compiler_params=_cparams("parallel", "arbitrary"),
        name="dsa_decode_attend",
    )(page_table.reshape(-1), thr.reshape(-1), jmax.reshape(-1), knew.reshape(-1),
      qe, scores, k_new, v_new, fold, *([cache_k] * pps), *([cache_v] * pps))


def _merge_ln_kernel(o_ref, bonus_ref, g_ref, at_ref, x_ref, wt_ref, wb_ref, xg_ref, xb_ref,
                     lg_ref, lb_ref, bd_ref, out_ref):
    bd = bd_ref[...]
    o = o_ref[...]
    mean = _dot_sel(o, bd) * (1.0 / HEAD_DIM)
    oc = o - mean
    var = _dot_sel(oc * oc, bd) * (1.0 / HEAD_DIM)
    rw = (oc * lax.rsqrt(var + GN_EPS) * xg_ref[...] + xb_ref[...] + bonus_ref[...]) * g_ref[...]
    f = (jnp.dot(rw.astype(BF16), wt_ref[...], preferred_element_type=F32)
         + jnp.dot(at_ref[...].astype(BF16), wb_ref[...], preferred_element_type=F32))
    out_ref[...] = _layer_norm(DEEPNORM_ALPHA * x_ref[...] + f, lg_ref[...], lb_ref[...])


def _merge_ln(o, bonus, g, at, x, wts, tm):
    n = x.shape[0]
    tm = min(tm, n)
    tok = lambda w: pl.BlockSpec((tm, w), lambda i: (i, 0))
    full = lambda a: pl.BlockSpec(a.shape, lambda i: (0, 0))
    ws = [wts["wo_top"], wts["wo_bot"], wts["lnx_g"], wts["lnx_b"], wts["ln1_g"], wts["ln1_b"], wts["bd"]]
    return pl.pallas_call(
        _merge_ln_kernel,
        grid=(n // tm,),
        in_specs=[tok(RWKV_WIDTH)] * 3 + [tok(ATTN_WIDTH), tok(D_MODEL)] + [full(a) for a in ws],
        out_specs=tok(D_MODEL),
        out_shape=jax.ShapeDtypeStruct((n, D_MODEL), F32),
        compiler_params=_cparams("parallel"),
        name="merge_ln",
    )(o, bonus, g, at, x, *ws)


def _take_top(src_ref, n_rows, val_ref, idx_ref):
    shape = src_ref.shape
    row = lax.broadcasted_iota(I32, shape, 1)

    def body(a, carry):
        sv = src_ref[...]
        m = jnp.max(sv, axis=1, keepdims=True)
        idx = jnp.min(jnp.where(sv == m, row, n_rows), axis=1, keepdims=True)
        val_ref[:, pl.ds(a, 1), :] = m
        idx_ref[:, pl.ds(a, 1), :] = idx
        src_ref[...] = jnp.where(row == idx, -jnp.inf, sv)
        return carry

    lax.fori_loop(0, PEER_TOPK, body, 0)


def _peer_route_kernel(x_ref, wq_ref, sk0_ref, sk1_ref, e_o, g_o, s_ref, t_ref, i_ref, c_ref, ts_ref, ic_ref):
    q = jnp.dot(x_ref[...].astype(BF16), wq_ref[...], preferred_element_type=F32)
    for h in range(PEER_HEADS):
        qh = q[:, h * PEER_DKEY:(h + 1) * PEER_DKEY]
        s_ref[h] = _dot3(sk0_ref[...], qh, _NT)
        s_ref[PEER_HEADS + h] = _dot3(sk1_ref[...], qh, _NT)
    _take_top(s_ref, N_KEYS, t_ref, i_ref)
    t1, t2 = t_ref[0:PEER_HEADS], t_ref[PEER_HEADS:2 * PEER_HEADS]
    c_ref[...] = jnp.concatenate([t1[:, a:a + 1, :] + t2 for a in range(PEER_TOPK)], axis=1)
    _take_top(c_ref, PEER_TOPK * PEER_TOPK, ts_ref, ic_ref)
    ic = ic_ref[...]
    i1, i2 = i_ref[0:PEER_HEADS], i_ref[PEER_HEADS:2 * PEER_HEADS]
    ia, ib = ic >> 4, ic & (PEER_TOPK - 1)
    e = jnp.zeros(ic.shape, I32)
    for a in range(PEER_TOPK):
        e = e + jnp.where(ia == a, i1[:, a:a + 1, :] * N_KEYS, 0) + jnp.where(ib == a, i2[:, a:a + 1, :], 0)
    ts = ts_ref[...]
    ex = jnp.exp(ts - jnp.max(ts, axis=1, keepdims=True))
    e_o[0] = e
    g_o[0] = ex / jnp.sum(ex, axis=1, keepdims=True)


def _peer_route(x, wq, sk0, sk1, tt):
    n = x.shape[0]
    nt = n // tt
    full = lambda a: pl.BlockSpec(a.shape, lambda i: (0, 0))
    out = pl.BlockSpec((1, PEER_HEADS, PEER_TOPK, tt), lambda i: (i, 0, 0, 0))
    return pl.pallas_call(
        _peer_route_kernel,
        grid=(nt,),
        in_specs=[pl.BlockSpec((tt, D_MODEL), lambda i: (i, 0)), full(wq), full(sk0), full(sk1)],
        out_specs=[out, out],
        out_shape=[jax.ShapeDtypeStruct((nt, PEER_HEADS, PEER_TOPK, tt), I32),
                   jax.ShapeDtypeStruct((nt, PEER_HEADS, PEER_TOPK, tt), F32)],
        scratch_shapes=[pltpu.VMEM((2 * PEER_HEADS, N_KEYS, tt), F32),
                        pltpu.VMEM((2 * PEER_HEADS, PEER_TOPK, tt), F32),
                        pltpu.VMEM((2 * PEER_HEADS, PEER_TOPK, tt), I32),
                        pltpu.VMEM((PEER_HEADS, PEER_TOPK * PEER_TOPK, tt), F32),
                        pltpu.VMEM((PEER_HEADS, PEER_TOPK, tt), F32),
                        pltpu.VMEM((PEER_HEADS, PEER_TOPK, tt), I32)],
        compiler_params=_cparams("parallel"),
        name="peer_route",
    )(x, wq, sk0, sk1)


def _peer_gather_kernel(tt, n_valid, idx_hbm, x_ref, g_ref, uv_hbm, lg_ref, lb_ref, o_ref, idx_smem, *rest):
    bufs, (sem, isem, y_ref) = rest[:GATHER_BUFFERS], rest[GATHER_BUFFERS:]
    i = pl.program_id(0)
    n_idx = PEER_SLOTS * tt
    islot = i % 2

    def idx_copy(tile, slot):
        return pltpu.make_async_copy(idx_hbm.at[tile], idx_smem.at[pl.ds(slot * n_idx, n_idx)], isem.at[slot])

    @pl.when(i == 0)
    def _():
        idx_copy(0, 0).start()

    if n_valid < tt:
        y_ref[...] = jnp.zeros(y_ref.shape, F32)
    idx_copy(i, islot).wait()

    @pl.when(i + 1 < pl.num_programs(0))
    def _():
        idx_copy(i + 1, 1 - islot).start()

    base = islot * n_idx

    def issue(t, k):
        tok = base + t * PEER_SLOTS
        for s in range(PEER_SLOTS):
            e = idx_smem[tok + s]
            pltpu.async_copy(uv_hbm.at[pl.ds(e, 1)], bufs[k].at[pl.ds(s, 1)], sem.at[k], priority=s % 2)

    def wait(k):
        pltpu.make_async_copy(uv_hbm.at[pl.ds(0, PEER_SLOTS)], bufs[k], sem.at[k]).wait()

    lane = lax.broadcasted_iota(I32, (PEER_SLOTS, tt), 1)

    def compute(t, k):
        xrow = x_ref[pl.ds(t, 1), :]
        word = bufs[k][...]
        u = lax.bitcast_convert_type(word << 16, F32)
        v = lax.bitcast_convert_type(word & jnp.int32(-65536), F32)
        h = jnp.sum(u * xrow, axis=-1, keepdims=True)
        gate = jnp.sum(jnp.where(lane == t, g_ref[0], 0.0), axis=-1, keepdims=True)
        coef = gate * jax.nn.gelu(h)
        y_ref[pl.ds(t, 1), :] = jnp.sum(coef * v, axis=0, keepdims=True)

    depth = len(bufs)
    ahead = depth - 1
    rounds = max(n_valid - ahead, 0) // depth
    for t in range(min(ahead, n_valid)):
        issue(t, t % depth)

    def body(r, carry):
        for k in range(depth):
            t = r * depth + k
            wait(k)
            issue(t + ahead, (k + ahead) % depth)
            compute(t, k)
        return carry

    lax.fori_loop(0, rounds, body, 0)
    for t in range(rounds * depth, n_valid):
        wait(t % depth)
        if t + ahead < n_valid:
            issue(t + ahead, (t + ahead) % depth)
        compute(t, t % depth)
    o_ref[...] = _layer_norm(DEEPNORM_ALPHA * x_ref[...] + y_ref[...], lg_ref[...], lb_ref[...])


def _peer_gather(idx, gates, x, uv, ln_g, ln_b, tt, n_valid):
    n = x.shape[0]
    nt = n // tt
    row = pl.BlockSpec((1, D_MODEL), lambda i: (0, 0))
    return pl.pallas_call(
        functools.partial(_peer_gather_kernel, tt, n_valid),
        grid=(nt,),
        in_specs=[pl.BlockSpec(memory_space=pl.ANY),
                  pl.BlockSpec((tt, D_MODEL), lambda i: (i, 0)),
                  pl.BlockSpec((1, PEER_SLOTS, tt), lambda i: (i, 0, 0)),
                  pl.BlockSpec(memory_space=pl.ANY), row, row],
        out_specs=pl.BlockSpec((tt, D_MODEL), lambda i: (i, 0)),
        out_shape=jax.ShapeDtypeStruct((n, D_MODEL), F32),
        scratch_shapes=[pltpu.SMEM((2 * PEER_SLOTS * tt,), I32)]
                       + [pltpu.VMEM((PEER_SLOTS, D_MODEL), I32)] * GATHER_BUFFERS
                       + [pltpu.SemaphoreType.DMA((GATHER_BUFFERS,)),
                          pltpu.SemaphoreType.DMA((2,)),
                          pltpu.VMEM((tt, D_MODEL), F32)],
        compiler_params=_cparams("arbitrary"),
        name="peer_gather",
    )(idx, x, gates, uv, ln_g, ln_b)


def _peer(x, wts, tt, n_valid):
    e, gates = _peer_route(x, wts["wq"], wts["sk0"], wts["sk1"], tt)
    nt = x.shape[0] // tt
    e_tok = e.reshape(nt, PEER_SLOTS, tt).transpose(0, 2, 1)
    return _peer_gather(e_tok.reshape(nt, PEER_SLOTS * tt), gates.reshape(nt, PEER_SLOTS, tt), x,
                        wts["uv"], wts["ln2_g"], wts["ln2_b"], tt, n_valid)


def _pack_bf16_pair(lo, hi):
    bits = lambda a: lax.bitcast_convert_type(a.astype(BF16), jnp.uint16).astype(jnp.uint32)
    return lax.bitcast_convert_type(bits(lo) | (bits(hi) << 16), I32)


def _layer_weights(l, w_in, shift_mu, decay_w0, decay_up, iclr_a0, iclr_up, gate_up, k_k, k_a, r_k,
                   lnx_g, lnx_b, idx_ln_g, idx_ln_b, w_out, ln1_g, ln1_b, ln2_g, ln2_b,
                   peer_wq, peer_subkeys, peer_u, peer_v):
    row = lambda a: a.reshape(1, -1).astype(F32)
    lora = jnp.zeros((LORA_W, 3 * RWKV_WIDTH), F32)
    lora = lora.at[0:W_LORA, 0:RWKV_WIDTH].set(decay_up[l])
    lora = lora.at[W_LORA:W_LORA + A_LORA, RWKV_WIDTH:2 * RWKV_WIDTH].set(iclr_up[l])
    lora = lora.at[W_LORA + A_LORA:, 2 * RWKV_WIDTH:].set(gate_up[l])
    lora_hi = lora.astype(BF16)
    pad_lane = lambda a: jnp.pad(a.reshape(1, -1), ((0, 0), (0, LANE - a.size)))
    half = PEER_DKEY // 2
    seg = jnp.arange(RWKV_WIDTH) // HEAD_DIM
    return dict(
        w_rw=w_in[l][:, :SHIFT_W].astype(BF16),
        w_at=jnp.pad(w_in[l][:, SHIFT_W:], ((0, 0), (0, AT_PAD_W - ATTN_PROJ_W))).astype(BF16),
        mu=row(shift_mu[l]), w0=row(decay_w0[l]), a0=row(iclr_a0[l]),
        lora_hi=lora_hi, lora_lo=(lora - lora_hi.astype(F32)).astype(BF16),
        k_k=row(k_k[l]), k_a=row(k_a[l]), r_k=row(r_k[l]),
        bd=(seg[:, None] == seg[None, :]).astype(BF16),
        lnx_g=row(lnx_g[l]), lnx_b=row(lnx_b[l]),
        idx_g=pad_lane(idx_ln_g[l]), idx_b=pad_lane(idx_ln_b[l]),
        wo_top=w_out[l][:RWKV_WIDTH].astype(BF16), wo_bot=w_out[l][RWKV_WIDTH:].astype(BF16),
        ln1_g=row(ln1_g[l]), ln1_b=row(ln1_b[l]), ln2_g=row(ln2_g[l]), ln2_b=row(ln2_b[l]),
        wq=peer_wq[l].astype(BF16),
        sk0=jnp.pad(peer_subkeys[l, 0], ((0, 0), (0, half))),
        sk1=jnp.pad(peer_subkeys[l, 1], ((0, 0), (half, 0))),
        uv=_pack_bf16_pair(peer_u[l], peer_v[l]),
    )


def _tile(n, cap):
    if n <= cap:
        return n
    return max(d for d in range(8, cap + 1, 8) if n % d == 0)


def _mixer_front(x, wts, tabs):
    tm = _tile(x.shape[0], 512)
    p_rw = _matmul(x, wts["w_rw"], tm)
    p_at = _matmul(x, wts["w_at"], tm)
    q, k, qi, kiw = _attn_prep(p_at, tabs, wts["idx_g"], wts["idx_b"], _tile(tabs[0].shape[0], 640))
    v = p_at[:, ATTN_WIDTH + KV_W:ATTN_WIDTH + 2 * KV_W]
    return p_rw, q, k, v, qi, kiw


def kernel(x_prompt, x_sample, cache_k, cache_v, cache_idx_k, state_wkv, state_shift, page_table, meta_tokens, w_in, shift_mu, decay_w0, decay_up, iclr_a0, iclr_up, gate_up, k_k, k_a, r_k, lnx_g, lnx_b, idx_ln_g, idx_ln_b, w_out, ln1_g, ln1_b, ln2_g, ln2_b, peer_wq, peer_subkeys, peer_u, peer_v):
    bsz, seq, _ = x_prompt.shape
    t = seq + N_META
    tp = -(-t // LANE) * LANE
    n_p = bsz * tp
    depth = w_in.shape[0]
    bd_, ts_, _ = x_sample.shape
    assert ts_ == 1
    npages = page_table.shape[1]
    past = npages * PAGE_SIZE
    n_pool = cache_k.shape[1]
    peer_tt = LANE
    ns_pad = -(-bd_ // peer_tt) * peer_tt

    xp = jnp.concatenate([jnp.broadcast_to(meta_tokens[None], (bsz, N_META, D_MODEL)), x_prompt], axis=1)
    xp = jnp.pad(xp, ((0, 0), (0, tp - t), (0, 0))).reshape(n_p, D_MODEL)
    xs = x_sample.reshape(bd_, D_MODEL)
    tabs_p = _rope_tables(jnp.arange(tp, dtype=I32))
    tabs_s = _rope_tables(jnp.full((bd_,), past, I32))
    ck = cache_k.reshape(depth, n_pool, PAGE_SIZE, KV_W)
    cv = cache_v.reshape(depth, n_pool, PAGE_SIZE, KV_W)
    hsel = jnp.arange(IDX_HEADS * IDX_DIM)[:, None] // IDX_DIM == jnp.arange(LANE)[None, :]
    hm = hsel.astype(F32)
    fold = (jnp.arange(KV_W)[:, None] % HEAD_DIM == jnp.arange(HEAD_DIM)[None, :]).astype(BF16)
    own = (jnp.arange(KV_W)[None, :] // HEAD_DIM == jnp.arange(ATTN_HEADS)[:, None] // (ATTN_HEADS // KV_HEADS))

    k_p, v_p, ki_p, wkv_p, sh_p = [], [], [], [], []
    k_s, v_s, ki_s, wkv_s, sh_s = [], [], [], [], []
    for l in range(depth):
        wts = _layer_weights(l, w_in, shift_mu, decay_w0, decay_up, iclr_a0, iclr_up, gate_up, k_k, k_a, r_k,
                             lnx_g, lnx_b, idx_ln_g, idx_ln_b, w_out, ln1_g, ln1_b, ln2_g, ln2_b,
                             peer_wq, peer_subkeys, peer_u, peer_v)

        p_rw, q, k, v, qi, kiw = _mixer_front(xp, wts, tabs_p)
        p_rw3 = p_rw.reshape(bsz, tp, SHIFT_W)
        r_, w_, kt_, kk_, b_, vv_, bonus, g = _rwkv_prep(
            p_rw3, jnp.zeros((bsz, 1, SHIFT_W), F32), t, True, wts, LANE)
        o, s_fin = _wkv(r_, w_, kt_, kk_, b_, vv_,
                        jnp.zeros((bsz, RWKV_HEADS, HEAD_DIM, HEAD_DIM), F32), bsz, LANE)
        three = lambda a: a.reshape(bsz, tp, -1)
        at = _dsa_prompt(three(q).astype(BF16), three(qi).astype(BF16),
                         three(kiw)[:, :, IDX_DIM:IDX_DIM + IDX_HEADS],
                         three(k).astype(BF16), three(v).astype(BF16),
                         three(kiw)[:, :, :IDX_DIM].astype(BF16), t)
        flat = lambda a: a.reshape(n_p, -1)
        x1 = _merge_ln(flat(o), flat(bonus), flat(g), flat(at), xp, wts, _tile(n_p, 256))
        xp = _peer(x1, wts, peer_tt, peer_tt)
        k_p.append(three(k)[:, :t].reshape(bsz, t, KV_HEADS, HEAD_DIM))
        v_p.append(three(v)[:, :t].reshape(bsz, t, KV_HEADS, HEAD_DIM))
        ki_p.append(three(kiw)[:, :t, :IDX_DIM])
        wkv_p.append(s_fin)
        sh_p.append(p_rw3[:, t - 1])

        p_rw, q, k, v, qi, kiw = _mixer_front(xs, wts, tabs_s)
        r_, w_, kt_, kk_, b_, vv_, bonus, g = _rwkv_prep(
            p_rw[None], state_shift[l][None], bd_, False, wts, bd_)
        tc_s = 8
        padt = lambda a, c: jnp.pad(a[0][:, None, :], ((0, 0), (0, tc_s - 1), (0, 0)), constant_values=c)
        o, s_fin = _wkv(padt(r_, 0.0), padt(w_, 1.0), padt(kt_, 0.0), padt(kk_, 0.0), padt(b_, 0.0),
                        padt(vv_, 0.0), state_wkv[l].astype(F32), 4, tc_s)
        o = o[:, 0]
        scores = _dsa_s_scores(page_table, qi.reshape(bd_, IDX_HEADS, IDX_DIM),
                               kiw[:, IDX_DIM:IDX_DIM + IDX_HEADS].reshape(bd_, IDX_HEADS, 1),
                               cache_idx_k, l)
        thr, jmax, knew = _dsa_s_bounds(scores.reshape(bd_, past), qi, kiw, hm, past)
        qe = jnp.where(own[None], jnp.tile(q.reshape(bd_, ATTN_HEADS, HEAD_DIM), (1, 1, KV_HEADS)), 0.0)
        at = _dsa_s_attend(page_table, thr, jmax, knew, qe, scores, k[:, None, :], v[:, None, :], fold,
                           ck, cv, l).reshape(bd_, ATTN_WIDTH)
        x1 = _merge_ln(o, bonus[0], g[0], at, xs, wts, bd_)
        x1p = jnp.pad(x1, ((0, ns_pad - bd_), (0, 0)))
        xs = _peer(x1p, wts, peer_tt, bd_)[:bd_]
        k_s.append(k.reshape(bd_, 1, KV_HEADS, HEAD_DIM))
        v_s.append(v.reshape(bd_, 1, KV_HEADS, HEAD_DIM))
        ki_s.append(kiw[:, None, :IDX_DIM])
        wkv_s.append(s_fin)
        sh_s.append(p_rw)

    y_prompt = xp.reshape(bsz, tp, D_MODEL)[:, N_META:t]
    y_sample = xs.reshape(bd_, 1, D_MODEL)
    return (y_prompt, y_sample, jnp.stack(k_p), jnp.stack(v_p), jnp.stack(ki_p),
            jnp.stack(wkv_p).astype(state_wkv.dtype), jnp.stack(sh_p).astype(state_shift.dtype),
            jnp.stack(k_s), jnp.stack(v_s), jnp.stack(ki_s),
            jnp.stack(wkv_s).astype(state_wkv.dtype), jnp.stack(sh_s).astype(state_shift.dtype))
```
